```python
import jax, jax.numpy as jnp
from jax import lax
import numpy as np

D_MODEL = 1024
BATCH = 8
SEQ = 2048
DEPTH = 2
DEC_BATCH = 128
DEC_SEQ = 4
PAST_LEN = 2048
PAGE_SIZE = 128

HEAD_DIM = 64
BLK = 128
ROPE_THETA = 10000.0
LN_EPS = 1e-5
RMS_EPS = 1e-6
TINY = 1e-30
ALPHA = (2 * DEPTH) ** 0.25
BETA = (8 * DEPTH) ** -0.25
N_EVEN = (DEPTH + 1) // 2
N_ODD = DEPTH // 2

HG_HEADS = 8
HG_DK = 64
HG_DV = 64
HG_CHUNK = 64

NSA_HEADS = 8
NSA_KV_HEADS = 2
NSA_GROUP = NSA_HEADS // NSA_KV_HEADS
CMP_BLOCK = 32
SEL_BLOCK = 64
SEL_RATIO = SEL_BLOCK // CMP_BLOCK
SEL_TOPN = 8
SEL_FORCE = 1e4
NSA_WINDOW = 512

EVEN_SPLITS = (HG_HEADS * HG_DK, HG_HEADS * HG_DK, HG_HEADS * HG_DV, HG_HEADS * HG_DV,
               NSA_HEADS * HEAD_DIM, 2 * NSA_KV_HEADS * HEAD_DIM, 2 * NSA_KV_HEADS * HEAD_DIM,
               2 * NSA_KV_HEADS * HEAD_DIM, 3 * NSA_HEADS)
EVEN_IN = sum(EVEN_SPLITS)
EVEN_MIX = HG_HEADS * HG_DV + NSA_HEADS * HEAD_DIM

DIL_HEADS = 16
DIL_CONFIGS = ((128, 1), (512, 4), (2048, 16))
DIL_MAX_WINDOW = 2048
ODD_IN = 3 * DIL_HEADS * HEAD_DIM
ODD_MIX = DIL_HEADS * HEAD_DIM

PEER_HEADS = 8
PEER_NKEYS = 128
PEER_EXPERTS = PEER_NKEYS * PEER_NKEYS
PEER_TOPK = 16
PEER_QDIM = 256
PEER_BLOCK = 128

kernel_name = 'hybrid_hgrn2_nsa_dilated_peer_step'


def layer_norm(x, g, b):
    xf = x.astype(jnp.float32)
    mu = jnp.mean(xf, -1, keepdims=True)
    var = jnp.mean(jnp.square(xf - mu), -1, keepdims=True)
    return ((xf - mu) * lax.rsqrt(var + LN_EPS) * g + b).astype(x.dtype)


def rope(x, pos):
    half = x.shape[-1] // 2
    inv = ROPE_THETA ** (-jnp.arange(half, dtype=jnp.float32) / half)
    ang = pos.astype(jnp.float32)[:, None] * inv[None, :]
    cos = jnp.cos(ang)[None, :, None, :]
    sin = jnp.sin(ang)[None, :, None, :]
    x1 = x[..., :half].astype(jnp.float32)
    x2 = x[..., half:].astype(jnp.float32)
    return jnp.concatenate([x1 * cos - x2 * sin, x1 * sin + x2 * cos], -1).astype(x.dtype)


def masked_softmax(s, mask):
    s = jnp.where(mask, s.astype(jnp.float32), -jnp.inf)
    m = jnp.max(s, axis=-1, keepdims=True)
    m = jnp.where(jnp.isfinite(m), m, 0.0)
    e = jnp.exp(s - m)
    l = jnp.sum(e, axis=-1, keepdims=True)
    return m, l, e


def band_attn(q, k, v, window):
    B, T, Hk, G, D = q.shape
    back = -(-window // BLK)
    nb = -(-T // BLK)
    Tp = nb * BLK
    qb = jnp.pad(q, ((0, 0), (0, Tp - T), (0, 0), (0, 0), (0, 0))).reshape(B, nb, BLK, Hk, G, D)
    padk = ((0, 0), (back * BLK, Tp - T), (0, 0), (0, 0))
    kb = jnp.pad(k, padk).reshape(B, nb + back, BLK, Hk, D)
    vb = jnp.pad(v, padk).reshape(B, nb + back, BLK, Hk, D)
    widx = jnp.arange(nb)[:, None] + jnp.arange(back + 1)[None, :]
    kw = kb[:, widx].reshape(B, nb, (back + 1) * BLK, Hk, D)
    vw = vb[:, widx].reshape(B, nb, (back + 1) * BLK, Hk, D)
    qpos = jnp.arange(Tp).reshape(nb, BLK)
    kpos = (jnp.arange(nb)[:, None] - back) * BLK + jnp.arange((back + 1) * BLK)[None, :]
    dist = qpos[:, :, None] - kpos[:, None, :]
    mask = (dist >= 0) & (dist <= window) & (kpos[:, None, :] >= 0)
    s = jnp.einsum('bnqhgd,bnkhd->bnqhgk', qb, kw) * (D ** -0.5)
    m, l, e = masked_softmax(s, mask[None, :, :, None, None, :])
    o = jnp.einsum('bnqhgk,bnkhd->bnqhgd', e, vw.astype(jnp.float32)) / jnp.maximum(l, TINY)
    m = m[..., 0].reshape(B, Tp, Hk, G)[:, :T]
    l = l[..., 0].reshape(B, Tp, Hk, G)[:, :T]
    o = o.reshape(B, Tp, Hk, G, D)[:, :T]
    return m, l, o


def merge_by_denominator(stats):
    m = jnp.stack([st[0] for st in stats])
    l = jnp.stack([st[1] for st in stats])
    o = jnp.stack([st[2] for st in stats])
    w = l * jnp.exp(m - jnp.max(m, axis=0, keepdims=True))
    return jnp.sum(w[..., None] * o, axis=0) / jnp.sum(w, axis=0)[..., None]


def hgrn2_chunk(S0, q, k, v, lf):
    C = q.shape[1]
    b = jnp.cumsum(lf, axis=1)
    causal = jnp.tril(jnp.ones((C, C), dtype=bool))
    diff = b[:, :, None] - b[:, None, :]
    decay = jnp.exp(jnp.where(causal[None, :, :, None, None], diff, -jnp.inf))
    a = jnp.einsum('bthk,btshk,bshk->bhts', q, decay, k)
    o = jnp.einsum('bhts,bshv->bthv', a, v) + jnp.einsum('bthk,bhkv->bthv', q * jnp.exp(b), S0)
    b_end = b[:, -1]
    S = jnp.exp(b_end)[..., None] * S0 + jnp.einsum('bshk,bshv->bhkv', k * jnp.exp(b_end[:, None] - b), v)
    return S, o


def nsa_blocks(cmp_kv, slc_kv, pe):
    B, Tk = cmp_kv.shape[:2]
    nc = Tk // CMP_BLOCK
    cb = cmp_kv[:, :nc * CMP_BLOCK].astype(jnp.float32).reshape(B, nc, CMP_BLOCK, 2, NSA_KV_HEADS, HEAD_DIM)
    kc = jnp.mean(cb[:, :, :, 0] + pe.astype(jnp.float32), axis=2)
    vc = jnp.mean(cb[:, :, :, 1], axis=2)
    nsel = -(-Tk // SEL_BLOCK)
    sb = jnp.pad(slc_kv, ((0, 0), (0, nsel * SEL_BLOCK - Tk), (0, 0), (0, 0), (0, 0)))
    sb = jnp.moveaxis(sb.reshape(B, nsel, SEL_BLOCK, 2, NSA_KV_HEADS, HEAD_DIM), 4, 1)
    return kc, vc, sb[..., 0, :], sb[..., 1, :]


def nsa_core(q_nope, q_rope, qpos, kc, vc, ks_blk, vs_blk):
    B, Tq = q_nope.shape[:2]
    scale = HEAD_DIM ** -0.5
    nc = kc.shape[1]
    nsel = ks_blk.shape[2]
    s = jnp.einsum('bqhgd,bchd->bqhgc', q_nope, kc) * scale
    avail = ((jnp.arange(nc) + 1) * CMP_BLOCK - 1)[None, :] <= qpos[:, None]
    _, l, e = masked_softmax(s, avail[None, :, None, None, :])
    p_cmp = e / jnp.maximum(l, TINY)
    o_cmp = jnp.einsum('bqhgc,bchd->bqhgd', p_cmp, vc)
    imp = jnp.sum(p_cmp, axis=3)
    imp = jnp.pad(imp, ((0, 0), (0, 0), (0, 0), (0, nsel * SEL_RATIO - nc)))
    imp = imp.reshape(B, Tq, NSA_KV_HEADS, nsel, SEL_RATIO).sum(-1)
    blk = jnp.arange(nsel)
    forced = (blk[None, :] == (qpos // SEL_BLOCK)[:, None]) | (blk[None, :] == 0)
    avail_b = blk[None, :] * SEL_BLOCK <= qpos[:, None]
    imp = jnp.where(forced[None, :, None, :], SEL_FORCE, jnp.where(avail_b[None, :, None, :], imp, -1.0))
    n_top = min(SEL_TOPN, nsel)
    _, idx = lax.top_k(imp, n_top)
    idx = jnp.transpose(idx, (0, 2, 1, 3))
    gather = jax.vmap(jax.vmap(lambda kb, ib: kb[ib]))
    kg = gather(ks_blk, idx).reshape(B, NSA_KV_HEADS, Tq, n_top * SEL_BLOCK, HEAD_DIM)
    vg = gather(vs_blk, idx).reshape(B, NSA_KV_HEADS, Tq, n_top * SEL_BLOCK, HEAD_DIM)
    kpos = (idx[..., None] * SEL_BLOCK + jnp.arange(SEL_BLOCK)).reshape(B, NSA_KV_HEADS, Tq, n_top * SEL_BLOCK)
    ss = jnp.einsum('bqhgd,bhqkd->bhqgk', q_rope, kg) * scale
    _, l2, e2 = masked_softmax(ss, (kpos <= qpos[None, None, :, None])[:, :, :, None, :])
    o_slc = jnp.einsum('bhqgk,bhqkd->bqhgd', e2 / jnp.maximum(l2, TINY), vg.astype(jnp.float32))
    return o_cmp, o_slc


def even_project(x, pos, lb, w_in):
    B, T, _ = x.shape
    h = x @ w_in
    hq, hf, hi, hg, nq, ncmp, nslc, nwin, ngate = jnp.split(h, np.cumsum(EVEN_SPLITS)[:-1], axis=-1)
    f = lb + (1.0 - lb) * jax.nn.sigmoid(hf.astype(jnp.float32))
    f = f.reshape(B, T, HG_HEADS, HG_DK)
    hg_q = jax.nn.silu(hq.astype(jnp.float32)).reshape(B, T, HG_HEADS, HG_DK)
    hg_k = 1.0 - f
    hg_lf = jnp.log(f)
    hg_v = hi.astype(jnp.float32).reshape(B, T, HG_HEADS, HG_DV)
    hg_gate = hg.reshape(B, T, HG_HEADS, HG_DV)
    q = nq.reshape(B, T, NSA_HEADS, HEAD_DIM)
    q_nope = q.reshape(B, T, NSA_KV_HEADS, NSA_GROUP, HEAD_DIM)
    q_rope = rope(q, pos).reshape(B, T, NSA_KV_HEADS, NSA_GROUP, HEAD_DIM)
    cmp_kv = ncmp.reshape(B, T, 2, NSA_KV_HEADS, HEAD_DIM)
    slc = nslc.reshape(B, T, 2, NSA_KV_HEADS, HEAD_DIM)
    slc_kv = jnp.stack([rope(slc[:, :, 0], pos), slc[:, :, 1]], axis=2)
    win = nwin.reshape(B, T, 2, NSA_KV_HEADS, HEAD_DIM)
    win_kv = jnp.stack([rope(win[:, :, 0], pos), win[:, :, 1]], axis=2)
    gates = jax.nn.sigmoid(ngate.astype(jnp.float32)).reshape(B, T, 3, NSA_KV_HEADS, NSA_GROUP)[..., None]
    return hg_q, hg_k, hg_v, hg_lf, hg_gate, q_nope, q_rope, cmp_kv, slc_kv, win_kv, gates


def even_finish(dtype, o_hg, hg_gate, o_cmp, o_slc, o_win, gates, hg_norm_g, w_out):
    B, T = o_hg.shape[:2]
    o = o_hg * lax.rsqrt(jnp.mean(jnp.square(o_hg), -1, keepdims=True) + RMS_EPS)
    o = o * hg_norm_g.astype(jnp.float32).reshape(HG_HEADS, HG_DV) * jax.nn.silu(hg_gate.astype(jnp.float32))
    nsa = gates[:, :, 0] * o_cmp + gates[:, :, 1] * o_slc + gates[:, :, 2] * o_win
    cat = jnp.concatenate([o.reshape(B, T, -1), nsa.reshape(B, T, -1)], axis=-1).astype(dtype)
    return cat @ w_out


def even_prompt(x, lb, w_in, w_out, hg_norm_g, cmp_pe):
    B, T, _ = x.shape
    pos = jnp.arange(T)
    hq, hk, hv, hlf, hgate, q_nope, q_rope, cmp_kv, slc_kv, win_kv, gates = even_project(x, pos, lb, w_in)
    nch = T // HG_CHUNK
    to_ch = lambda a: jnp.moveaxis(a.reshape(B, nch, HG_CHUNK, *a.shape[2:]), 1, 0)
    S0 = jnp.zeros((B, HG_HEADS, HG_DK, HG_DV), jnp.float32)
    S_fin, o_hg = lax.scan(lambda S, c: hgrn2_chunk(S, *c), S0, (to_ch(hq), to_ch(hk), to_ch(hv), to_ch(hlf)))
    o_hg = jnp.moveaxis(o_hg, 0, 1).reshape(B, T, HG_HEADS, HG_DV)
    kc, vc, ksb, vsb = nsa_blocks(cmp_kv, slc_kv, cmp_pe)
    nq = T // BLK
    to_q = lambda a: jnp.moveaxis(a.reshape(B, nq, BLK, *a.shape[2:]), 1, 0)
    from_q = lambda a: jnp.moveaxis(a, 0, 1).reshape(B, T, *a.shape[3:])
    o_cmp, o_slc = lax.map(lambda c: nsa_core(c[0], c[1], c[2], kc, vc, ksb, vsb),
                           (to_q(q_nope), to_q(q_rope), pos.reshape(nq, BLK)))
    o_cmp, o_slc = from_q(o_cmp), from_q(o_slc)
    _, _, o_win = band_attn(q_rope, win_kv[:, :, 0], win_kv[:, :, 1], NSA_WINDOW)
    y = even_finish(x.dtype, o_hg, hgate, o_cmp, o_slc, o_win, gates, hg_norm_g, w_out)
    return y, (S_fin, cmp_kv, slc_kv, win_kv[:, -min(NSA_WINDOW, T):])


def even_sample(x, past_len, page_table, hg_state, cmp_pool, slc_pool, win_buf, lb, w_in, w_out, hg_norm_g, cmp_pe):
    B, T, _ = x.shape
    pos = past_len + jnp.arange(T)
    hq, hk, hv, hlf, hgate, q_nope, q_rope, cmp_kv, slc_kv, win_kv, gates = even_project(x, pos, lb, w_in)
    S_new, o_hg = hgrn2_chunk(hg_state.astype(jnp.float32), hq, hk, hv, hlf)
    gather_pages = lambda pool: pool[page_table].reshape(B, past_len, *pool.shape[2:])
    cmp_all = jnp.concatenate([gather_pages(cmp_pool), cmp_kv], axis=1)
    slc_all = jnp.concatenate([gather_pages(slc_pool), slc_kv], axis=1)
    kc, vc, ksb, vsb = nsa_blocks(cmp_all, slc_all, cmp_pe)
    o_cmp, o_slc = nsa_core(q_nope, q_rope, pos, kc, vc, ksb, vsb)
    Lw = win_buf.shape[1]
    rows = jnp.concatenate([win_buf, win_kv], axis=1)
    kpos = past_len - Lw + jnp.arange(Lw + T)
    dist = pos[:, None] - kpos[None, :]
    mask = (dist >= 0) & (dist <= NSA_WINDOW)
    s = jnp.einsum('bqhgd,bkhd->bqhgk', q_rope, rows[:, :, 0]) * (HEAD_DIM ** -0.5)
    _, l, e = masked_softmax(s, mask[None, :, None, None, :])
    o_win = jnp.einsum('bqhgk,bkhd->bqhgd', e, rows[:, :, 1].astype(jnp.float32)) / jnp.maximum(l, TINY)
    y = even_finish(x.dtype, o_hg, hgate, o_cmp, o_slc, o_win, gates, hg_norm_g, w_out)
    return y, (S_new, cmp_kv, slc_kv, win_kv)


def odd_project(x, pos, w_in):
    B, T, _ = x.shape
    h = (x @ w_in).reshape(B, T, 3, DIL_HEADS, HEAD_DIM)
    return rope(h[:, :, 0], pos), rope(h[:, :, 1], pos), h[:, :, 2]


def to_sub(a, d):
    B, T = a.shape[:2]
    return a.reshape(B, T // d, d, *a.shape[2:]).swapaxes(1, 2).reshape(B * d, T // d, *a.shape[2:])


def from_sub(a, d, B):
    n = a.shape[1]
    return a.reshape(B, d, n, *a.shape[2:]).swapaxes(1, 2).reshape(B, n * d, *a.shape[2:])


def odd_prompt(x, w_in, w_out):
    B, T, _ = x.shape
    pos = jnp.arange(T)
    q, k, v = odd_project(x, pos, w_in)
    stats = []
    for window, dil in DIL_CONFIGS:
        m, l, o = band_attn(to_sub(q, dil)[:, :, :, None], to_sub(k, dil), to_sub(v, dil), window // dil)
        stats.append((from_sub(m[..., 0], dil, B), from_sub(l[..., 0], dil, B), from_sub(o[:, :, :, 0], dil, B)))
    o = merge_by_denominator(stats)
    y = o.reshape(B, T, ODD_MIX).astype(x.dtype) @ w_out
    return y, jnp.stack([k, v], axis=2)[:, -min(DIL_MAX_WINDOW, T):]


def odd_sample(x, past_len, buf, w_in, w_out):
    B, T, _ = x.shape
    pos = past_len + jnp.arange(T)
    q, k, v = odd_project(x, pos, w_in)
    new_kv = jnp.stack([k, v], axis=2)
    L = buf.shape[1]
    rows = jnp.concatenate([buf, new_kv], axis=1)
    stats = []
    for window, dil in DIL_CONFIGS:
        nk = window // dil + 1
        ridx = L + jnp.arange(T)[:, None] - dil * jnp.arange(nk)[None, :]
        g = rows[:, jnp.maximum(ridx, 0)]
        s = jnp.einsum('bqhd,bqkhd->bqhk', q, g[:, :, :, 0]) * (HEAD_DIM ** -0.5)
        m, l, e = masked_softmax(s, (ridx >= 0)[None, :, None, :])
        o = jnp.einsum('bqhk,bqkhd->bqhd', e, g[:, :, :, 1].astype(jnp.float32)) / jnp.maximum(l, TINY)
        stats.append((m[..., 0], l[..., 0], o))
    o = merge_by_denominator(stats)
    y = o.reshape(B, T, ODD_MIX).astype(x.dtype) @ w_out
    return y, new_kv


def peer_ffn(x, w_q, sub_keys, u, v):
    B, T, D = x.shape
    n = B * T
    xt = x.reshape(n, D)
    q = (xt @ w_q).reshape(n, PEER_HEADS, 2, PEER_QDIM // 2)
    s = jnp.einsum('nhcd,hckd->nhck', q, sub_keys).astype(jnp.float32)
    s1, i1 = lax.top_k(s[:, :, 0], PEER_TOPK)
    s2, i2 = lax.top_k(s[:, :, 1], PEER_TOPK)
    cand = (s1[..., :, None] + s2[..., None, :]).reshape(n, PEER_HEADS, PEER_TOPK * PEER_TOPK)
    cidx = (i1[..., :, None] * PEER_NKEYS + i2[..., None, :]).reshape(n, PEER_HEADS, PEER_TOPK * PEER_TOPK)
    top, sel = lax.top_k(cand, PEER_TOPK)
    eidx = jnp.take_along_axis(cidx, sel, axis=-1)
    gate = jax.nn.softmax(top, axis=-1)
    nblk = -(-n // PEER_BLOCK)
    npad = nblk * PEER_BLOCK - n
    xb = jnp.pad(xt, ((0, npad), (0, 0))).reshape(nblk, PEER_BLOCK, D)
    eb = jnp.pad(eidx, ((0, npad), (0, 0), (0, 0))).reshape(nblk, PEER_BLOCK, PEER_HEADS, PEER_TOPK)
    gb = jnp.pad(gate, ((0, npad), (0, 0), (0, 0))).reshape(nblk, PEER_BLOCK, PEER_HEADS, PEER_TOPK)

    def expert_block(args):
        xk, ek, gk = args
        a = jax.nn.gelu(jnp.einsum('nhkd,nd->nhk', u[ek], xk).astype(jnp.float32), approximate=False)
        return jnp.einsum('nhk,nhkd->nd', (gk * a).astype(v.dtype), v[ek])

    y = lax.map(expert_block, (xb, eb, gb)).reshape(nblk * PEER_BLOCK, D)[:n]
    return y.reshape(B, T, D).astype(x.dtype)


def setup_inputs(seed: int = 0) -> dict:
    key = jax.random.key(seed)
    ks = jax.random.split(key, 24)
    f32 = jnp.float32
    n_pages = PAST_LEN // PAGE_SIZE
    used = DEC_BATCH * n_pages
    n_pool = used + max(1, used // 4)
    win_len = min(NSA_WINDOW, PAST_LEN)
    dil_len = min(DIL_MAX_WINDOW, PAST_LEN)

    def nrm(i, shape, scale):
        return jax.random.normal(ks[i], shape, f32) * scale

    page_table = jax.random.permutation(ks[7], n_pool)[:used].reshape(DEC_BATCH, n_pages).astype(jnp.int32)
    return {
        'x_prompt': nrm(0, (BATCH, SEQ, D_MODEL), 1.0),
        'x_sample': nrm(1, (DEC_BATCH, DEC_SEQ, D_MODEL), 1.0),
        'state_hgrn': nrm(2, (N_EVEN, DEC_BATCH, HG_HEADS, HG_DK, HG_DV), 0.5),
        'cache_cmp_kv': nrm(3, (N_EVEN, n_pool, PAGE_SIZE, 2, NSA_KV_HEADS, HEAD_DIM), 1.0),
        'cache_slc_kv': nrm(4, (N_EVEN, n_pool, PAGE_SIZE, 2, NSA_KV_HEADS, HEAD_DIM), 1.0),
        'cache_win_kv': nrm(5, (N_EVEN, DEC_BATCH, win_len, 2, NSA_KV_HEADS, HEAD_DIM), 1.0),
        'cache_dil_kv': nrm(6, (N_ODD, DEC_BATCH, dil_len, 2, DIL_HEADS, HEAD_DIM), 1.0),
        'page_table': page_table,
        'hg_gamma': nrm(8, (DEPTH + 1, HG_HEADS * HG_DK), 0.5),
        'even_w_in': nrm(9, (N_EVEN, D_MODEL, EVEN_IN), D_MODEL ** -0.5),
        'even_w_out': nrm(10, (N_EVEN, EVEN_MIX, D_MODEL), BETA * EVEN_MIX ** -0.5),
        'hg_norm_g': 1.0 + nrm(11, (N_EVEN, HG_HEADS * HG_DV), 0.02),
        'nsa_cmp_pe': nrm(12, (N_EVEN, CMP_BLOCK, NSA_KV_HEADS, HEAD_DIM), 0.1),
        'odd_w_in': nrm(13, (N_ODD, D_MODEL, ODD_IN), D_MODEL ** -0.5),
        'odd_w_out': nrm(14, (N_ODD, ODD_MIX, D_MODEL), BETA * ODD_MIX ** -0.5),
        'ln_mix_g': 1.0 + nrm(15, (DEPTH, D_MODEL), 0.02),
        'ln_mix_b': nrm(16, (DEPTH, D_MODEL), 0.02),
        'peer_w_q': nrm(17, (DEPTH, D_MODEL, PEER_HEADS * PEER_QDIM), D_MODEL ** -0.5),
        'peer_sub_keys': nrm(18, (DEPTH, PEER_HEADS, 2, PEER_NKEYS, PEER_QDIM // 2), (PEER_QDIM // 2) ** -0.5),
        'peer_u': nrm(19, (DEPTH, PEER_EXPERTS, D_MODEL), D_MODEL ** -0.5),
        'peer_v': nrm(20, (DEPTH, PEER_EXPERTS, D_MODEL), BETA * PEER_TOPK ** -0.5),
        'ln_ffn_g': 1.0 + nrm(21, (DEPTH, D_MODEL), 0.02),
        'ln_ffn_b': nrm(22, (DEPTH, D_MODEL), 0.02),
    }


def reference(x_prompt, x_sample, state_hgrn, cache_cmp_kv, cache_slc_kv, cache_win_kv, cache_dil_kv,
              page_table, hg_gamma, even_w_in, even_w_out, hg_norm_g, nsa_cmp_pe, odd_w_in, odd_w_out,
              ln_mix_g, ln_mix_b, peer_w_q, peer_sub_keys, peer_u, peer_v, ln_ffn_g, ln_ffn_b):
    past_len = page_table.shape[1] * PAGE_SIZE
    lb_table = jnp.cumsum(jax.nn.softmax(hg_gamma.astype(jnp.float32), axis=0), axis=0)
    yp, ys = x_prompt, x_sample
    hg_p, hg_s, cmp_p, cmp_s, slc_p, slc_s, win_p, win_s, dil_p, dil_s = ([] for _ in range(10))
    for layer in range(DEPTH):
        if layer % 2 == 0:
            e = layer // 2
            mp, (a, b, c, d) = even_prompt(yp, lb_table[layer], even_w_in[e], even_w_out[e], hg_norm_g[e], nsa_cmp_pe[e])
            ms, (a2, b2, c2, d2) = even_sample(ys, past_len, page_table, state_hgrn[e], cache_cmp_kv[e], cache_slc_kv[e],
                                               cache_win_kv[e], lb_table[layer], even_w_in[e], even_w_out[e],
                                               hg_norm_g[e], nsa_cmp_pe[e])
            hg_p.append(a); cmp_p.append(b); slc_p.append(c); win_p.append(d)
            hg_s.append(a2); cmp_s.append(b2); slc_s.append(c2); win_s.append(d2)
        else:
            o = layer // 2
            mp, kvp = odd_prompt(yp, odd_w_in[o], odd_w_out[o])
            ms, kvs = odd_sample(ys, past_len, cache_dil_kv[o], odd_w_in[o], odd_w_out[o])
            dil_p.append(kvp); dil_s.append(kvs)
        yp = layer_norm(ALPHA * yp + mp, ln_mix_g[layer], ln_mix_b[layer])
        ys = layer_norm(ALPHA * ys + ms, ln_mix_g[layer], ln_mix_b[layer])
        yp = layer_norm(ALPHA * yp + peer_ffn(yp, peer_w_q[layer], peer_sub_keys[layer], peer_u[layer], peer_v[layer]),
                        ln_ffn_g[layer], ln_ffn_b[layer])
        ys = layer_norm(ALPHA * ys + peer_ffn(ys, peer_w_q[layer], peer_sub_keys[layer], peer_u[layer], peer_v[layer]),
                        ln_ffn_g[layer], ln_ffn_b[layer])
    return (yp, ys, jnp.stack(hg_p), jnp.stack(hg_s), jnp.stack(cmp_p), jnp.stack(cmp_s),
            jnp.stack(slc_p), jnp.stack(slc_s), jnp.stack(win_p), jnp.stack(win_s),
            jnp.stack(dil_p), jnp.stack(dil_s))
```

```python
import functools
import math

import numpy as np
import jax
import jax.numpy as jnp
from jax import lax
from jax.experimental import pallas as pl
from jax.experimental.pallas import tpu as pltpu

F32 = jnp.float32
BF = jnp.bfloat16

D_MODEL = 1024
HEAD_DIM = 64
LANES = 128
ROPE_THETA = 10000.0
LN_EPS = 1e-5
RMS_EPS = 1e-6
TINY = 1e-30
DEPTH = 2
ALPHA = (2 * DEPTH) ** 0.25
PAGE_SIZE = 128

HG_HEADS = 8
NSA_HEADS = 8
NSA_KV_HEADS = 2
NSA_GROUP = NSA_HEADS // NSA_KV_HEADS
CMP_BLOCK = 32
SEL_BLOCK = 64
SEL_TOPN = 8
SEL_FORCE = 1e4
NSA_WINDOW = 512
DIL_HEADS = 16
DIL_CONFIGS = ((128, 1), (512, 4), (2048, 16))
PEER_HEADS = 8
PEER_NKEYS = 128
PEER_TOPK = 16
SAMPLE_ROWS = 8

VMEM_LIMIT = 56 * 1024 * 1024


def _cparams(sem):
    return pltpu.CompilerParams(dimension_semantics=sem, vmem_limit_bytes=VMEM_LIMIT)


def _dot(a, b):
    return jnp.dot(a, b, preferred_element_type=F32)


def _dot_nt(a, b):
    return lax.dot_general(a, b, (((1,), (1,)), ((), ())), preferred_element_type=F32)


def _dot_tn(a, b):
    return lax.dot_general(a, b, (((0,), (0,)), ((), ())), preferred_element_type=F32)


def _iota(shape, dim):
    return lax.broadcasted_iota(jnp.int32, shape, dim)


def _masked_softmax(s, mask, axis):
    s = jnp.where(mask, s, -jnp.inf)
    m = jnp.max(s, axis=axis, keepdims=True)
    m = jnp.where(m > -jnp.inf, m, 0.0)
    e = jnp.exp(s - m)
    l = jnp.sum(e, axis=axis, keepdims=True)
    return m, l, e


def _rope_chunk(x, c, s):
    lane = _iota(x.shape, 1)
    sw = jnp.where((lane & 63) < 32, pltpu.roll(x, 96, 1), pltpu.roll(x, 32, 1))
    return x * c + sw * s


def _rope_tables(pos):
    half = HEAD_DIM // 2
    inv = ROPE_THETA ** (-jnp.arange(half, dtype=F32) / half)
    ang = pos.astype(F32)[:, None] * inv[None, :]
    cos, sin = jnp.cos(ang), jnp.sin(ang)
    return jnp.tile(cos, (1, 4)), jnp.tile(jnp.concatenate([-sin, sin], 1), (1, 2))


def _layer_norm(z, g, b):
    mu = jnp.mean(z, -1, keepdims=True)
    zc = z - mu
    var = jnp.mean(zc * zc, -1, keepdims=True)
    return zc * lax.rsqrt(var + LN_EPS) * g + b


EVEN_COLS = 2048 + 512 + 768 + 1536


def _even_weight(w_in):
    hperm = np.array([[j, NSA_GROUP + j] for j in range(NSA_GROUP)]).reshape(-1)
    qcols = (2048 + hperm[:, None] * HEAD_DIM + np.arange(HEAD_DIM)[None, :]).reshape(-1)
    gate0 = 2048 + 512 + 768
    gcols = []
    for c in range(3):
        for h in hperm:
            kvh, g = divmod(int(h), NSA_GROUP)
            gcols.append(np.full(HEAD_DIM, gate0 + c * NSA_HEADS + kvh * NSA_GROUP + g))
    cols = np.concatenate([np.arange(2048), qcols, np.arange(2560, 3328), np.concatenate(gcols)])
    return jnp.take(w_in, jnp.asarray(cols, jnp.int32), axis=1).astype(BF)


def _proj_even_body(layer, x_ref, w_ref, gam_ref, cos_ref, sin_ref, hq_o, hk_o, hlf_o, hv_o, hg_o,
                    qn_o, qr_o, cmp_o, slc_o, win_o, gc_o, gs_o, gw_o):
    x = x_ref[...].astype(BF)

    def mm(a, b):
        return _dot(x, w_ref[:, a:b])

    gam = gam_ref[...]
    ge = jnp.exp(gam - jnp.max(gam, axis=0, keepdims=True))
    sm = ge / jnp.sum(ge, axis=0, keepdims=True)
    lb = jnp.sum(sm[0:layer + 1], axis=0, keepdims=True)
    c = cos_ref[...]
    s = sin_ref[...]
    hq = mm(0, 512)
    hq_o[...] = hq * jax.nn.sigmoid(hq)
    f = lb + (1.0 - lb) * jax.nn.sigmoid(mm(512, 1024))
    hk_o[...] = 1.0 - f
    hlf_o[...] = jnp.log(f)
    hv_o[...] = mm(1024, 1536)
    hg = mm(1536, 2048)
    hg_o[...] = hg * jax.nn.sigmoid(hg)
    for j in range(4):
        qj = mm(2048 + LANES * j, 2048 + LANES * (j + 1))
        qn_o[:, LANES * j:LANES * (j + 1)] = qj
        qr_o[:, LANES * j:LANES * (j + 1)] = _rope_chunk(qj, c, s)
    cmp_o[...] = mm(2560, 2816)
    slc_o[:, 0:LANES] = _rope_chunk(mm(2816, 2944), c, s)
    slc_o[:, LANES:2 * LANES] = mm(2944, 3072)
    win_o[:, 0:LANES] = _rope_chunk(mm(3072, 3200), c, s)
    win_o[:, LANES:2 * LANES] = mm(3200, 3328)
    gc_o[...] = jax.nn.sigmoid(mm(3328, 3840))
    gs_o[...] = jax.nn.sigmoid(mm(3840, 4352))
    gw_o[...] = jax.nn.sigmoid(mm(4352, 4864))


def _proj_even(x, w, gamma, cos, sin, layer, tm):
    n = x.shape[0]
    widths = [512] * 7 + [256] * 3 + [512] * 3
    row = lambda w_: pl.BlockSpec((tm, w_), lambda i: (i, 0))
    full = lambda a: pl.BlockSpec(a.shape, lambda i: (0,) * a.ndim)
    return pl.pallas_call(
        functools.partial(_proj_even_body, layer),
        grid=(n // tm,),
        in_specs=[row(D_MODEL), full(w), full(gamma), row(LANES), row(LANES)],
        out_specs=[row(w_) for w_ in widths],
        out_shape=[jax.ShapeDtypeStruct((n, w_), F32) for w_ in widths],
        compiler_params=_cparams(("parallel",)),
        name="proj_even",
    )(x, w, gamma, cos, sin)


def _hgrn_consts(C):
    L = int(math.log2(C))
    t = np.arange(C)[:, None]
    i = np.arange(C)[None, :]
    mats = [i <= t]
    bms = []
    for lv in range(L):
        half = 1 << lv
        blk = 2 * half
        mid = (t // blk) * blk + half
        upper = t >= mid
        mats.append(upper & (i >= mid) & (i <= t))
        mats.append((~upper) & (i > t) & (i <= mid - 1))
        bms.append((t // blk) == (i // blk))
    bms.append(t == i)
    sel = np.concatenate(mats, 0).astype(np.float32)
    bm = np.stack(bms).astype(np.float32)
    return jnp.asarray(sel, BF), jnp.asarray(np.concatenate([bm, bm], 1), F32)


def _hgrn_body(C, L, q_ref, k_ref, lf_ref, v_ref, gate_ref, g_ref, s0_ref, sel_ref, bm_ref,
               o_ref, so_ref, st_scr):
    ci = pl.program_id(2)

    @pl.when(ci == 0)
    def _():
        st_scr[...] = s0_ref[0, 0]

    lane = _iota((C, LANES), 1)
    row = _iota((C, LANES), 0)
    lm0 = lane < HEAD_DIM
    q = q_ref[...]
    k = k_ref[...]
    lf = lf_ref[...]
    v = v_ref[...]

    def split_heads(a):
        return jnp.concatenate([jnp.where(lm0, a, 0.0), jnp.where(lm0, 0.0, a)], 0).astype(BF)

    hi = lf.astype(BF)
    r1 = lf - hi.astype(F32)
    md = r1.astype(BF)
    lo = (r1 - md.astype(F32)).astype(BF)
    seg3 = _dot(sel_ref[...], jnp.concatenate([hi, md, lo], axis=1))
    seg = seg3[:, 0:LANES] + seg3[:, LANES:2 * LANES] + seg3[:, 2 * LANES:3 * LANES]
    b = seg[0:C]

    a = _dot_nt(split_heads(q), k.astype(BF)) * bm_ref[L]
    for lv in range(L):
        up = ((row >> lv) & 1) == 1
        eu = jnp.where(up, jnp.exp(seg[(1 + 2 * lv) * C:(2 + 2 * lv) * C]), 0.0)
        el = jnp.where(up, 0.0, jnp.exp(seg[(2 + 2 * lv) * C:(3 + 2 * lv) * C]))
        a = a + _dot_nt(split_heads(q * eu), (k * el).astype(BF)) * bm_ref[lv]
    a2 = jnp.concatenate([a[0:C], a[C:2 * C]], axis=1).astype(BF)
    st = st_scr[...]
    o = _dot(a2, split_heads(v)) + _dot_nt((q * jnp.exp(b)).astype(BF), st.astype(BF))

    bend = b[C - 1:C]
    upd = _dot_tn(v.astype(BF), (k * jnp.exp(bend - b)).astype(BF))
    same_head = (_iota((LANES, LANES), 0) < HEAD_DIM) == (_iota((LANES, LANES), 1) < HEAD_DIM)
    st_new = st * jnp.exp(bend) + jnp.where(same_head, upd, 0.0)
    st_scr[...] = st_new
    so_ref[0, 0] = st_new

    o2 = o * o
    ms0 = jnp.sum(jnp.where(lm0, o2, 0.0), axis=1, keepdims=True) * (1.0 / HEAD_DIM)
    ms1 = jnp.sum(jnp.where(lm0, 0.0, o2), axis=1, keepdims=True) * (1.0 / HEAD_DIM)
    ms = jnp.where(lm0, ms0, ms1)
    o_ref[...] = o * lax.rsqrt(ms + RMS_EPS) * g_ref[...] * gate_ref[...]


def _hgrn(q, k, lf, v, gate, g, st0, B, T, C):
    L = int(math.log2(C))
    sel, bm = _hgrn_consts(C)
    nck = T // C
    tok = pl.BlockSpec((C, LANES), lambda b, p, c: (b * nck + c, p))
    stspec = pl.BlockSpec((1, 1, LANES, LANES), lambda b, p, c: (b, p, 0, 0))
    full = lambda a: pl.BlockSpec(a.shape, lambda b, p, c: (0,) * a.ndim)
    return pl.pallas_call(
        functools.partial(_hgrn_body, C, L),
        grid=(B, HG_HEADS // 2, nck),
        in_specs=[tok, tok, tok, tok, tok, pl.BlockSpec((1, LANES), lambda b, p, c: (0, p)),
                  stspec, full(sel), full(bm)],
        out_specs=[tok, stspec],
        out_shape=[jax.ShapeDtypeStruct((B * T, 512), F32),
                   jax.ShapeDtypeStruct((B, HG_HEADS // 2, LANES, LANES), F32)],
        scratch_shapes=[pltpu.VMEM((LANES, LANES), F32)],
        compiler_params=_cparams(("parallel", "parallel", "arbitrary")),
        name="hgrn2",
    )(q, k, lf, v, gate, g, st0, sel, bm)


def _state_to_pairs(s):
    B = s.shape[0]
    st = jnp.swapaxes(s, -1, -2).reshape(B, 4, 2, HEAD_DIM, HEAD_DIM)
    z = jnp.zeros_like(st[:, :, 0])
    top = jnp.concatenate([st[:, :, 0], z], -1)
    bot = jnp.concatenate([z, st[:, :, 1]], -1)
    return jnp.concatenate([top, bot], -2)


def _pairs_to_state(sp):
    B = sp.shape[0]
    a = sp[:, :, :HEAD_DIM, :HEAD_DIM]
    b = sp[:, :, HEAD_DIM:, HEAD_DIM:]
    return jnp.swapaxes(jnp.stack([a, b], 2).reshape(B, HG_HEADS, HEAD_DIM, HEAD_DIM), -1, -2)


def _stack_heads(ref, rows):
    lm0 = _iota((rows, LANES), 1) < HEAD_DIM
    parts = []
    for j in range(NSA_GROUP):
        cj = ref[:, LANES * j:LANES * (j + 1)]
        parts += [jnp.where(lm0, cj, 0.0), jnp.where(lm0, 0.0, cj)]
    return jnp.concatenate(parts, 0)


def _block_means(cmp_ref, pe_ref, kc_scr, vc_scr, nc):
    h = nc // 2
    ck = cmp_ref[:, 0:LANES].reshape(h, 2 * CMP_BLOCK, LANES)
    pe = pe_ref[...][None]
    kc_scr[0:h] = jnp.mean(ck[:, 0:CMP_BLOCK] + pe, axis=1)
    kc_scr[h:nc] = jnp.mean(ck[:, CMP_BLOCK:2 * CMP_BLOCK] + pe, axis=1)
    cv = cmp_ref[:, LANES:2 * LANES].reshape(h, 2 * CMP_BLOCK, LANES)
    vc_scr[0:h] = jnp.mean(cv[:, 0:CMP_BLOCK], axis=1)
    vc_scr[h:nc] = jnp.mean(cv[:, CMP_BLOCK:2 * CMP_BLOCK], axis=1)


def _cmp_and_select(kc, qn_st, q0, nc, nsel_rows, nsel):
    h = nc // 2
    ncol = 2 * NSA_GROUP * LANES
    s = _dot_nt(kc.astype(BF), qn_st) * (HEAD_DIM ** -0.5)
    r = _iota((nc, ncol), 0)
    cidx = jnp.where(r < h, 2 * r, 2 * (r - h) + 1)
    qpos = q0 + (_iota((nc, ncol), 1) & (LANES - 1))
    avail = (cidx + 1) * CMP_BLOCK - 1 <= qpos
    _, l, e = _masked_softmax(s, avail, 0)
    p = e / jnp.maximum(l, TINY)
    pp = p[0:h] + p[h:nc]
    w2 = 2 * LANES
    imp = pp[:, 0:w2] + pp[:, w2:2 * w2] + pp[:, 2 * w2:3 * w2] + pp[:, 3 * w2:4 * w2]
    if nsel_rows > h:
        imp = jnp.concatenate([imp, jnp.zeros((nsel_rows - h, w2), F32)], 0)
    blk = _iota((nsel_rows, w2), 0)
    qp = q0 + (_iota((nsel_rows, w2), 1) & (LANES - 1))
    forced = (blk == qp // SEL_BLOCK) | (blk == 0)
    imp = jnp.where(forced, SEL_FORCE, jnp.where(blk * SEL_BLOCK <= qp, imp, -1.0))
    imp = jnp.where(blk < nsel, imp, -2.0)
    rank = jnp.zeros((nsel_rows, w2), F32)
    for i in range(nsel):
        ri = imp[i:i + 1, :]
        beats = (ri > imp) | ((ri == imp) & (blk > i))
        rank = rank + jnp.where(beats, 1.0, 0.0)
    sel = jnp.where(rank < float(min(SEL_TOPN, nsel)), 1.0, 0.0)
    return p, sel


def _attend_rows(q_st, k, v, mask):
    s = _dot_nt(q_st, k) * (HEAD_DIM ** -0.5)
    _, l, e = _masked_softmax(s, mask, 1)
    return _dot(e.astype(BF), v) / jnp.maximum(l, TINY)


def _attend(q_st, k, v, mask, rows):
    o = _attend_rows(q_st, k, v, mask)
    lm0 = _iota((rows, LANES), 1) < HEAD_DIM
    return jnp.where(lm0, o[0:rows], o[rows:2 * rows])


def _nsa_prompt_body(T, qn_ref, qr_ref, cmp_ref, slc_ref, win_ref, gc_ref, gs_ref, gw_ref, pe_ref,
                     e_ref, o_ref, kc_scr, vc_scr):
    qb = pl.program_id(1)
    nc = T // CMP_BLOCK
    nsel = T // SEL_BLOCK
    R = LANES

    @pl.when(qb == 0)
    def _():
        _block_means(cmp_ref, pe_ref, kc_scr, vc_scr, nc)

    q0 = qb * R
    qn_st = _stack_heads(qn_ref, R).astype(BF)
    qr_st = _stack_heads(qr_ref, R).astype(BF)
    p, sel = _cmp_and_select(kc_scr[...], qn_st, q0, nc, nsel, nsel)
    oc = _dot(p.T.astype(BF), vc_scr[...].astype(BF))

    sel_e = _dot_tn(sel.astype(BF), e_ref[...])
    qpos = q0 + (_iota((2 * R, T), 0) & (R - 1))
    mask_s = (sel_e > 0.5) & (_iota((2 * R, T), 1) <= qpos)
    ks = slc_ref[:, 0:LANES].astype(BF)
    vs = slc_ref[:, LANES:2 * LANES].astype(BF)

    nw = NSA_WINDOW + R
    start = pl.multiple_of(jnp.maximum(qb - NSA_WINDOW // R, 0) * R, R)
    kw = win_ref[pl.ds(start, nw), 0:LANES].astype(BF)
    vw = win_ref[pl.ds(start, nw), LANES:2 * LANES].astype(BF)
    dist = q0 + (_iota((2 * R, nw), 0) & (R - 1)) - (start + _iota((2 * R, nw), 1))
    mask_w = (dist >= 0) & (dist <= NSA_WINDOW)

    lm0 = _iota((R, LANES), 1) < HEAD_DIM
    for j in range(NSA_GROUP):
        cols = slice(LANES * j, LANES * (j + 1))
        qj = qr_st[2 * R * j:2 * R * (j + 1)]
        o_s = _attend(qj, ks, vs, mask_s, R)
        o_w = _attend(qj, kw, vw, mask_w, R)
        o_c = jnp.where(lm0, oc[2 * R * j:2 * R * j + R], oc[2 * R * j + R:2 * R * (j + 1)])
        o_ref[:, cols] = gc_ref[:, cols] * o_c + gs_ref[:, cols] * o_s + gw_ref[:, cols] * o_w


def _sel_expand(nrows, nkeys):
    e = (np.arange(nkeys)[None, :] // SEL_BLOCK) == np.arange(nrows)[:, None]
    return jnp.asarray(e.astype(np.float32), BF)


def _nsa_prompt(qn, qr, cmp, slc, win, gc, gs, gw, pe, B, T):
    R = LANES
    nq = T // R
    tok = pl.BlockSpec((R, 512), lambda b, i: (b * nq + i, 0))
    seq = pl.BlockSpec((T, 256), lambda b, i: (b, 0))
    e = _sel_expand(T // SEL_BLOCK, T)
    full = lambda a: pl.BlockSpec(a.shape, lambda b, i: (0,) * a.ndim)
    return pl.pallas_call(
        functools.partial(_nsa_prompt_body, T),
        grid=(B, nq),
        in_specs=[tok, tok, seq, seq, seq, tok, tok, tok, full(pe), full(e)],
        out_specs=tok,
        out_shape=jax.ShapeDtypeStruct((B * T, 512), F32),
        scratch_shapes=[pltpu.VMEM((T // CMP_BLOCK, LANES), F32), pltpu.VMEM((T // CMP_BLOCK, LANES), F32)],
        compiler_params=_cparams(("parallel", "arbitrary")),
        name="nsa_prompt",
    )(qn, qr, cmp, slc, win, gc, gs, gw, pe, e)


def _nsa_sample_body(P, npages, *refs):
    (qn_ref, qr_ref, slcn_ref, winn_ref, winb_ref, gc_ref, gs_ref, gw_ref, pe_ref, e_ref) = refs[1:11]
    cmp_pages = refs[11:11 + npages]
    slc_pages = refs[11 + npages:11 + 2 * npages]
    o_ref = refs[11 + 2 * npages]
    cmp_all, slc_all, win_all, kc_scr, vc_scr = refs[12 + 2 * npages:]
    S = SAMPLE_ROWS
    nc = P // CMP_BLOCK
    nsel = -(-(P + 4) // SEL_BLOCK)
    nsel_rows = e_ref.shape[0]
    nk = slc_all.shape[0]
    lw = winb_ref.shape[1]
    nkw = win_all.shape[0]
    for pg in range(npages):
        cmp_all[PAGE_SIZE * pg:PAGE_SIZE * (pg + 1)] = cmp_pages[pg][0]
        slc_all[PAGE_SIZE * pg:PAGE_SIZE * (pg + 1)] = slc_pages[pg][0]
    slc_all[P:P + S] = slcn_ref[...]
    slc_all[P + S:nk] = jnp.zeros((nk - P - S, 256), F32)
    win_all[0:lw] = winb_ref[0]
    win_all[lw:lw + S] = winn_ref[...]
    win_all[lw + S:nkw] = jnp.zeros((nkw - lw - S, 256), F32)
    _block_means(cmp_all, pe_ref, kc_scr, vc_scr, nc)

    lm0s = _iota((S, LANES), 1) < HEAD_DIM
    zpad = jnp.zeros((LANES - S, LANES), F32)
    qn_parts, qr_parts = [], []
    for j in range(NSA_GROUP):
        cn = qn_ref[:, LANES * j:LANES * (j + 1)]
        cr = qr_ref[:, LANES * j:LANES * (j + 1)]
        qn_parts += [jnp.where(lm0s, cn, 0.0), zpad, jnp.where(lm0s, 0.0, cn), zpad]
        qr_parts += [jnp.where(lm0s, cr, 0.0), jnp.where(lm0s, 0.0, cr)]
    qn_st = jnp.concatenate(qn_parts, 0).astype(BF)
    qr_st = jnp.concatenate(qr_parts, 0).astype(BF)
    p, sel = _cmp_and_select(kc_scr[...], qn_st, P, nc, nsel_rows, nsel)
    pt = p.T
    pc = jnp.concatenate([pt[LANES * i:LANES * i + S] for i in range(2 * NSA_GROUP)], 0)
    o_c = _dot(pc.astype(BF), vc_scr[...].astype(BF))

    R = 2 * NSA_GROUP * S
    sel_e = _dot_tn(sel.astype(BF), e_ref[...])
    sel_c = jnp.concatenate([sel_e[0:S], sel_e[LANES:LANES + S]] * NSA_GROUP, 0)
    qpos = P + (_iota((R, nk), 0) & (S - 1))
    mask_s = (sel_c > 0.5) & (_iota((R, nk), 1) <= qpos)
    o_s = _attend_rows(qr_st, slc_all[:, 0:LANES].astype(BF), slc_all[:, LANES:2 * LANES].astype(BF), mask_s)
    dist = (_iota((R, nkw), 0) & (S - 1)) + lw - _iota((R, nkw), 1)
    mask_w = (dist >= 0) & (dist <= NSA_WINDOW)
    o_w = _attend_rows(qr_st, win_all[:, 0:LANES].astype(BF), win_all[:, LANES:2 * LANES].astype(BF), mask_w)
    for j in range(NSA_GROUP):
        cols = slice(LANES * j, LANES * (j + 1))
        r0 = slice(2 * S * j, 2 * S * j + S)
        r1 = slice(2 * S * j + S, 2 * S * (j + 1))
        pick = lambda a: jnp.where(lm0s, a[r0], a[r1])
        o_ref[:, cols] = gc_ref[:, cols] * pick(o_c) + gs_ref[:, cols] * pick(o_s) + gw_ref[:, cols] * pick(o_w)


def _nsa_sample(page_table, qn, qr, slc_new, win_new, win_buf, gc, gs, gw, pe, cmp_pool, slc_pool, B, P):
    S = SAMPLE_ROWS
    npages = P // PAGE_SIZE
    nk = P + LANES
    nsel_rows = 8 * (-(-(-(-(P + 4) // SEL_BLOCK)) // 8))
    e = _sel_expand(nsel_rows, nk)
    tok = lambda w_: pl.BlockSpec((S, w_), lambda b, pt: (b, 0))
    full = lambda a: pl.BlockSpec(a.shape, lambda b, pt: (0,) * a.ndim)
    page = lambda pg: pl.BlockSpec((1, PAGE_SIZE, 256), lambda b, pt: (pt[b * npages + pg], 0, 0))
    in_specs = ([tok(512), tok(512), tok(256), tok(256),
                 pl.BlockSpec((1, win_buf.shape[1], 256), lambda b, pt: (b, 0, 0)),
                 tok(512), tok(512), tok(512), full(pe), full(e)]
                + [page(pg) for pg in range(npages)] * 2)
    gs_ = pltpu.PrefetchScalarGridSpec(
        num_scalar_prefetch=1, grid=(B,), in_specs=in_specs, out_specs=tok(512),
        scratch_shapes=[pltpu.VMEM((P, 256), F32), pltpu.VMEM((nk, 256), F32),
                        pltpu.VMEM((win_buf.shape[1] + LANES, 256), F32),
                        pltpu.VMEM((P // CMP_BLOCK, LANES), F32), pltpu.VMEM((P // CMP_BLOCK, LANES), F32)])
    return pl.pallas_call(
        functools.partial(_nsa_sample_body, P, npages),
        grid_spec=gs_,
        out_shape=jax.ShapeDtypeStruct((B * S, 512), F32),
        compiler_params=_cparams(("arbitrary",)),
        name="nsa_sample",
    )(page_table.reshape(-1), qn, qr, slc_new, win_new, win_buf, gc, gs, gw, pe, e,
      *([cmp_pool] * npages), *([slc_pool] * npages))


def _outproj_ln_body(a_ref, b_ref, x_ref, wa_ref, wb_ref, g_ref, bb_ref, y_ref):
    mix = _dot(a_ref[...].astype(BF), wa_ref[...]) + _dot(b_ref[...].astype(BF), wb_ref[...])
    y_ref[...] = _layer_norm(ALPHA * x_ref[...] + mix, g_ref[...], bb_ref[...])


def _outproj_ln(a, acol, b, bcol, x, wa, wb, g, bb, tm):
    n = x.shape[0]
    full = lambda t: pl.BlockSpec(t.shape, lambda i: (0,) * t.ndim)
    return pl.pallas_call(
        _outproj_ln_body,
        grid=(n // tm,),
        in_specs=[pl.BlockSpec((tm, 512), lambda i: (i, acol)), pl.BlockSpec((tm, 512), lambda i: (i, bcol)),
                  pl.BlockSpec((tm, D_MODEL), lambda i: (i, 0)), full(wa), full(wb), full(g), full(bb)],
        out_specs=pl.BlockSpec((tm, D_MODEL), lambda i: (i, 0)),
        out_shape=jax.ShapeDtypeStruct((n, D_MODEL), F32),
        compiler_params=_cparams(("parallel",)),
        name="outproj_ln",
    )(a, b, x, wa, wb, g, bb)


def _proj_odd_body(x_ref, w_ref, cos_ref, sin_ref, q_o, kv_o):
    x = x_ref[...].astype(BF)
    c = cos_ref[...]
    s = sin_ref[...]
    nchunk = DIL_HEADS * HEAD_DIM // LANES
    for j in range(nchunk):
        cols = slice(LANES * j, LANES * (j + 1))
        q_o[:, cols] = _rope_chunk(_dot(x, w_ref[:, cols]), c, s)
        kv_o[:, cols] = _rope_chunk(_dot(x, w_ref[:, D_MODEL + LANES * j:D_MODEL + LANES * (j + 1)]), c, s)
    kv_o[:, D_MODEL:2 * D_MODEL] = _dot(x, w_ref[:, 2 * D_MODEL:3 * D_MODEL])


def _proj_odd(x, w, cos, sin, tm):
    n = x.shape[0]
    row = lambda w_: pl.BlockSpec((tm, w_), lambda i: (i, 0))
    return pl.pallas_call(
        _proj_odd_body,
        grid=(n // tm,),
        in_specs=[row(D_MODEL), pl.BlockSpec(w.shape, lambda i: (0, 0)), row(LANES), row(LANES)],
        out_specs=[row(D_MODEL), row(2 * D_MODEL)],
        out_shape=[jax.ShapeDtypeStruct((n, D_MODEL), F32), jax.ShapeDtypeStruct((n, 2 * D_MODEL), F32)],
        compiler_params=_cparams(("parallel",)),
        name="proj_odd",
    )(x, w, cos, sin)


def _dil_prompt_body(T, q_ref, k_ref, v_ref, o_ref, acc_scr, m_scr, l_scr):
    R = LANES
    lm0 = _iota((R, LANES), 1) < HEAD_DIM
    for ci, (window, d) in enumerate(DIL_CONFIGS):
        band = window // d
        nblk = T // d // R
        for r in range(d):
            for i in range(nblk):
                q0 = r + d * R * i
                rows_q = pl.ds(q0, R, stride=d) if d > 1 else pl.ds(q0, R)
                if i > 0:
                    k0, nk = q0 - d * R, 2 * R
                else:
                    k0, nk = q0, R
                rows_k = pl.ds(k0, nk, stride=d) if d > 1 else pl.ds(k0, nk)
                qs = q_ref[rows_q, :]
                q_st = jnp.concatenate([jnp.where(lm0, qs, 0.0), jnp.where(lm0, 0.0, qs)], 0).astype(BF)
                ks = k_ref[rows_k, :].astype(BF)
                vs = v_ref[rows_k, :].astype(BF)
                dist = (_iota((2 * R, nk), 0) & (R - 1)) + (nk - R) - _iota((2 * R, nk), 1)
                s = _dot_nt(q_st, ks) * (HEAD_DIM ** -0.5)
                m, l, e = _masked_softmax(s, (dist >= 0) & (dist <= band), 1)
                acc = _dot(e.astype(BF), vs)
                acc_scr[ci, rows_q, :] = jnp.where(lm0, acc[0:R], acc[R:2 * R])
                m_scr[ci, rows_q, :] = jnp.where(lm0, m[0:R], m[R:2 * R])
                l_scr[ci, rows_q, :] = jnp.where(lm0, l[0:R], l[R:2 * R])
    ncfg = len(DIL_CONFIGS)
    mx = m_scr[0]
    for ci in range(1, ncfg):
        mx = jnp.maximum(mx, m_scr[ci])
    num = jnp.zeros((T, LANES), F32)
    den = jnp.zeros((T, LANES), F32)
    for ci in range(ncfg):
        w = jnp.exp(m_scr[ci] - mx)
        num = num + w * acc_scr[ci]
        den = den + w * l_scr[ci]
    o_ref[...] = num / den


def _dil_prompt(q, kv, B, T):
    npair = DIL_HEADS // 2
    ncfg = len(DIL_CONFIGS)
    return pl.pallas_call(
        functools.partial(_dil_prompt_body, T),
        grid=(B, npair),
        in_specs=[pl.BlockSpec((T, LANES), lambda b, p: (b, p)),
                  pl.BlockSpec((T, LANES), lambda b, p: (b, p)),
                  pl.BlockSpec((T, LANES), lambda b, p: (b, npair + p))],
        out_specs=pl.BlockSpec((T, LANES), lambda b, p: (b, p)),
        out_shape=jax.ShapeDtypeStruct((B * T, D_MODEL), F32),
        scratch_shapes=[pltpu.VMEM((ncfg, T, LANES), F32)] * 3,
        compiler_params=_cparams(("parallel", "parallel")),
        name="dil_prompt",
    )(q, kv, kv)


def _dil_sample_body(Lb, q_ref, kvn_ref, kb_ref, vb_ref, o_ref):
    S = SAMPLE_ROWS
    H = DIL_HEADS
    R = S * H
    hm = (_iota((H, D_MODEL), 1) // HEAD_DIM) == _iota((H, D_MODEL), 0)
    q = q_ref[...]
    qbd = jnp.concatenate([jnp.where(hm, q[t:t + 1, :], 0.0) for t in range(S)], 0).astype(BF)
    s_b = _dot_nt(kb_ref[0].astype(BF), qbd) * (HEAD_DIM ** -0.5)
    s_n = _dot_nt(kvn_ref[:, 0:D_MODEL].astype(BF), qbd) * (HEAD_DIM ** -0.5)
    dist_b = Lb + _iota((Lb, R), 1) // H - _iota((Lb, R), 0)
    dist_n = _iota((S, R), 1) // H - _iota((S, R), 0)
    stats = []
    for window, d in DIL_CONFIGS:
        ok_b = (dist_b <= window) & ((dist_b & (d - 1)) == 0)
        ok_n = (dist_n >= 0) & ((dist_n & (d - 1)) == 0)
        sb = jnp.where(ok_b, s_b, -jnp.inf)
        sn = jnp.where(ok_n, s_n, -jnp.inf)
        m = jnp.maximum(jnp.max(sb, axis=0, keepdims=True), jnp.max(sn, axis=0, keepdims=True))
        m = jnp.where(m > -jnp.inf, m, 0.0)
        eb = jnp.exp(sb - m)
        en = jnp.exp(sn - m)
        l = jnp.sum(eb, axis=0, keepdims=True) + jnp.sum(en, axis=0, keepdims=True)
        stats.append((m, l, eb, en))
    mx = functools.reduce(jnp.maximum, [st[0] for st in stats])
    pb = jnp.zeros((Lb, R), F32)
    pn = jnp.zeros((S, R), F32)
    den = jnp.zeros((1, R), F32)
    for m, l, eb, en in stats:
        w = jnp.exp(m - mx)
        pb = pb + w * eb
        pn = pn + w * en
        den = den + w * l
    inv = 1.0 / den
    o = _dot_tn((pb * inv).astype(BF), vb_ref[0].astype(BF)) \
        + _dot_tn((pn * inv).astype(BF), kvn_ref[:, D_MODEL:2 * D_MODEL].astype(BF))
    keep = (_iota((R, D_MODEL), 1) // HEAD_DIM) == (_iota((R, D_MODEL), 0) & (H - 1))
    o_ref[...] = jnp.sum(jnp.where(keep, o, 0.0).reshape(S, H, D_MODEL), axis=1)


def _dil_sample(q, kv_new, buf, B):
    S = SAMPLE_ROWS
    Lb = buf.shape[1]
    return pl.pallas_call(
        functools.partial(_dil_sample_body, Lb),
        grid=(B,),
        in_specs=[pl.BlockSpec((S, D_MODEL), lambda b: (b, 0)),
                  pl.BlockSpec((S, 2 * D_MODEL), lambda b: (b, 0)),
                  pl.BlockSpec((1, Lb, D_MODEL), lambda b: (b, 0, 0)),
                  pl.BlockSpec((1, Lb, D_MODEL), lambda b: (b, 0, 1))],
        out_specs=pl.BlockSpec((S, D_MODEL), lambda b: (b, 0)),
        out_shape=jax.ShapeDtypeStruct((B * S, D_MODEL), F32),
        compiler_params=_cparams(("parallel",)),
        name="dil_sample",
    )(q, kv_new, buf, buf)


def _top16(s, nrow, tn):
    iota_k = _iota((nrow, tn), 0).astype(F32)
    iota_r = _iota((PEER_TOPK, tn), 0)

    def step(it, carry):
        work, rank, vals = carry
        m = jnp.max(work, axis=0, keepdims=True)
        idx = jnp.min(jnp.where(work == m, iota_k, float(nrow)), axis=0, keepdims=True)
        oh = iota_k == idx
        itf = it.astype(F32)
        return (jnp.where(oh, -jnp.inf, work), jnp.where(oh, itf, rank), jnp.where(iota_r == it, m, vals))

    init = (s, jnp.full((nrow, tn), float(PEER_TOPK), F32), jnp.zeros((PEER_TOPK, tn), F32))
    _, rank, vals = lax.fori_loop(0, PEER_TOPK, step, init)
    return vals, rank


def _peer_topk_body(y_ref, wq_ref, keys_ref, r2_o, e2_o, cnt_o, w1_o):
    tn = y_ref.shape[0]
    K = PEER_TOPK
    yb = y_ref[...].astype(BF)
    svals, ranks = [], []
    for c in range(2):
        qc = _dot(yb, wq_ref[:, LANES * c:LANES * (c + 1)])
        st = _dot_nt(keys_ref[0, c], qc.astype(BF))
        vals, rank = _top16(st, PEER_NKEYS, tn)
        svals.append(vals)
        ranks.append(rank)
    s1, s2 = svals
    cand = jnp.concatenate([s1[r:r + 1] + s2 for r in range(K)], 0)
    _, crank = _top16(cand, K * K, tn)
    sel = crank < float(K)
    e1 = jnp.exp(s1 - s1[0:1])
    e2 = jnp.exp(s2 - s2[0:1])
    prod = jnp.concatenate([e1[r:r + 1] * e2 for r in range(K)], 0)
    z = jnp.sum(jnp.where(sel, prod, 0.0), axis=0, keepdims=True)
    cnt = jnp.sum(jnp.where(sel, 1.0, 0.0).reshape(K, K, tn), axis=1)
    w1 = e1 / z
    cntk = jnp.zeros((PEER_NKEYS, tn), F32)
    w1k = jnp.zeros((PEER_NKEYS, tn), F32)
    e2k = jnp.zeros((PEER_NKEYS, tn), F32)
    for r in range(K):
        hit1 = ranks[0] == float(r)
        cntk = jnp.where(hit1, cnt[r:r + 1], cntk)
        w1k = jnp.where(hit1, w1[r:r + 1], w1k)
        e2k = jnp.where(ranks[1] == float(r), e2[r:r + 1], e2k)
    r2_o[0] = ranks[1]
    e2_o[0] = e2k
    cnt_o[0] = cntk
    w1_o[0] = w1k


def _peer_topk(y, wq, keys, tn):
    n = y.shape[0]
    out = pl.BlockSpec((1, PEER_NKEYS, tn), lambda i, h: (h, 0, i))
    return pl.pallas_call(
        _peer_topk_body,
        grid=(n // tn, PEER_HEADS),
        in_specs=[pl.BlockSpec((tn, D_MODEL), lambda i, h: (i, 0)),
                  pl.BlockSpec((D_MODEL, 2 * LANES), lambda i, h: (0, h)),
                  pl.BlockSpec((1, 2, PEER_NKEYS, LANES), lambda i, h: (h, 0, 0, 0))],
        out_specs=[out] * 4,
        out_shape=[jax.ShapeDtypeStruct((PEER_HEADS, PEER_NKEYS, n), F32)] * 4,
        compiler_params=_cparams(("parallel", "arbitrary")),
        name="peer_topk",
    )(y, wq, keys)


def _gelu(x):
    return 0.5 * x * (1.0 + lax.erf(x * np.float32(math.sqrt(0.5))))


def _peer_main_body(npe, xt_ref, u_ref, vt_ref, r2_ref, e2_ref, cnt_ref, w1_ref, yt_ref):
    e = pl.program_id(1)
    tn = xt_ref.shape[1]

    @pl.when(e == 0)
    def _():
        yt_ref[...] = jnp.zeros(yt_ref.shape, F32)

    a = _gelu(_dot(u_ref[...], xt_ref[...]))
    parts = []
    for cc in range(npe):
        c = e * npe + cc
        g = jnp.zeros((PEER_NKEYS, tn), F32)
        for h in range(PEER_HEADS):
            cnt_row = cnt_ref[h, pl.ds(c, 1), :]
            w_row = w1_ref[h, pl.ds(c, 1), :]
            g = g + jnp.where(r2_ref[h] < cnt_row, e2_ref[h] * w_row, 0.0)
        parts.append((g * a[PEER_NKEYS * cc:PEER_NKEYS * (cc + 1)]).astype(BF))
    ga = parts[0] if npe == 1 else jnp.concatenate(parts, 0)
    yt_ref[...] += _dot(vt_ref[...], ga)


def _peer_main(xt, u, vt, r2, e2, cnt, w1, tn, te):
    n = xt.shape[1]
    ne = u.shape[0]
    npe = te // PEER_NKEYS
    tab = pl.BlockSpec((PEER_HEADS, PEER_NKEYS, tn), lambda i, e: (0, 0, i))
    return pl.pallas_call(
        functools.partial(_peer_main_body, npe),
        grid=(n // tn, ne // te),
        in_specs=[pl.BlockSpec((D_MODEL, tn), lambda i, e: (0, i)),
                  pl.BlockSpec((te, D_MODEL), lambda i, e: (e, 0)),
                  pl.BlockSpec((D_MODEL, te), lambda i, e: (0, e)),
                  tab, tab, tab, tab],
        out_specs=pl.BlockSpec((D_MODEL, tn), lambda i, e: (0, i)),
        out_shape=jax.ShapeDtypeStruct((D_MODEL, n), F32),
        compiler_params=_cparams(("parallel", "arbitrary")),
        name="peer_main",
    )(xt, u, vt, r2, e2, cnt, w1)


def _ln_t_body(x_ref, ft_ref, g_ref, b_ref, y_ref):
    y_ref[...] = _layer_norm(ALPHA * x_ref[...] + ft_ref[...].T, g_ref[...], b_ref[...])


def _ln_t(x, ft, g, b, tn):
    n = x.shape[0]
    full = lambda t: pl.BlockSpec(t.shape, lambda i: (0,) * t.ndim)
    return pl.pallas_call(
        _ln_t_body,
        grid=(n // tn,),
        in_specs=[pl.BlockSpec((tn, D_MODEL), lambda i: (i, 0)), pl.BlockSpec((D_MODEL, tn), lambda i: (0, i)),
                  full(g), full(b)],
        out_specs=pl.BlockSpec((tn, D_MODEL), lambda i: (i, 0)),
        out_shape=jax.ShapeDtypeStruct((n, D_MODEL), F32),
        compiler_params=_cparams(("parallel",)),
        name="ln_residual",
    )(x, ft, g, b)


def _peer_layer(y, wq, keys, u, v, g, b):
    r2, e2, cnt, w1 = _peer_topk(y, wq.astype(BF), keys.astype(BF), 256)
    ft = _peer_main(y.T.astype(BF), u.astype(BF), v.T.astype(BF), r2, e2, cnt, w1, 512, 512)
    return _ln_t(y, ft, g, b, 256)


def _pad_rows(a, S):
    return jnp.pad(a, ((0, 0), (0, S - a.shape[1])) + ((0, 0),) * (a.ndim - 2))


def kernel(x_prompt, x_sample, state_hgrn, cache_cmp_kv, cache_slc_kv, cache_win_kv, cache_dil_kv, page_table,
           hg_gamma, even_w_in, even_w_out, hg_norm_g, nsa_cmp_pe, odd_w_in, odd_w_out, ln_mix_g, ln_mix_b,
           peer_w_q, peer_sub_keys, peer_u, peer_v, ln_ffn_g, ln_ffn_b):
    B, T, D = x_prompt.shape
    Bs, Ts, _ = x_sample.shape
    S = SAMPLE_ROWS
    P = page_table.shape[1] * PAGE_SIZE
    npr = B * T
    yp = x_prompt.reshape(npr, D)
    ys = _pad_rows(x_sample, S).reshape(Bs * S, D)
    cos_p, sin_p = _rope_tables(jnp.tile(jnp.arange(T), B))
    cos_s, sin_s = _rope_tables(jnp.tile(P + jnp.arange(S), Bs))
    live = (jnp.arange(Bs * S) % S < Ts)[:, None]
    hperm = np.array([[j, NSA_GROUP + j] for j in range(NSA_GROUP)]).reshape(-1)
    outs = {}

    for layer in range(DEPTH):
        row2 = lambda a: a[layer].reshape(1, D)
        if layer % 2 == 0:
            e = layer // 2
            w = _even_weight(even_w_in[e])
            wo = even_w_out[e]
            wa = wo[:512].astype(BF)
            wb = wo[512:].reshape(NSA_HEADS, HEAD_DIM, D)[hperm].reshape(512, D).astype(BF)
            g = hg_norm_g[e].reshape(1, 512)
            pe = nsa_cmp_pe[e].reshape(CMP_BLOCK, LANES)
            hq, hk, hlf, hv, hg, qn, qr, cmp, slc, win, gc, gs, gw = _proj_even(yp, w, hg_gamma, cos_p, sin_p, layer, 256)
            st0 = jnp.zeros((B, HG_HEADS // 2, LANES, LANES), F32)
            o_hg, st = _hgrn(hq, hk, hlf, hv, hg, g, st0, B, T, LANES)
            nsa = _nsa_prompt(qn, qr, cmp, slc, win, gc, gs, gw, pe, B, T)
            mp = (o_hg, nsa)
            kv5 = lambda a, b_, t_: a.reshape(b_, t_, 2, NSA_KV_HEADS, HEAD_DIM)
            outs.setdefault("hg_p", []).append(_pairs_to_state(st))
            outs.setdefault("cmp_p", []).append(kv5(cmp, B, T))
            outs.setdefault("slc_p", []).append(kv5(slc, B, T))
            outs.setdefault("win_p", []).append(kv5(win, B, T)[:, -min(NSA_WINDOW, T):])
            hq, hk, hlf, hv, hg, qn, qr, cmp, slc, win, gc, gs, gw = _proj_even(ys, w, hg_gamma, cos_s, sin_s, layer, 256)
            hk = jnp.where(live, hk, 0.0)
            hlf = jnp.where(live, hlf, 0.0)
            C = 2 * S
            pad = lambda a: _pad_rows(a.reshape(Bs, S, 512), C).reshape(Bs * C, 512)
            o_hg, st = _hgrn(pad(hq), pad(hk), pad(hlf), pad(hv), pad(hg), g,
                             _state_to_pairs(state_hgrn[e].astype(F32)), Bs, C, C)
            o_hg = o_hg.reshape(Bs, C, 512)[:, :S].reshape(Bs * S, 512)
            npool = cache_cmp_kv.shape[1]
            nsa = _nsa_sample(page_table, qn, qr, slc, win, cache_win_kv[e].reshape(Bs, -1, 256), gc, gs, gw, pe,
                              cache_cmp_kv[e].reshape(npool, PAGE_SIZE, 256),
                              cache_slc_kv[e].reshape(npool, PAGE_SIZE, 256), Bs, P)
            ms = (o_hg, nsa)
            outs.setdefault("hg_s", []).append(_pairs_to_state(st))
            outs.setdefault("cmp_s", []).append(kv5(cmp, Bs, S)[:, :Ts])
            outs.setdefault("slc_s", []).append(kv5(slc, Bs, S)[:, :Ts])
            outs.setdefault("win_s", []).append(kv5(win, Bs, S)[:, :Ts])
            acol, bcol = 0, 0
        else:
            o = layer // 2
            w = odd_w_in[o].astype(BF)
            wo = odd_w_out[o]
            wa = wo[:512].astype(BF)
            wb = wo[512:].astype(BF)
            q, kv = _proj_odd(yp, w, cos_p, sin_p, 256)
            att = _dil_prompt(q, kv, B, T)
            mp = (att, att)
            kv6 = lambda a, b_, t_: a.reshape(b_, t_, 2, DIL_HEADS, HEAD_DIM)
            outs.setdefault("dil_p", []).append(kv6(kv, B, T)[:, -min(DIL_CONFIGS[-1][0], T):])
            q, kv = _proj_odd(ys, w, cos_s, sin_s, 256)
            buf = cache_dil_kv[o]
            att = _dil_sample(q, kv, buf.reshape(Bs, buf.shape[1], 2 * D), Bs)
            ms = (att, att)
            outs.setdefault("dil_s", []).append(kv6(kv, Bs, S)[:, :Ts])
            acol, bcol = 0, 1
        lg, lbias = row2(ln_mix_g), row2(ln_mix_b)
        yp = _outproj_ln(mp[0], acol, mp[1], bcol, yp, wa, wb, lg, lbias, 256)
        ys = _outproj_ln(ms[0], acol, ms[1], bcol, ys, wa, wb, lg, lbias, 256)
        y = jnp.concatenate([yp, ys], 0)
        y = _peer_layer(y, peer_w_q[layer], peer_sub_keys[layer], peer_u[layer], peer_v[layer],
                        row2(ln_ffn_g), row2(ln_ffn_b))
        yp, ys = y[:npr], y[npr:]

    stack = lambda k_: jnp.stack(outs[k_])
    return (yp.reshape(B, T, D), ys.reshape(Bs, S, D)[:, :Ts], stack("hg_p"), stack("hg_s"),
            stack("cmp_p"), stack("cmp_s"), stack("slc_p"), stack("slc_s"), stack("win_p"), stack("win_s"),
            stack("dil_p"), stack("dil_s"))
```

```python
import functools
import math

import numpy as np
import jax
import jax.numpy as jnp
from jax import lax
from jax.experimental import pallas as pl
from jax.experimental.pallas import tpu as pltpu

F32 = jnp.float32
BF = jnp.bfloat16

D_MODEL = 1024
HEAD_DIM = 64
LANES = 128
ROPE_THETA = 10000.0
LN_EPS = 1e-5
RMS_EPS = 1e-6
TINY = 1e-30
DEPTH = 2
ALPHA = (2 * DEPTH) ** 0.25
PAGE_SIZE = 128

HG_HEADS = 8
NSA_HEADS = 8
NSA_KV_HEADS = 2
NSA_GROUP = NSA_HEADS // NSA_KV_HEADS
CMP_BLOCK = 32
SEL_BLOCK = 64
SEL_TOPN = 8
SEL_FORCE = 1e4
NSA_WINDOW = 512
DIL_HEADS = 16
DIL_CONFIGS = ((128, 1), (512, 4), (2048, 16))
PEER_HEADS = 8
PEER_NKEYS = 128
PEER_TOPK = 16
SAMPLE_ROWS = 8

VMEM_LIMIT = 56 * 1024 * 1024


def _cparams(sem):
    return pltpu.CompilerParams(dimension_semantics=sem, vmem_limit_bytes=VMEM_LIMIT)


def _dot(a, b):
    return jnp.dot(a, b, preferred_element_type=F32)


def _dot_nt(a, b):
    return lax.dot_general(a, b, (((1,), (1,)), ((), ())), preferred_element_type=F32)


def _dot_tn(a, b):
    return lax.dot_general(a, b, (((0,), (0,)), ((), ())), preferred_element_type=F32)


def _iota(shape, dim):
    return lax.broadcasted_iota(jnp.int32, shape, dim)


def _masked_softmax(s, mask, axis):
    s = jnp.where(mask, s, -jnp.inf)
    m = jnp.max(s, axis=axis, keepdims=True)
    m = jnp.where(m > -jnp.inf, m, 0.0)
    e = jnp.exp(s - m)
    l = jnp.sum(e, axis=axis, keepdims=True)
    return m, l, e


def _rope_chunk(x, c, s):
    lane = _iota(x.shape, 1)
    sw = jnp.where((lane & 63) < 32, pltpu.roll(x, 96, 1), pltpu.roll(x, 32, 1))
    return x * c + sw * s


def _rope_tables(pos):
    half = HEAD_DIM // 2
    inv = ROPE_THETA ** (-jnp.arange(half, dtype=F32) / half)
    ang = pos.astype(F32)[:, None] * inv[None, :]
    cos, sin = jnp.cos(ang), jnp.sin(ang)
    return jnp.tile(cos, (1, 4)), jnp.tile(jnp.concatenate([-sin, sin], 1), (1, 2))


def _layer_norm(z, g, b):
    mu = jnp.mean(z, -1, keepdims=True)
    zc = z - mu
    var = jnp.mean(zc * zc, -1, keepdims=True)
    return zc * lax.rsqrt(var + LN_EPS) * g + b


EVEN_COLS = 2048 + 512 + 768 + 1536


def _even_weight(w_in):
    hperm = np.array([[j, NSA_GROUP + j] for j in range(NSA_GROUP)]).reshape(-1)
    qcols = (2048 + hperm[:, None] * HEAD_DIM + np.arange(HEAD_DIM)[None, :]).reshape(-1)
    gate0 = 2048 + 512 + 768
    gcols = []
    for c in range(3):
        for h in hperm:
            kvh, g = divmod(int(h), NSA_GROUP)
            gcols.append(np.full(HEAD_DIM, gate0 + c * NSA_HEADS + kvh * NSA_GROUP + g))
    cols = np.concatenate([np.arange(2048), qcols, np.arange(2560, 3328), np.concatenate(gcols)])
    return jnp.take(w_in, jnp.asarray(cols, jnp.int32), axis=1).astype(BF)


def _proj_even_body(layer, x_ref, w_ref, gam_ref, cos_ref, sin_ref, hq_o, hk_o, hlf_o, hv_o, hg_o,
                    qn_o, qr_o, cmp_o, slc_o, win_o, gc_o, gs_o, gw_o):
    x = x_ref[...].astype(BF)

    def mm(a, b):
        return _dot(x, w_ref[:, a:b])

    gam = gam_ref[...]
    ge = jnp.exp(gam - jnp.max(gam, axis=0, keepdims=True))
    sm = ge / jnp.sum(ge, axis=0, keepdims=True)
    lb = jnp.sum(sm[0:layer + 1], axis=0, keepdims=True)
    c = cos_ref[...]
    s = sin_ref[...]
    hq = mm(0, 512)
    hq_o[...] = hq * jax.nn.sigmoid(hq)
    f = lb + (1.0 - lb) * jax.nn.sigmoid(mm(512, 1024))
    hk_o[...] = 1.0 - f
    hlf_o[...] = jnp.log(f)
    hv_o[...] = mm(1024, 1536)
    hg = mm(1536, 2048)
    hg_o[...] = hg * jax.nn.sigmoid(hg)
    for j in range(4):
        qj = mm(2048 + LANES * j, 2048 + LANES * (j + 1))
        qn_o[:, LANES * j:LANES * (j + 1)] = qj
        qr_o[:, LANES * j:LANES * (j + 1)] = _rope_chunk(qj, c, s)
    cmp_o[...] = mm(2560, 2816)
    slc_o[:, 0:LANES] = _rope_chunk(mm(2816, 2944), c, s)
    slc_o[:, LANES:2 * LANES] = mm(2944, 3072)
    win_o[:, 0:LANES] = _rope_chunk(mm(3072, 3200), c, s)
    win_o[:, LANES:2 * LANES] = mm(3200, 3328)
    gc_o[...] = jax.nn.sigmoid(mm(3328, 3840))
    gs_o[...] = jax.nn.sigmoid(mm(3840, 4352))
    gw_o[...] = jax.nn.sigmoid(mm(4352, 4864))


def _proj_even(x, w, gamma, cos, sin, layer, tm):
    n = x.shape[0]
    widths = [512] * 7 + [256] * 3 + [512] * 3
    row = lambda w_: pl.BlockSpec((tm, w_), lambda i: (i, 0))
    full = lambda a: pl.BlockSpec(a.shape, lambda i: (0,) * a.ndim)
    return pl.pallas_call(
        functools.partial(_proj_even_body, layer),
        grid=(n // tm,),
        in_specs=[row(D_MODEL), full(w), full(gamma), row(LANES), row(LANES)],
        out_specs=[row(w_) for w_ in widths],
        out_shape=[jax.ShapeDtypeStruct((n, w_), F32) for w_ in widths],
        compiler_params=_cparams(("parallel",)),
        name="proj_even",
    )(x, w, gamma, cos, sin)


def _hgrn_consts(C):
    L = int(math.log2(C))
    t = np.arange(C)[:, None]
    i = np.arange(C)[None, :]
    mats = [i <= t]
    bms = []
    for lv in range(L):
        half = 1 << lv
        blk = 2 * half
        mid = (t // blk) * blk + half
        upper = t >= mid
        mats.append(upper & (i >= mid) & (i <= t))
        mats.append((~upper) & (i > t) & (i <= mid - 1))
        bms.append((t // blk) == (i // blk))
    bms.append(t == i)
    sel = np.concatenate(mats, 0).astype(np.float32)
    bm = np.stack(bms).astype(np.float32)
    return jnp.asarray(sel, BF), jnp.asarray(np.concatenate([bm, bm], 1), F32)


def _hgrn_body(C, L, q_ref, k_ref, lf_ref, v_ref, gate_ref, g_ref, s0_ref, sel_ref, bm_ref,
               o_ref, so_ref, st_scr):
    ci = pl.program_id(2)

    @pl.when(ci == 0)
    def _():
        st_scr[...] = s0_ref[0, 0]

    lane = _iota((C, LANES), 1)
    row = _iota((C, LANES), 0)
    lm0 = lane < HEAD_DIM
    q = q_ref[...]
    k = k_ref[...]
    lf = lf_ref[...]
    v = v_ref[...]

    def split_heads(a):
        return jnp.concatenate([jnp.where(lm0, a, 0.0), jnp.where(lm0, 0.0, a)], 0).astype(BF)

    hi = lf.astype(BF)
    r1 = lf - hi.astype(F32)
    md = r1.astype(BF)
    lo = (r1 - md.astype(F32)).astype(BF)
    seg3 = _dot(sel_ref[...], jnp.concatenate([hi, md, lo], axis=1))
    seg = seg3[:, 0:LANES] + seg3[:, LANES:2 * LANES] + seg3[:, 2 * LANES:3 * LANES]
    b = seg[0:C]

    a = _dot_nt(split_heads(q), k.astype(BF)) * bm_ref[L]
    for lv in range(L):
        up = ((row >> lv) & 1) == 1
        eu = jnp.where(up, jnp.exp(seg[(1 + 2 * lv) * C:(2 + 2 * lv) * C]), 0.0)
        el = jnp.where(up, 0.0, jnp.exp(seg[(2 + 2 * lv) * C:(3 + 2 * lv) * C]))
        a = a + _dot_nt(split_heads(q * eu), (k * el).astype(BF)) * bm_ref[lv]
    a2 = jnp.concatenate([a[0:C], a[C:2 * C]], axis=1).astype(BF)
    st = st_scr[...]
    o = _dot(a2, split_heads(v)) + _dot_nt((q * jnp.exp(b)).astype(BF), st.astype(BF))

    bend = b[C - 1:C]
    upd = _dot_tn(v.astype(BF), (k * jnp.exp(bend - b)).astype(BF))
    same_head = (_iota((LANES, LANES), 0) < HEAD_DIM) == (_iota((LANES, LANES), 1) < HEAD_DIM)
    st_new = st * jnp.exp(bend) + jnp.where(same_head, upd, 0.0)
    st_scr[...] = st_new
    so_ref[0, 0] = st_new

    o2 = o * o
    ms0 = jnp.sum(jnp.where(lm0, o2, 0.0), axis=1, keepdims=True) * (1.0 / HEAD_DIM)
    ms1 = jnp.sum(jnp.where(lm0, 0.0, o2), axis=1, keepdims=True) * (1.0 / HEAD_DIM)
    ms = jnp.where(lm0, ms0, ms1)
    o_ref[...] = o * lax.rsqrt(ms + RMS_EPS) * g_ref[...] * gate_ref[...]


def _hgrn(q, k, lf, v, gate, g, st0, B, T, C):
    L = int(math.log2(C))
    sel, bm = _hgrn_consts(C)
    nck = T // C
    tok = pl.BlockSpec((C, LANES), lambda b, p, c: (b * nck + c, p))
    stspec = pl.BlockSpec((1, 1, LANES, LANES), lambda b, p, c: (b, p, 0, 0))
    full = lambda a: pl.BlockSpec(a.shape, lambda b, p, c: (0,) * a.ndim)
    return pl.pallas_call(
        functools.partial(_hgrn_body, C, L),
        grid=(B, HG_HEADS // 2, nck),
        in_specs=[tok, tok, tok, tok, tok, pl.BlockSpec((1, LANES), lambda b, p, c: (0, p)),
                  stspec, full(sel), full(bm)],
        out_specs=[tok, stspec],
        out_shape=[jax.ShapeDtypeStruct((B * T, 512), F32),
                   jax.ShapeDtypeStruct((B, HG_HEADS // 2, LANES, LANES), F32)],
        scratch_shapes=[pltpu.VMEM((LANES, LANES), F32)],
        compiler_params=_cparams(("parallel", "parallel", "arbitrary")),
        name="hgrn2",
    )(q, k, lf, v, gate, g, st0, sel, bm)


def _state_to_pairs(s):
    B = s.shape[0]
    st = jnp.swapaxes(s, -1, -2).reshape(B, 4, 2, HEAD_DIM, HEAD_DIM)
    z = jnp.zeros_like(st[:, :, 0])
    top = jnp.concatenate([st[:, :, 0], z], -1)
    bot = jnp.concatenate([z, st[:, :, 1]], -1)
    return jnp.concatenate([top, bot], -2)


def _pairs_to_state(sp):
    B = sp.shape[0]
    a = sp[:, :, :HEAD_DIM, :HEAD_DIM]
    b = sp[:, :, HEAD_DIM:, HEAD_DIM:]
    return jnp.swapaxes(jnp.stack([a, b], 2).reshape(B, HG_HEADS, HEAD_DIM, HEAD_DIM), -1, -2)


def _stack_heads(ref, rows):
    lm0 = _iota((rows, LANES), 1) < HEAD_DIM
    parts = []
    for j in range(NSA_GROUP):
        cj = ref[:, LANES * j:LANES * (j + 1)]
        parts += [jnp.where(lm0, cj, 0.0), jnp.where(lm0, 0.0, cj)]
    return jnp.concatenate(parts, 0)


def _block_means(cmp_ref, pe_ref, kc_scr, vc_scr, nc):
    h = nc // 2
    ck = cmp_ref[:, 0:LANES].reshape(h, 2 * CMP_BLOCK, LANES)
    pe = pe_ref[...][None]
    kc_scr[0:h] = jnp.mean(ck[:, 0:CMP_BLOCK] + pe, axis=1)
    kc_scr[h:nc] = jnp.mean(ck[:, CMP_BLOCK:2 * CMP_BLOCK] + pe, axis=1)
    cv = cmp_ref[:, LANES:2 * LANES].reshape(h, 2 * CMP_BLOCK, LANES)
    vc_scr[0:h] = jnp.mean(cv[:, 0:CMP_BLOCK], axis=1)
    vc_scr[h:nc] = jnp.mean(cv[:, CMP_BLOCK:2 * CMP_BLOCK], axis=1)


def _cmp_and_select(kc, qn_st, q0, nc, nsel_rows, nsel):
    h = nc // 2
    ncol = 2 * NSA_GROUP * LANES
    s = _dot_nt(kc.astype(BF), qn_st) * (HEAD_DIM ** -0.5)
    r = _iota((nc, ncol), 0)
    cidx = jnp.where(r < h, 2 * r, 2 * (r - h) + 1)
    qpos = q0 + (_iota((nc, ncol), 1) & (LANES - 1))
    avail = (cidx + 1) * CMP_BLOCK - 1 <= qpos
    _, l, e = _masked_softmax(s, avail, 0)
    p = e / jnp.maximum(l, TINY)
    pp = p[0:h] + p[h:nc]
    w2 = 2 * LANES
    imp = pp[:, 0:w2] + pp[:, w2:2 * w2] + pp[:, 2 * w2:3 * w2] + pp[:, 3 * w2:4 * w2]
    if nsel_rows > h:
        imp = jnp.concatenate([imp, jnp.zeros((nsel_rows - h, w2), F32)], 0)
    blk = _iota((nsel_rows, w2), 0)
    qp = q0 + (_iota((nsel_rows, w2), 1) & (LANES - 1))
    forced = (blk == qp // SEL_BLOCK) | (blk == 0)
    imp = jnp.where(forced, SEL_FORCE, jnp.where(blk * SEL_BLOCK <= qp, imp, -1.0))
    imp = jnp.where(blk < nsel, imp, -2.0)
    rank = jnp.zeros((nsel_rows, w2), F32)
    for i in range(nsel):
        ri = imp[i:i + 1, :]
        beats = (ri > imp) | ((ri == imp) & (blk > i))
        rank = rank + jnp.where(beats, 1.0, 0.0)
    sel = jnp.where(rank < float(min(SEL_TOPN, nsel)), 1.0, 0.0)
    return p, sel


def _attend_rows(q_st, k, v, mask):
    s = _dot_nt(q_st, k) * (HEAD_DIM ** -0.5)
    _, l, e = _masked_softmax(s, mask, 1)
    return _dot(e.astype(BF), v) / jnp.maximum(l, TINY)


def _attend(q_st, k, v, mask, rows):
    o = _attend_rows(q_st, k, v, mask)
    lm0 = _iota((rows, LANES), 1) < HEAD_DIM
    return jnp.where(lm0, o[0:rows], o[rows:2 * rows])


def _nsa_prompt_body(T, qn_ref, qr_ref, cmp_ref, slc_ref, win_ref, gc_ref, gs_ref, gw_ref, pe_ref,
                     e_ref, o_ref, kc_scr, vc_scr):
    qb = pl.program_id(1)
    nc = T // CMP_BLOCK
    nsel = T // SEL_BLOCK
    R = LANES

    @pl.when(qb == 0)
    def _():
        _block_means(cmp_ref, pe_ref, kc_scr, vc_scr, nc)

    q0 = qb * R
    qn_st = _stack_heads(qn_ref, R).astype(BF)
    qr_st = _stack_heads(qr_ref, R).astype(BF)
    p, sel = _cmp_and_select(kc_scr[...], qn_st, q0, nc, nsel, nsel)
    oc = _dot(p.T.astype(BF), vc_scr[...].astype(BF))

    sel_e = _dot_tn(sel.astype(BF), e_ref[...])
    qpos = q0 + (_iota((2 * R, T), 0) & (R - 1))
    mask_s = (sel_e > 0.5) & (_iota((2 * R, T), 1) <= qpos)
    ks = slc_ref[:, 0:LANES].astype(BF)
    vs = slc_ref[:, LANES:2 * LANES].astype(BF)

    nw = NSA_WINDOW + R
    start = pl.multiple_of(jnp.maximum(qb - NSA_WINDOW // R, 0) * R, R)
    kw = win_ref[pl.ds(start, nw), 0:LANES].astype(BF)
    vw = win_ref[pl.ds(start, nw), LANES:2 * LANES].astype(BF)
    dist = q0 + (_iota((2 * R, nw), 0) & (R - 1)) - (start + _iota((2 * R, nw), 1))
    mask_w = (dist >= 0) & (dist <= NSA_WINDOW)

    lm0 = _iota((R, LANES), 1) < HEAD_DIM
    for j in range(NSA_GROUP):
        cols = slice(LANES * j, LANES * (j + 1))
        qj = qr_st[2 * R * j:2 * R * (j + 1)]
        o_s = _attend(qj, ks, vs, mask_s, R)
        o_w = _attend(qj, kw, vw, mask_w, R)
        o_c = jnp.where(lm0, oc[2 * R * j:2 * R * j + R], oc[2 * R * j + R:2 * R * (j + 1)])
        o_ref[:, cols] = gc_ref[:, cols] * o_c + gs_ref[:, cols] * o_s + gw_ref[:, cols] * o_w


def _sel_expand(nrows, nkeys):
    e = (np.arange(nkeys)[None, :] // SEL_BLOCK) == np.arange(nrows)[:, None]
    return jnp.asarray(e.astype(np.float32), BF)


def _nsa_prompt(qn, qr, cmp, slc, win, gc, gs, gw, pe, B, T):
    R = LANES
    nq = T // R
    tok = pl.BlockSpec((R, 512), lambda b, i: (b * nq + i, 0))
    seq = pl.BlockSpec((T, 256), lambda b, i: (b, 0))
    e = _sel_expand(T // SEL_BLOCK, T)
    full = lambda a: pl.BlockSpec(a.shape, lambda b, i: (0,) * a.ndim)
    return pl.pallas_call(
        functools.partial(_nsa_prompt_body, T),
        grid=(B, nq),
        in_specs=[tok, tok, seq, seq, seq, tok, tok, tok, full(pe), full(e)],
        out_specs=tok,
        out_shape=jax.ShapeDtypeStruct((B * T, 512), F32),
        scratch_shapes=[pltpu.VMEM((T // CMP_BLOCK, LANES), F32), pltpu.VMEM((T // CMP_BLOCK, LANES), F32)],
        compiler_params=_cparams(("parallel", "arbitrary")),
        name="nsa_prompt",
    )(qn, qr, cmp, slc, win, gc, gs, gw, pe, e)


def _nsa_sample_body(P, npages, *refs):
    (qn_ref, qr_ref, slcn_ref, winn_ref, winb_ref, gc_ref, gs_ref, gw_ref, pe_ref, e_ref) = refs[1:11]
    cmp_pages = refs[11:11 + npages]
    slc_pages = refs[11 + npages:11 + 2 * npages]
    o_ref = refs[11 + 2 * npages]
    cmp_all, slc_all, win_all, kc_scr, vc_scr = refs[12 + 2 * npages:]
    S = SAMPLE_ROWS
    nc = P // CMP_BLOCK
    nsel = -(-(P + 4) // SEL_BLOCK)
    nsel_rows = e_ref.shape[0]
    nk = slc_all.shape[0]
    lw = winb_ref.shape[1]
    nkw = win_all.shape[0]
    for pg in range(npages):
        cmp_all[PAGE_SIZE * pg:PAGE_SIZE * (pg + 1)] = cmp_pages[pg][0]
        slc_all[PAGE_SIZE * pg:PAGE_SIZE * (pg + 1)] = slc_pages[pg][0]
    slc_all[P:P + S] = slcn_ref[...]
    slc_all[P + S:nk] = jnp.zeros((nk - P - S, 256), F32)
    win_all[0:lw] = winb_ref[0]
    win_all[lw:lw + S] = winn_ref[...]
    win_all[lw + S:nkw] = jnp.zeros((nkw - lw - S, 256), F32)
    _block_means(cmp_all, pe_ref, kc_scr, vc_scr, nc)

    lm0s = _iota((S, LANES), 1) < HEAD_DIM
    zpad = jnp.zeros((LANES - S, LANES), F32)
    qn_parts, qr_parts = [], []
    for j in range(NSA_GROUP):
        cn = qn_ref[:, LANES * j:LANES * (j + 1)]
        cr = qr_ref[:, LANES * j:LANES * (j + 1)]
        qn_parts += [jnp.where(lm0s, cn, 0.0), zpad, jnp.where(lm0s, 0.0, cn), zpad]
        qr_parts += [jnp.where(lm0s, cr, 0.0), jnp.where(lm0s, 0.0, cr)]
    qn_st = jnp.concatenate(qn_parts, 0).astype(BF)
    qr_st = jnp.concatenate(qr_parts, 0).astype(BF)
    p, sel = _cmp_and_select(kc_scr[...], qn_st, P, nc, nsel_rows, nsel)
    pt = p.T
    pc = jnp.concatenate([pt[LANES * i:LANES * i + S] for i in range(2 * NSA_GROUP)], 0)
    o_c = _dot(pc.astype(BF), vc_scr[...].astype(BF))

    R = 2 * NSA_GROUP * S
    sel_e = _dot_tn(sel.astype(BF), e_ref[...])
    sel_c = jnp.concatenate([sel_e[0:S], sel_e[LANES:LANES + S]] * NSA_GROUP, 0)
    qpos = P + (_iota((R, nk), 0) & (S - 1))
    mask_s = (sel_c > 0.5) & (_iota((R, nk), 1) <= qpos)
    o_s = _attend_rows(qr_st, slc_all[:, 0:LANES].astype(BF), slc_all[:, LANES:2 * LANES].astype(BF), mask_s)
    dist = (_iota((R, nkw), 0) & (S - 1)) + lw - _iota((R, nkw), 1)
    mask_w = (dist >= 0) & (dist <= NSA_WINDOW)
    o_w = _attend_rows(qr_st, win_all[:, 0:LANES].astype(BF), win_all[:, LANES:2 * LANES].astype(BF), mask_w)
    for j in range(NSA_GROUP):
        cols = slice(LANES * j, LANES * (j + 1))
        r0 = slice(2 * S * j, 2 * S * j + S)
        r1 = slice(2 * S * j + S, 2 * S * (j + 1))
        pick = lambda a: jnp.where(lm0s, a[r0], a[r1])
        o_ref[:, cols] = gc_ref[:, cols] * pick(o_c) + gs_ref[:, cols] * pick(o_s) + gw_ref[:, cols] * pick(o_w)


def _nsa_sample(page_table, qn, qr, slc_new, win_new, win_buf, gc, gs, gw, pe, cmp_pool, slc_pool, B, P):
    S = SAMPLE_ROWS
    npages = P // PAGE_SIZE
    nk = P + LANES
    nsel_rows = 8 * (-(-(-(-(P + 4) // SEL_BLOCK)) // 8))
    e = _sel_expand(nsel_rows, nk)
    tok = lambda w_: pl.BlockSpec((S, w_), lambda b, pt: (b, 0))
    full = lambda a: pl.BlockSpec(a.shape, lambda b, pt: (0,) * a.ndim)
    page = lambda pg: pl.BlockSpec((1, PAGE_SIZE, 256), lambda b, pt: (pt[b * npages + pg], 0, 0))
    in_specs = ([tok(512), tok(512), tok(256), tok(256),
                 pl.BlockSpec((1, win_buf.shape[1], 256), lambda b, pt: (b, 0, 0)),
                 tok(512), tok(512), tok(512), full(pe), full(e)]
                + [page(pg) for pg in range(npages)] * 2)
    gs_ = pltpu.PrefetchScalarGridSpec(
        num_scalar_prefetch=1, grid=(B,), in_specs=in_specs, out_specs=tok(512),
        scratch_shapes=[pltpu.VMEM((P, 256), F32), pltpu.VMEM((nk, 256), F32),
                        pltpu.VMEM((win_buf.shape[1] + LANES, 256), F32),
                        pltpu.VMEM((P // CMP_BLOCK, LANES), F32), pltpu.VMEM((P // CMP_BLOCK, LANES), F32)])
    return pl.pallas_call(
        functools.partial(_nsa_sample_body, P, npages),
        grid_spec=gs_,
        out_shape=jax.ShapeDtypeStruct((B * S, 512), F32),
        compiler_params=_cparams(("arbitrary",)),
        name="nsa_sample",
    )(page_table.reshape(-1), qn, qr, slc_new, win_new, win_buf, gc, gs, gw, pe, e,
      *([cmp_pool] * npages), *([slc_pool] * npages))


def _outproj_ln_body(a_ref, b_ref, x_ref, wa_ref, wb_ref, g_ref, bb_ref, y_ref):
    mix = _dot(a_ref[...].astype(BF), wa_ref[...]) + _dot(b_ref[...].astype(BF), wb_ref[...])
    y_ref[...] = _layer_norm(ALPHA * x_ref[...] + mix, g_ref[...], bb_ref[...])


def _outproj_ln(a, acol, b, bcol, x, wa, wb, g, bb, tm):
    n = x.shape[0]
    full = lambda t: pl.BlockSpec(t.shape, lambda i: (0,) * t.ndim)
    return pl.pallas_call(
        _outproj_ln_body,
        grid=(n // tm,),
        in_specs=[pl.BlockSpec((tm, 512), lambda i: (i, acol)), pl.BlockSpec((tm, 512), lambda i: (i, bcol)),
                  pl.BlockSpec((tm, D_MODEL), lambda i: (i, 0)), full(wa), full(wb), full(g), full(bb)],
        out_specs=pl.BlockSpec((tm, D_MODEL), lambda i: (i, 0)),
        out_shape=jax.ShapeDtypeStruct((n, D_MODEL), F32),
        compiler_params=_cparams(("parallel",)),
        name="outproj_ln",
    )(a, b, x, wa, wb, g, bb)


def _proj_odd_body(x_ref, w_ref, cos_ref, sin_ref, q_o, kv_o):
    x = x_ref[...].astype(BF)
    c = cos_ref[...]
    s = sin_ref[...]
    nchunk = DIL_HEADS * HEAD_DIM // LANES
    for j in range(nchunk):
        cols = slice(LANES * j, LANES * (j + 1))
        q_o[:, cols] = _rope_chunk(_dot(x, w_ref[:, cols]), c, s)
        kv_o[:, cols] = _rope_chunk(_dot(x, w_ref[:, D_MODEL + LANES * j:D_MODEL + LANES * (j + 1)]), c, s)
    kv_o[:, D_MODEL:2 * D_MODEL] = _dot(x, w_ref[:, 2 * D_MODEL:3 * D_MODEL])


def _proj_odd(x, w, cos, sin, tm):
    n = x.shape[0]
    row = lambda w_: pl.BlockSpec((tm, w_), lambda i: (i, 0))
    return pl.pallas_call(
        _proj_odd_body,
        grid=(n // tm,),
        in_specs=[row(D_MODEL), pl.BlockSpec(w.shape, lambda i: (0, 0)), row(LANES), row(LANES)],
        out_specs=[row(D_MODEL), row(2 * D_MODEL)],
        out_shape=[jax.ShapeDtypeStruct((n, D_MODEL), F32), jax.ShapeDtypeStruct((n, 2 * D_MODEL), F32)],
        compiler_params=_cparams(("parallel",)),
        name="proj_odd",
    )(x, w, cos, sin)


def _dil_prompt_body(T, q_ref, k_ref, v_ref, o_ref, acc_scr, m_scr, l_scr):
    R = LANES
    lm0 = _iota((R, LANES), 1) < HEAD_DIM
    for ci, (window, d) in enumerate(DIL_CONFIGS):
        band = window // d
        nblk = T // d // R
        for r in range(d):
            for i in range(nblk):
                q0 = r + d * R * i
                rows_q = pl.ds(q0, R, stride=d) if d > 1 else pl.ds(q0, R)
                if i > 0:
                    k0, nk = q0 - d * R, 2 * R
                else:
                    k0, nk = q0, R
                rows_k = pl.ds(k0, nk, stride=d) if d > 1 else pl.ds(k0, nk)
                qs = q_ref[rows_q, :]
                q_st = jnp.concatenate([jnp.where(lm0, qs, 0.0), jnp.where(lm0, 0.0, qs)], 0).astype(BF)
                ks = k_ref[rows_k, :].astype(BF)
                vs = v_ref[rows_k, :].astype(BF)
                dist = (_iota((2 * R, nk), 0) & (R - 1)) + (nk - R) - _iota((2 * R, nk), 1)
                s = _dot_nt(q_st, ks) * (HEAD_DIM ** -0.5)
                m, l, e = _masked_softmax(s, (dist >= 0) & (dist <= band), 1)
                acc = _dot(e.astype(BF), vs)
                acc_scr[ci, rows_q, :] = jnp.where(lm0, acc[0:R], acc[R:2 * R])
                m_scr[ci, rows_q, :] = jnp.where(lm0, m[0:R], m[R:2 * R])
                l_scr[ci, rows_q, :] = jnp.where(lm0, l[0:R], l[R:2 * R])
    ncfg = len(DIL_CONFIGS)
    mx = m_scr[0]
    for ci in range(1, ncfg):
        mx = jnp.maximum(mx, m_scr[ci])
    num = jnp.zeros((T, LANES), F32)
    den = jnp.zeros((T, LANES), F32)
    for ci in range(ncfg):
        w = jnp.exp(m_scr[ci] - mx)
        num = num + w * acc_scr[ci]
        den = den + w * l_scr[ci]
    o_ref[...] = num / den


def _dil_prompt(q, kv, B, T):
    npair = DIL_HEADS // 2
    ncfg = len(DIL_CONFIGS)
    return pl.pallas_call(
        functools.partial(_dil_prompt_body, T),
        grid=(B, npair),
        in_specs=[pl.BlockSpec((T, LANES), lambda b, p: (b, p)),
                  pl.BlockSpec((T, LANES), lambda b, p: (b, p)),
                  pl.BlockSpec((T, LANES), lambda b, p: (b, npair + p))],
        out_specs=pl.BlockSpec((T, LANES), lambda b, p: (b, p)),
        out_shape=jax.ShapeDtypeStruct((B * T, D_MODEL), F32),
        scratch_shapes=[pltpu.VMEM((ncfg, T, LANES), F32)] * 3,
        compiler_params=_cparams(("parallel", "parallel")),
        name="dil_prompt",
    )(q, kv, kv)


def _dil_sample_body(Lb, q_ref, kvn_ref, kb_ref, vb_ref, o_ref):
    S = SAMPLE_ROWS
    H = DIL_HEADS
    R = S * H
    hm = (_iota((H, D_MODEL), 1) // HEAD_DIM) == _iota((H, D_MODEL), 0)
    q = q_ref[...]
    qbd = jnp.concatenate([jnp.where(hm, q[t:t + 1, :], 0.0) for t in range(S)], 0).astype(BF)
    s_b = _dot_nt(kb_ref[0].astype(BF), qbd) * (HEAD_DIM ** -0.5)
    s_n = _dot_nt(kvn_ref[:, 0:D_MODEL].astype(BF), qbd) * (HEAD_DIM ** -0.5)
    dist_b = Lb + _iota((Lb, R), 1) // H - _iota((Lb, R), 0)
    dist_n = _iota((S, R), 1) // H - _iota((S, R), 0)
    stats = []
    for window, d in DIL_CONFIGS:
        ok_b = (dist_b <= window) & ((dist_b & (d - 1)) == 0)
        ok_n = (dist_n >= 0) & ((dist_n & (d - 1)) == 0)
        sb = jnp.where(ok_b, s_b, -jnp.inf)
        sn = jnp.where(ok_n, s_n, -jnp.inf)
        m = jnp.maximum(jnp.max(sb, axis=0, keepdims=True), jnp.max(sn, axis=0, keepdims=True))
        m = jnp.where(m > -jnp.inf, m, 0.0)
        eb = jnp.exp(sb - m)
        en = jnp.exp(sn - m)
        l = jnp.sum(eb, axis=0, keepdims=True) + jnp.sum(en, axis=0, keepdims=True)
        stats.append((m, l, eb, en))
    mx = functools.reduce(jnp.maximum, [st[0] for st in stats])
    pb = jnp.zeros((Lb, R), F32)
    pn = jnp.zeros((S, R), F32)
    den = jnp.zeros((1, R), F32)
    for m, l, eb, en in stats:
        w = jnp.exp(m - mx)
        pb = pb + w * eb
        pn = pn + w * en
        den = den + w * l
    inv = 1.0 / den
    o = _dot_tn((pb * inv).astype(BF), vb_ref[0].astype(BF)) \
        + _dot_tn((pn * inv).astype(BF), kvn_ref[:, D_MODEL:2 * D_MODEL].astype(BF))
    keep = (_iota((R, D_MODEL), 1) // HEAD_DIM) == (_iota((R, D_MODEL), 0) & (H - 1))
    o_ref[...] = jnp.sum(jnp.where(keep, o, 0.0).reshape(S, H, D_MODEL), axis=1)


def _dil_sample(q, kv_new, buf, B):
    S = SAMPLE_ROWS
    Lb = buf.shape[1]
    return pl.pallas_call(
        functools.partial(_dil_sample_body, Lb),
        grid=(B,),
        in_specs=[pl.BlockSpec((S, D_MODEL), lambda b: (b, 0)),
                  pl.BlockSpec((S, 2 * D_MODEL), lambda b: (b, 0)),
                  pl.BlockSpec((1, Lb, D_MODEL), lambda b: (b, 0, 0)),
                  pl.BlockSpec((1, Lb, D_MODEL), lambda b: (b, 0, 1))],
        out_specs=pl.BlockSpec((S, D_MODEL), lambda b: (b, 0)),
        out_shape=jax.ShapeDtypeStruct((B * S, D_MODEL), F32),
        compiler_params=_cparams(("parallel",)),
        name="dil_sample",
    )(q, kv_new, buf, buf)


def _pop_max(work, iota_k, nrow):
    m = jnp.max(work, axis=0, keepdims=True)
    idx = jnp.min(jnp.where(work == m, iota_k, float(nrow)), axis=0, keepdims=True)
    return m, idx, jnp.where(iota_k == idx, -jnp.inf, work)


def _peer_topk_body(y_ref, wq_ref, keys_ref, r2_o, e2_o, cnt_o, w1_o):
    tn = y_ref.shape[0]
    K = PEER_TOPK
    NK = PEER_NKEYS
    q = _dot(y_ref[...].astype(BF), wq_ref[...]).astype(BF)
    s1 = _dot_nt(keys_ref[0, 0], q[:, 0:LANES])
    s2 = _dot_nt(keys_ref[0, 1], q[:, LANES:2 * LANES])
    iota_k = _iota((NK, tn), 0).astype(F32)
    iota_r = _iota((K, tn), 0)

    w1_, w2_ = s1, s2
    v1, i1, i2 = [], [], []
    v2 = jnp.zeros((K, tn), F32)
    for it in range(K):
        m, idx, w1_ = _pop_max(w1_, iota_k, NK)
        v1.append(m)
        i1.append(idx)
        m, idx, w2_ = _pop_max(w2_, iota_k, NK)
        v2 = jnp.where(iota_r == it, m, v2)
        i2.append(idx)

    rows = [K] + [8] * (K - 1)
    pieces = []
    for r1 in range(K):
        piece = v1[r1] + v2[0:rows[r1]]
        nvalid = K // (r1 + 1)
        if nvalid < rows[r1]:
            piece = jnp.where(_iota((rows[r1], tn), 0) < nvalid, piece, -jnp.inf)
        pieces.append(piece)
    cand = jnp.concatenate(pieces, 0)
    nc = cand.shape[0]
    iota_c = _iota((nc, tn), 0).astype(F32)
    work = cand
    for it in range(K):
        _, _, work = _pop_max(work, iota_c, nc)
    sel = (work == -jnp.inf) & (cand > -jnp.inf)
    z = jnp.sum(jnp.where(sel, jnp.exp(cand - cand[0:1]), 0.0), axis=0, keepdims=True)
    self_ = jnp.where(sel, 1.0, 0.0)

    r2 = jnp.full((NK, tn), float(K), F32)
    cntk = jnp.zeros((NK, tn), F32)
    off = 0
    for r in range(K):
        cnt_r = jnp.sum(self_[off:off + rows[r]], axis=0, keepdims=True)
        off += rows[r]
        cntk = jnp.where(iota_k == i1[r], cnt_r, cntk)
        r2 = jnp.where(iota_k == i2[r], float(r), r2)
    r2_o[0] = r2.astype(BF)
    e2_o[0] = jnp.exp(s2 - v2[0:1]).astype(BF)
    cnt_o[0] = cntk
    w1_o[0] = jnp.exp(s1 - v1[0]) / z


def _peer_topk(y, wq, keys, tn):
    n = y.shape[0]
    out = pl.BlockSpec((1, PEER_NKEYS, tn), lambda i, h: (h, 0, i))
    shp = lambda dt: jax.ShapeDtypeStruct((PEER_HEADS, PEER_NKEYS, n), dt)
    return pl.pallas_call(
        _peer_topk_body,
        grid=(n // tn, PEER_HEADS),
        in_specs=[pl.BlockSpec((tn, D_MODEL), lambda i, h: (i, 0)),
                  pl.BlockSpec((D_MODEL, 2 * LANES), lambda i, h: (0, h)),
                  pl.BlockSpec((1, 2, PEER_NKEYS, LANES), lambda i, h: (h, 0, 0, 0))],
        out_specs=[out] * 4,
        out_shape=[shp(BF), shp(BF), shp(F32), shp(F32)],
        compiler_params=_cparams(("parallel", "arbitrary")),
        name="peer_topk",
    )(y, wq, keys)


def _gelu(x):
    return 0.5 * x * (1.0 + lax.erf(x * np.float32(math.sqrt(0.5))))


def _peer_main_body(npe, xt_ref, u_ref, vt_ref, r2_ref, e2_ref, cnt_ref, w1_ref, yt_ref):
    e = pl.program_id(1)
    tn = xt_ref.shape[1]

    @pl.when(e == 0)
    def _():
        yt_ref[...] = jnp.zeros(yt_ref.shape, F32)

    xt = xt_ref[...]
    sub = 2 * PEER_NKEYS
    acc = yt_ref[...]
    nsub = npe // 2
    act = lambda s: _gelu(_dot(u_ref[sub * s:sub * (s + 1), :], xt))
    a_next = act(0)
    for s in range(nsub):
        a = a_next
        if s + 1 < nsub:
            a_next = act(s + 1)
        parts = []
        for cc in range(2):
            c = e * npe + 2 * s + cc
            g = jnp.zeros((PEER_NKEYS, tn), BF)
            for h in range(PEER_HEADS):
                cnt_row = cnt_ref[h, pl.ds(c, 1), :].astype(BF)
                w_row = w1_ref[h, pl.ds(c, 1), :].astype(BF)
                g = g + jnp.where(r2_ref[h] < cnt_row, e2_ref[h] * w_row, jnp.zeros((), BF))
            parts.append(g * a[PEER_NKEYS * cc:PEER_NKEYS * (cc + 1)].astype(BF))
        acc = acc + _dot(vt_ref[:, sub * s:sub * (s + 1)], jnp.concatenate(parts, 0))
    yt_ref[...] = acc


def _peer_main(xt, u, vt, r2, e2, cnt, w1, tn, te):
    n = xt.shape[1]
    ne = u.shape[0]
    npe = te // PEER_NKEYS
    tab = pl.BlockSpec((PEER_HEADS, PEER_NKEYS, tn), lambda i, e: (0, 0, i))
    return pl.pallas_call(
        functools.partial(_peer_main_body, npe),
        grid=(n // tn, ne // te),
        in_specs=[pl.BlockSpec((D_MODEL, tn), lambda i, e: (0, i)),
                  pl.BlockSpec((te, D_MODEL), lambda i, e: (e, 0)),
                  pl.BlockSpec((D_MODEL, te), lambda i, e: (0, e)),
                  tab, tab, tab, tab],
        out_specs=pl.BlockSpec((D_MODEL, tn), lambda i, e: (0, i)),
        out_shape=jax.ShapeDtypeStruct((D_MODEL, n), F32),
        compiler_params=_cparams(("parallel", "arbitrary")),
        name="peer_main",
    )(xt, u, vt, r2, e2, cnt, w1)


def _ln_t_body(x_ref, ft_ref, g_ref, b_ref, y_ref):
    y_ref[...] = _layer_norm(ALPHA * x_ref[...] + ft_ref[...].T, g_ref[...], b_ref[...])


def _ln_t(x, ft, g, b, tn):
    n = x.shape[0]
    full = lambda t: pl.BlockSpec(t.shape, lambda i: (0,) * t.ndim)
    return pl.pallas_call(
        _ln_t_body,
        grid=(n // tn,),
        in_specs=[pl.BlockSpec((tn, D_MODEL), lambda i: (i, 0)), pl.BlockSpec((D_MODEL, tn), lambda i: (0, i)),
                  full(g), full(b)],
        out_specs=pl.BlockSpec((tn, D_MODEL), lambda i: (i, 0)),
        out_shape=jax.ShapeDtypeStruct((n, D_MODEL), F32),
        compiler_params=_cparams(("parallel",)),
        name="ln_residual",
    )(x, ft, g, b)


def _peer_layer(y, wq, keys, u, v, g, b):
    r2, e2, cnt, w1 = _peer_topk(y, wq.astype(BF), keys.astype(BF), LANES)
    ft = _peer_main(y.T.astype(BF), u.astype(BF), v.T.astype(BF), r2, e2, cnt, w1, 512, 2048)
    return _ln_t(y, ft, g, b, 256)


def _pad_rows(a, S):
    return jnp.pad(a, ((0, 0), (0, S - a.shape[1])) + ((0, 0),) * (a.ndim - 2))


def kernel(x_prompt, x_sample, state_hgrn, cache_cmp_kv, cache_slc_kv, cache_win_kv, cache_dil_kv, page_table,
           hg_gamma, even_w_in, even_w_out, hg_norm_g, nsa_cmp_pe, odd_w_in, odd_w_out, ln_mix_g, ln_mix_b,
           peer_w_q, peer_sub_keys, peer_u, peer_v, ln_ffn_g, ln_ffn_b):
    B, T, D = x_prompt.shape
    Bs, Ts, _ = x_sample.shape
    S = SAMPLE_ROWS
    P = page_table.shape[1] * PAGE_SIZE
    npr = B * T
    yp = x_prompt.reshape(npr, D)
    ys = _pad_rows(x_sample, S).reshape(Bs * S, D)
    cos_p, sin_p = _rope_tables(jnp.tile(jnp.arange(T), B))
    cos_s, sin_s = _rope_tables(jnp.tile(P + jnp.arange(S), Bs))
    live = (jnp.arange(Bs * S) % S < Ts)[:, None]
    hperm = np.array([[j, NSA_GROUP + j] for j in range(NSA_GROUP)]).reshape(-1)
    outs = {}

    for layer in range(DEPTH):
        row2 = lambda a: a[layer].reshape(1, D)
        if layer % 2 == 0:
            e = layer // 2
            w = _even_weight(even_w_in[e])
            wo = even_w_out[e]
            wa = wo[:512].astype(BF)
            wb = wo[512:].reshape(NSA_HEADS, HEAD_DIM, D)[hperm].reshape(512, D).astype(BF)
            g = hg_norm_g[e].reshape(1, 512)
            pe = nsa_cmp_pe[e].reshape(CMP_BLOCK, LANES)
            hq, hk, hlf, hv, hg, qn, qr, cmp, slc, win, gc, gs, gw = _proj_even(yp, w, hg_gamma, cos_p, sin_p, layer, 256)
            st0 = jnp.zeros((B, HG_HEADS // 2, LANES, LANES), F32)
            o_hg, st = _hgrn(hq, hk, hlf, hv, hg, g, st0, B, T, LANES)
            nsa = _nsa_prompt(qn, qr, cmp, slc, win, gc, gs, gw, pe, B, T)
            mp = (o_hg, nsa)
            kv5 = lambda a, b_, t_: a.reshape(b_, t_, 2, NSA_KV_HEADS, HEAD_DIM)
            outs.setdefault("hg_p", []).append(_pairs_to_state(st))
            outs.setdefault("cmp_p", []).append(kv5(cmp, B, T))
            outs.setdefault("slc_p", []).append(kv5(slc, B, T))
            outs.setdefault("win_p", []).append(kv5(win, B, T)[:, -min(NSA_WINDOW, T):])
            hq, hk, hlf, hv, hg, qn, qr, cmp, slc, win, gc, gs, gw = _proj_even(ys, w, hg_gamma, cos_s, sin_s, layer, 256)
            hk = jnp.where(live, hk, 0.0)
            hlf = jnp.where(live, hlf, 0.0)
            C = 2 * S
            pad = lambda a: _pad_rows(a.reshape(Bs, S, 512), C).reshape(Bs * C, 512)
            o_hg, st = _hgrn(pad(hq), pad(hk), pad(hlf), pad(hv), pad(hg), g,
                             _state_to_pairs(state_hgrn[e].astype(F32)), Bs, C, C)
            o_hg = o_hg.reshape(Bs, C, 512)[:, :S].reshape(Bs * S, 512)
            npool = cache_cmp_kv.shape[1]
            nsa = _nsa_sample(page_table, qn, qr, slc, win, cache_win_kv[e].reshape(Bs, -1, 256), gc, gs, gw, pe,
                              cache_cmp_kv[e].reshape(npool, PAGE_SIZE, 256),
                              cache_slc_kv[e].reshape(npool, PAGE_SIZE, 256), Bs, P)
            ms = (o_hg, nsa)
            outs.setdefault("hg_s", []).append(_pairs_to_state(st))
            outs.setdefault("cmp_s", []).append(kv5(cmp, Bs, S)[:, :Ts])
            outs.setdefault("slc_s", []).append(kv5(slc, Bs, S)[:, :Ts])
            outs.setdefault("win_s", []).append(kv5(win, Bs, S)[:, :Ts])
            acol, bcol = 0, 0
        else:
            o = layer // 2
            w = odd_w_in[o].astype(BF)
            wo = odd_w_out[o]
            wa = wo[:512].astype(BF)
            wb = wo[512:].astype(BF)
            q, kv = _proj_odd(yp, w, cos_p, sin_p, 256)
            att = _dil_prompt(q, kv, B, T)
            mp = (att, att)
            kv6 = lambda a, b_, t_: a.reshape(b_, t_, 2, DIL_HEADS, HEAD_DIM)
            outs.setdefault("dil_p", []).append(kv6(kv, B, T)[:, -min(DIL_CONFIGS[-1][0], T):])
            q, kv = _proj_odd(ys, w, cos_s, sin_s, 256)
            buf = cache_dil_kv[o]
            att = _dil_sample(q, kv, buf.reshape(Bs, buf.shape[1], 2 * D), Bs)
            ms = (att, att)
            outs.setdefault("dil_s", []).append(kv6(kv, Bs, S)[:, :Ts])
            acol, bcol = 0, 1
        lg, lbias = row2(ln_mix_g), row2(ln_mix_b)
        yp = _outproj_ln(mp[0], acol, mp[1], bcol, yp, wa, wb, lg, lbias, 256)
        ys = _outproj_ln(ms[0], acol, ms[1], bcol, ys, wa, wb, lg, lbias, 256)
        y = jnp.concatenate([yp, ys], 0)
        y = _peer_layer(y, peer_w_q[layer], peer_sub_keys[layer], peer_u[layer], peer_v[layer],
                        row2(ln_ffn_g), row2(ln_ffn_b))
        yp, ys = y[:npr], y[npr:]

    stack = lambda k_: jnp.stack(outs[k_])
    return (yp.reshape(B, T, D), ys.reshape(Bs, S, D)[:, :Ts], stack("hg_p"), stack("hg_s"),
            stack("cmp_p"), stack("cmp_s"), stack("slc_p"), stack("slc_s"), stack("win_p"), stack("win_s"),
            stack("dil_p"), stack("dil_s"))
```

```python
import functools
import math

import numpy as np
import jax
import jax.numpy as jnp
from jax import lax
from jax.experimental import pallas as pl
from jax.experimental.pallas import tpu as pltpu

F32 = jnp.float32
BF = jnp.bfloat16

D_MODEL = 1024
HEAD_DIM = 64
LANES = 128
ROPE_THETA = 10000.0
LN_EPS = 1e-5
RMS_EPS = 1e-6
TINY = 1e-30
DEPTH = 2
ALPHA = (2 * DEPTH) ** 0.25
PAGE_SIZE = 128

HG_HEADS = 8
NSA_HEADS = 8
NSA_KV_HEADS = 2
NSA_GROUP = NSA_HEADS // NSA_KV_HEADS
CMP_BLOCK = 32
SEL_BLOCK = 64
SEL_TOPN = 8
SEL_FORCE = 1e4
NSA_WINDOW = 512
DIL_HEADS = 16
DIL_CONFIGS = ((128, 1), (512, 4), (2048, 16))
PEER_HEADS = 8
PEER_NKEYS = 128
PEER_TOPK = 16
SAMPLE_ROWS = 8

VMEM_LIMIT = 56 * 1024 * 1024


def _cparams(sem):
    return pltpu.CompilerParams(dimension_semantics=sem, vmem_limit_bytes=VMEM_LIMIT)


def _dot(a, b):
    return jnp.dot(a, b, preferred_element_type=F32)


def _dot_nt(a, b):
    return lax.dot_general(a, b, (((1,), (1,)), ((), ())), preferred_element_type=F32)


def _dot_tn(a, b):
    return lax.dot_general(a, b, (((0,), (0,)), ((), ())), preferred_element_type=F32)


def _iota(shape, dim):
    return lax.broadcasted_iota(jnp.int32, shape, dim)


def _masked_softmax(s, mask, axis):
    s = jnp.where(mask, s, -jnp.inf)
    m = jnp.max(s, axis=axis, keepdims=True)
    m = jnp.where(m > -jnp.inf, m, 0.0)
    e = jnp.exp(s - m)
    l = jnp.sum(e, axis=axis, keepdims=True)
    return m, l, e


def _rope_chunk(x, c, s):
    lane = _iota(x.shape, 1)
    sw = jnp.where((lane & 63) < 32, pltpu.roll(x, 96, 1), pltpu.roll(x, 32, 1))
    return x * c + sw * s


def _rope_tables(pos):
    half = HEAD_DIM // 2
    inv = ROPE_THETA ** (-jnp.arange(half, dtype=F32) / half)
    ang = pos.astype(F32)[:, None] * inv[None, :]
    cos, sin = jnp.cos(ang), jnp.sin(ang)
    return jnp.tile(cos, (1, 4)), jnp.tile(jnp.concatenate([-sin, sin], 1), (1, 2))


def _layer_norm(z, g, b):
    mu = jnp.mean(z, -1, keepdims=True)
    zc = z - mu
    var = jnp.mean(zc * zc, -1, keepdims=True)
    return zc * lax.rsqrt(var + LN_EPS) * g + b


EVEN_COLS = 2048 + 512 + 768 + 1536


def _even_weight(w_in):
    hperm = np.array([[j, NSA_GROUP + j] for j in range(NSA_GROUP)]).reshape(-1)
    qcols = (2048 + hperm[:, None] * HEAD_DIM + np.arange(HEAD_DIM)[None, :]).reshape(-1)
    gate0 = 2048 + 512 + 768
    gcols = []
    for c in range(3):
        for h in hperm:
            kvh, g = divmod(int(h), NSA_GROUP)
            gcols.append(np.full(HEAD_DIM, gate0 + c * NSA_HEADS + kvh * NSA_GROUP + g))
    cols = np.concatenate([np.arange(2048), qcols, np.arange(2560, 3328), np.concatenate(gcols)])
    return jnp.take(w_in, jnp.asarray(cols, jnp.int32), axis=1).astype(BF)


def _proj_even_body(layer, x_ref, w_ref, gam_ref, cos_ref, sin_ref, hq_o, hk_o, hlf_o, hv_o, hg_o,
                    qn_o, qr_o, cmp_o, slc_o, win_o, gc_o, gs_o, gw_o):
    x = x_ref[...].astype(BF)

    def mm(a, b):
        return _dot(x, w_ref[:, a:b])

    gam = gam_ref[...]
    ge = jnp.exp(gam - jnp.max(gam, axis=0, keepdims=True))
    sm = ge / jnp.sum(ge, axis=0, keepdims=True)
    lb = jnp.sum(sm[0:layer + 1], axis=0, keepdims=True)
    c = cos_ref[...]
    s = sin_ref[...]
    hq = mm(0, 512)
    hq_o[...] = hq * jax.nn.sigmoid(hq)
    f = lb + (1.0 - lb) * jax.nn.sigmoid(mm(512, 1024))
    hk_o[...] = 1.0 - f
    hlf_o[...] = jnp.log(f)
    hv_o[...] = mm(1024, 1536)
    hg = mm(1536, 2048)
    hg_o[...] = hg * jax.nn.sigmoid(hg)
    for j in range(4):
        qj = mm(2048 + LANES * j, 2048 + LANES * (j + 1))
        qn_o[:, LANES * j:LANES * (j + 1)] = qj
        qr_o[:, LANES * j:LANES * (j + 1)] = _rope_chunk(qj, c, s)
    cmp_o[...] = mm(2560, 2816)
    slc_o[:, 0:LANES] = _rope_chunk(mm(2816, 2944), c, s)
    slc_o[:, LANES:2 * LANES] = mm(2944, 3072)
    win_o[:, 0:LANES] = _rope_chunk(mm(3072, 3200), c, s)
    win_o[:, LANES:2 * LANES] = mm(3200, 3328)
    gc_o[...] = jax.nn.sigmoid(mm(3328, 3840))
    gs_o[...] = jax.nn.sigmoid(mm(3840, 4352))
    gw_o[...] = jax.nn.sigmoid(mm(4352, 4864))


def _proj_even(x, w, gamma, cos, sin, layer, tm):
    n = x.shape[0]
    widths = [512] * 7 + [256] * 3 + [512] * 3
    row = lambda w_: pl.BlockSpec((tm, w_), lambda i: (i, 0))
    full = lambda a: pl.BlockSpec(a.shape, lambda i: (0,) * a.ndim)
    return pl.pallas_call(
        functools.partial(_proj_even_body, layer),
        grid=(n // tm,),
        in_specs=[row(D_MODEL), full(w), full(gamma), row(LANES), row(LANES)],
        out_specs=[row(w_) for w_ in widths],
        out_shape=[jax.ShapeDtypeStruct((n, w_), F32) for w_ in widths],
        compiler_params=_cparams(("parallel",)),
        name="proj_even",
    )(x, w, gamma, cos, sin)


def _hgrn_consts(C):
    L = int(math.log2(C))
    t = np.arange(C)[:, None]
    i = np.arange(C)[None, :]
    mats = [i <= t]
    bms = []
    for lv in range(L):
        half = 1 << lv
        blk = 2 * half
        mid = (t // blk) * blk + half
        upper = t >= mid
        mats.append(upper & (i >= mid) & (i <= t))
        mats.append((~upper) & (i > t) & (i <= mid - 1))
        bms.append((t // blk) == (i // blk))
    bms.append(t == i)
    sel = np.concatenate(mats, 0).astype(np.float32)
    bm = np.stack(bms).astype(np.float32)
    return jnp.asarray(sel, BF), jnp.asarray(np.concatenate([bm, bm], 1), F32)


def _hgrn_body(C, L, q_ref, k_ref, lf_ref, v_ref, gate_ref, g_ref, s0_ref, sel_ref, bm_ref,
               o_ref, so_ref, st_scr):
    ci = pl.program_id(1)
    npair = HG_HEADS // 2

    @pl.when(ci == 0)
    def _():
        st_scr[...] = s0_ref[0]

    lane = _iota((C, LANES), 1)
    row = _iota((C, LANES), 0)
    lm0 = lane < HEAD_DIM
    same_head = (_iota((LANES, LANES), 0) < HEAD_DIM) == (_iota((LANES, LANES), 1) < HEAD_DIM)

    def split_heads(a):
        return jnp.concatenate([jnp.where(lm0, a, 0.0), jnp.where(lm0, 0.0, a)], 0).astype(BF)

    lf = lf_ref[...]
    hi = lf.astype(BF)
    r1 = lf - hi.astype(F32)
    md = r1.astype(BF)
    lo = (r1 - md.astype(F32)).astype(BF)
    seg3 = _dot(sel_ref[...], jnp.concatenate([hi, md, lo], axis=1))
    w = npair * LANES
    seg_all = seg3[:, 0:w] + seg3[:, w:2 * w] + seg3[:, 2 * w:3 * w]

    for p in range(npair):
        cols = slice(LANES * p, LANES * (p + 1))
        q = q_ref[:, cols]
        k = k_ref[:, cols]
        v = v_ref[:, cols]
        seg = seg_all[:, cols]
        b = seg[0:C]
        a = _dot_nt(split_heads(q), k.astype(BF)) * bm_ref[L]
        for lv in range(L):
            up = ((row >> lv) & 1) == 1
            eu = jnp.where(up, jnp.exp(seg[(1 + 2 * lv) * C:(2 + 2 * lv) * C]), 0.0)
            el = jnp.where(up, 0.0, jnp.exp(seg[(2 + 2 * lv) * C:(3 + 2 * lv) * C]))
            a = a + _dot_nt(split_heads(q * eu), (k * el).astype(BF)) * bm_ref[lv]
        a2 = jnp.concatenate([a[0:C], a[C:2 * C]], axis=1).astype(BF)
        st = st_scr[p]
        o = _dot(a2, split_heads(v)) + _dot_nt((q * jnp.exp(b)).astype(BF), st.astype(BF))

        bend = b[C - 1:C]
        upd = _dot_tn(v.astype(BF), (k * jnp.exp(bend - b)).astype(BF))
        st_new = st * jnp.exp(bend) + jnp.where(same_head, upd, 0.0)
        st_scr[p] = st_new
        so_ref[0, p] = st_new

        o2 = o * o
        ms0 = jnp.sum(jnp.where(lm0, o2, 0.0), axis=1, keepdims=True) * (1.0 / HEAD_DIM)
        ms1 = jnp.sum(jnp.where(lm0, 0.0, o2), axis=1, keepdims=True) * (1.0 / HEAD_DIM)
        ms = jnp.where(lm0, ms0, ms1)
        o_ref[:, cols] = o * lax.rsqrt(ms + RMS_EPS) * g_ref[:, cols] * gate_ref[:, cols]


def _hgrn(q, k, lf, v, gate, g, st0, B, T, C):
    L = int(math.log2(C))
    sel, bm = _hgrn_consts(C)
    nck = T // C
    npair = HG_HEADS // 2
    tok = pl.BlockSpec((C, npair * LANES), lambda b, c: (b * nck + c, 0))
    stspec = pl.BlockSpec((1, npair, LANES, LANES), lambda b, c: (b, 0, 0, 0))
    full = lambda a: pl.BlockSpec(a.shape, lambda b, c: (0,) * a.ndim)
    return pl.pallas_call(
        functools.partial(_hgrn_body, C, L),
        grid=(B, nck),
        in_specs=[tok, tok, tok, tok, tok, full(g), stspec, full(sel), full(bm)],
        out_specs=[tok, stspec],
        out_shape=[jax.ShapeDtypeStruct((B * T, npair * LANES), F32),
                   jax.ShapeDtypeStruct((B, npair, LANES, LANES), F32)],
        scratch_shapes=[pltpu.VMEM((npair, LANES, LANES), F32)],
        compiler_params=_cparams(("parallel", "arbitrary")),
        name="hgrn2",
    )(q, k, lf, v, gate, g, st0, sel, bm)


def _state_to_pairs(s):
    B = s.shape[0]
    st = jnp.swapaxes(s, -1, -2).reshape(B, 4, 2, HEAD_DIM, HEAD_DIM)
    z = jnp.zeros_like(st[:, :, 0])
    top = jnp.concatenate([st[:, :, 0], z], -1)
    bot = jnp.concatenate([z, st[:, :, 1]], -1)
    return jnp.concatenate([top, bot], -2)


def _pairs_to_state(sp):
    B = sp.shape[0]
    a = sp[:, :, :HEAD_DIM, :HEAD_DIM]
    b = sp[:, :, HEAD_DIM:, HEAD_DIM:]
    return jnp.swapaxes(jnp.stack([a, b], 2).reshape(B, HG_HEADS, HEAD_DIM, HEAD_DIM), -1, -2)


def _stack_heads(ref, rows):
    lm0 = _iota((rows, LANES), 1) < HEAD_DIM
    parts = []
    for j in range(NSA_GROUP):
        cj = ref[:, LANES * j:LANES * (j + 1)]
        parts += [jnp.where(lm0, cj, 0.0), jnp.where(lm0, 0.0, cj)]
    return jnp.concatenate(parts, 0)


def _block_means(cmp_ref, pe_ref, kc_scr, vc_scr, nc):
    h = nc // 2
    ck = cmp_ref[:, 0:LANES].reshape(h, 2 * CMP_BLOCK, LANES)
    pe = pe_ref[...][None]
    kc_scr[0:h] = jnp.mean(ck[:, 0:CMP_BLOCK] + pe, axis=1)
    kc_scr[h:nc] = jnp.mean(ck[:, CMP_BLOCK:2 * CMP_BLOCK] + pe, axis=1)
    cv = cmp_ref[:, LANES:2 * LANES].reshape(h, 2 * CMP_BLOCK, LANES)
    vc_scr[0:h] = jnp.mean(cv[:, 0:CMP_BLOCK], axis=1)
    vc_scr[h:nc] = jnp.mean(cv[:, CMP_BLOCK:2 * CMP_BLOCK], axis=1)


def _cmp_and_select(kc, qn_st, q0, nc, nsel_rows, nsel):
    h = nc // 2
    ncol = 2 * NSA_GROUP * LANES
    s = _dot_nt(kc.astype(BF), qn_st) * (HEAD_DIM ** -0.5)
    r = _iota((nc, ncol), 0)
    cidx = jnp.where(r < h, 2 * r, 2 * (r - h) + 1)
    qpos = q0 + (_iota((nc, ncol), 1) & (LANES - 1))
    avail = (cidx + 1) * CMP_BLOCK - 1 <= qpos
    _, l, e = _masked_softmax(s, avail, 0)
    p = e / jnp.maximum(l, TINY)
    pp = p[0:h] + p[h:nc]
    w2 = 2 * LANES
    imp = pp[:, 0:w2] + pp[:, w2:2 * w2] + pp[:, 2 * w2:3 * w2] + pp[:, 3 * w2:4 * w2]
    if nsel_rows > h:
        imp = jnp.concatenate([imp, jnp.zeros((nsel_rows - h, w2), F32)], 0)
    blk = _iota((nsel_rows, w2), 0)
    qp = q0 + (_iota((nsel_rows, w2), 1) & (LANES - 1))
    forced = (blk == qp // SEL_BLOCK) | (blk == 0)
    imp = jnp.where(forced, SEL_FORCE, jnp.where(blk * SEL_BLOCK <= qp, imp, -1.0))
    imp = jnp.where(blk < nsel, imp, -2.0)
    rank = jnp.zeros((nsel_rows, w2), F32)
    for i in range(nsel):
        ri = imp[i:i + 1, :]
        beats = (ri > imp) | ((ri == imp) & (blk > i))
        rank = rank + jnp.where(beats, 1.0, 0.0)
    sel = jnp.where(rank < float(min(SEL_TOPN, nsel)), 1.0, 0.0)
    return p, sel


def _attend_rows(q_st, k, v, mask):
    s = _dot_nt(q_st, k) * (HEAD_DIM ** -0.5)
    _, l, e = _masked_softmax(s, mask, 1)
    return _dot(e.astype(BF), v) / jnp.maximum(l, TINY)


def _attend(q_st, k, v, mask, rows):
    o = _attend_rows(q_st, k, v, mask)
    lm0 = _iota((rows, LANES), 1) < HEAD_DIM
    return jnp.where(lm0, o[0:rows], o[rows:2 * rows])


def _nsa_prompt_body(T, qn_ref, qr_ref, cmp_ref, slc_ref, win_ref, gc_ref, gs_ref, gw_ref, pe_ref,
                     e_ref, o_ref, kc_scr, vc_scr):
    qb = pl.program_id(1)
    nc = T // CMP_BLOCK
    nsel = T // SEL_BLOCK
    R = LANES

    @pl.when(qb == 0)
    def _():
        _block_means(cmp_ref, pe_ref, kc_scr, vc_scr, nc)

    q0 = qb * R
    qn_st = _stack_heads(qn_ref, R).astype(BF)
    qr_st = _stack_heads(qr_ref, R).astype(BF)
    p, sel = _cmp_and_select(kc_scr[...], qn_st, q0, nc, nsel, nsel)
    oc = _dot(p.T.astype(BF), vc_scr[...].astype(BF))

    sel_e = _dot_tn(sel.astype(BF), e_ref[...])
    qpos = q0 + (_iota((2 * R, T), 0) & (R - 1))
    mask_s = (sel_e > 0.5) & (_iota((2 * R, T), 1) <= qpos)
    ks = slc_ref[:, 0:LANES].astype(BF)
    vs = slc_ref[:, LANES:2 * LANES].astype(BF)

    nw = NSA_WINDOW + R
    start = pl.multiple_of(jnp.maximum(qb - NSA_WINDOW // R, 0) * R, R)
    kw = win_ref[pl.ds(start, nw), 0:LANES].astype(BF)
    vw = win_ref[pl.ds(start, nw), LANES:2 * LANES].astype(BF)
    dist = q0 + (_iota((2 * R, nw), 0) & (R - 1)) - (start + _iota((2 * R, nw), 1))
    mask_w = (dist >= 0) & (dist <= NSA_WINDOW)

    lm0 = _iota((R, LANES), 1) < HEAD_DIM
    for j in range(NSA_GROUP):
        cols = slice(LANES * j, LANES * (j + 1))
        qj = qr_st[2 * R * j:2 * R * (j + 1)]
        o_s = _attend(qj, ks, vs, mask_s, R)
        o_w = _attend(qj, kw, vw, mask_w, R)
        o_c = jnp.where(lm0, oc[2 * R * j:2 * R * j + R], oc[2 * R * j + R:2 * R * (j + 1)])
        o_ref[:, cols] = gc_ref[:, cols] * o_c + gs_ref[:, cols] * o_s + gw_ref[:, cols] * o_w


def _sel_expand(nrows, nkeys):
    e = (np.arange(nkeys)[None, :] // SEL_BLOCK) == np.arange(nrows)[:, None]
    return jnp.asarray(e.astype(np.float32), BF)


def _nsa_prompt(qn, qr, cmp, slc, win, gc, gs, gw, pe, B, T):
    R = LANES
    nq = T // R
    tok = pl.BlockSpec((R, 512), lambda b, i: (b * nq + i, 0))
    seq = pl.BlockSpec((T, 256), lambda b, i: (b, 0))
    e = _sel_expand(T // SEL_BLOCK, T)
    full = lambda a: pl.BlockSpec(a.shape, lambda b, i: (0,) * a.ndim)
    return pl.pallas_call(
        functools.partial(_nsa_prompt_body, T),
        grid=(B, nq),
        in_specs=[tok, tok, seq, seq, seq, tok, tok, tok, full(pe), full(e)],
        out_specs=tok,
        out_shape=jax.ShapeDtypeStruct((B * T, 512), F32),
        scratch_shapes=[pltpu.VMEM((T // CMP_BLOCK, LANES), F32), pltpu.VMEM((T // CMP_BLOCK, LANES), F32)],
        compiler_params=_cparams(("parallel", "arbitrary")),
        name="nsa_prompt",
    )(qn, qr, cmp, slc, win, gc, gs, gw, pe, e)


def _nsa_sample_body(P, npages, *refs):
    (qn_ref, qr_ref, slcn_ref, winn_ref, winb_ref, gc_ref, gs_ref, gw_ref, pe_ref, e_ref) = refs[1:11]
    cmp_pages = refs[11:11 + npages]
    slc_pages = refs[11 + npages:11 + 2 * npages]
    o_ref = refs[11 + 2 * npages]
    cmp_all, slc_all, win_all, kc_scr, vc_scr = refs[12 + 2 * npages:]
    S = SAMPLE_ROWS
    nc = P // CMP_BLOCK
    nsel = -(-(P + 4) // SEL_BLOCK)
    nsel_rows = e_ref.shape[0]
    nk = slc_all.shape[0]
    lw = winb_ref.shape[1]
    nkw = win_all.shape[0]
    for pg in range(npages):
        cmp_all[PAGE_SIZE * pg:PAGE_SIZE * (pg + 1)] = cmp_pages[pg][0]
        slc_all[PAGE_SIZE * pg:PAGE_SIZE * (pg + 1)] = slc_pages[pg][0]
    slc_all[P:P + S] = slcn_ref[...]
    slc_all[P + S:nk] = jnp.zeros((nk - P - S, 256), F32)
    win_all[0:lw] = winb_ref[0]
    win_all[lw:lw + S] = winn_ref[...]
    win_all[lw + S:nkw] = jnp.zeros((nkw - lw - S, 256), F32)
    _block_means(cmp_all, pe_ref, kc_scr, vc_scr, nc)

    lm0s = _iota((S, LANES), 1) < HEAD_DIM
    zpad = jnp.zeros((LANES - S, LANES), F32)
    qn_parts, qr_parts = [], []
    for j in range(NSA_GROUP):
        cn = qn_ref[:, LANES * j:LANES * (j + 1)]
        cr = qr_ref[:, LANES * j:LANES * (j + 1)]
        qn_parts += [jnp.where(lm0s, cn, 0.0), zpad, jnp.where(lm0s, 0.0, cn), zpad]
        qr_parts += [jnp.where(lm0s, cr, 0.0), jnp.where(lm0s, 0.0, cr)]
    qn_st = jnp.concatenate(qn_parts, 0).astype(BF)
    qr_st = jnp.concatenate(qr_parts, 0).astype(BF)
    p, sel = _cmp_and_select(kc_scr[...], qn_st, P, nc, nsel_rows, nsel)
    pt = p.T
    pc = jnp.concatenate([pt[LANES * i:LANES * i + S] for i in range(2 * NSA_GROUP)], 0)
    o_c = _dot(pc.astype(BF), vc_scr[...].astype(BF))

    R = 2 * NSA_GROUP * S
    sel_e = _dot_tn(sel.astype(BF), e_ref[...])
    sel_c = jnp.concatenate([sel_e[0:S], sel_e[LANES:LANES + S]] * NSA_GROUP, 0)
    qpos = P + (_iota((R, nk), 0) & (S - 1))
    mask_s = (sel_c > 0.5) & (_iota((R, nk), 1) <= qpos)
    o_s = _attend_rows(qr_st, slc_all[:, 0:LANES].astype(BF), slc_all[:, LANES:2 * LANES].astype(BF), mask_s)
    dist = (_iota((R, nkw), 0) & (S - 1)) + lw - _iota((R, nkw), 1)
    mask_w = (dist >= 0) & (dist <= NSA_WINDOW)
    o_w = _attend_rows(qr_st, win_all[:, 0:LANES].astype(BF), win_all[:, LANES:2 * LANES].astype(BF), mask_w)
    for j in range(NSA_GROUP):
        cols = slice(LANES * j, LANES * (j + 1))
        r0 = slice(2 * S * j, 2 * S * j + S)
        r1 = slice(2 * S * j + S, 2 * S * (j + 1))
        pick = lambda a: jnp.where(lm0s, a[r0], a[r1])
        o_ref[:, cols] = gc_ref[:, cols] * pick(o_c) + gs_ref[:, cols] * pick(o_s) + gw_ref[:, cols] * pick(o_w)


def _nsa_sample(page_table, qn, qr, slc_new, win_new, win_buf, gc, gs, gw, pe, cmp_pool, slc_pool, B, P):
    S = SAMPLE_ROWS
    npages = P // PAGE_SIZE
    nk = P + LANES
    nsel_rows = 8 * (-(-(-(-(P + 4) // SEL_BLOCK)) // 8))
    e = _sel_expand(nsel_rows, nk)
    tok = lambda w_: pl.BlockSpec((S, w_), lambda b, pt: (b, 0))
    full = lambda a: pl.BlockSpec(a.shape, lambda b, pt: (0,) * a.ndim)
    page = lambda pg: pl.BlockSpec((1, PAGE_SIZE, 256), lambda b, pt: (pt[b * npages + pg], 0, 0))
    in_specs = ([tok(512), tok(512), tok(256), tok(256),
                 pl.BlockSpec((1, win_buf.shape[1], 256), lambda b, pt: (b, 0, 0)),
                 tok(512), tok(512), tok(512), full(pe), full(e)]
                + [page(pg) for pg in range(npages)] * 2)
    gs_ = pltpu.PrefetchScalarGridSpec(
        num_scalar_prefetch=1, grid=(B,), in_specs=in_specs, out_specs=tok(512),
        scratch_shapes=[pltpu.VMEM((P, 256), F32), pltpu.VMEM((nk, 256), F32),
                        pltpu.VMEM((win_buf.shape[1] + LANES, 256), F32),
                        pltpu.VMEM((P // CMP_BLOCK, LANES), F32), pltpu.VMEM((P // CMP_BLOCK, LANES), F32)])
    return pl.pallas_call(
        functools.partial(_nsa_sample_body, P, npages),
        grid_spec=gs_,
        out_shape=jax.ShapeDtypeStruct((B * S, 512), F32),
        compiler_params=_cparams(("arbitrary",)),
        name="nsa_sample",
    )(page_table.reshape(-1), qn, qr, slc_new, win_new, win_buf, gc, gs, gw, pe, e,
      *([cmp_pool] * npages), *([slc_pool] * npages))


def _outproj_ln_body(a_ref, b_ref, x_ref, wa_ref, wb_ref, g_ref, bb_ref, y_ref):
    mix = _dot(a_ref[...].astype(BF), wa_ref[...]) + _dot(b_ref[...].astype(BF), wb_ref[...])
    y_ref[...] = _layer_norm(ALPHA * x_ref[...] + mix, g_ref[...], bb_ref[...])


def _outproj_ln(a, acol, b, bcol, x, wa, wb, g, bb, tm):
    n = x.shape[0]
    full = lambda t: pl.BlockSpec(t.shape, lambda i: (0,) * t.ndim)
    return pl.pallas_call(
        _outproj_ln_body,
        grid=(n // tm,),
        in_specs=[pl.BlockSpec((tm, 512), lambda i: (i, acol)), pl.BlockSpec((tm, 512), lambda i: (i, bcol)),
                  pl.BlockSpec((tm, D_MODEL), lambda i: (i, 0)), full(wa), full(wb), full(g), full(bb)],
        out_specs=pl.BlockSpec((tm, D_MODEL), lambda i: (i, 0)),
        out_shape=jax.ShapeDtypeStruct((n, D_MODEL), F32),
        compiler_params=_cparams(("parallel",)),
        name="outproj_ln",
    )(a, b, x, wa, wb, g, bb)


def _proj_odd_body(x_ref, w_ref, cos_ref, sin_ref, q_o, kv_o):
    x = x_ref[...].astype(BF)
    c = cos_ref[...]
    s = sin_ref[...]
    nchunk = DIL_HEADS * HEAD_DIM // LANES
    for j in range(nchunk):
        cols = slice(LANES * j, LANES * (j + 1))
        q_o[:, cols] = _rope_chunk(_dot(x, w_ref[:, cols]), c, s)
        kv_o[:, cols] = _rope_chunk(_dot(x, w_ref[:, D_MODEL + LANES * j:D_MODEL + LANES * (j + 1)]), c, s)
    kv_o[:, D_MODEL:2 * D_MODEL] = _dot(x, w_ref[:, 2 * D_MODEL:3 * D_MODEL])


def _proj_odd(x, w, cos, sin, tm):
    n = x.shape[0]
    row = lambda w_: pl.BlockSpec((tm, w_), lambda i: (i, 0))
    return pl.pallas_call(
        _proj_odd_body,
        grid=(n // tm,),
        in_specs=[row(D_MODEL), pl.BlockSpec(w.shape, lambda i: (0, 0)), row(LANES), row(LANES)],
        out_specs=[row(D_MODEL), row(2 * D_MODEL)],
        out_shape=[jax.ShapeDtypeStruct((n, D_MODEL), F32), jax.ShapeDtypeStruct((n, 2 * D_MODEL), F32)],
        compiler_params=_cparams(("parallel",)),
        name="proj_odd",
    )(x, w, cos, sin)


def _dil_prompt_body(T, q_ref, k_ref, v_ref, o_ref, acc_scr, m_scr, l_scr):
    R = LANES
    lm0 = _iota((R, LANES), 1) < HEAD_DIM
    for ci, (window, d) in enumerate(DIL_CONFIGS):
        band = window // d
        nblk = T // d // R
        for r in range(d):
            for i in range(nblk):
                q0 = r + d * R * i
                rows_q = pl.ds(q0, R, stride=d) if d > 1 else pl.ds(q0, R)
                if i > 0:
                    k0, nk = q0 - d * R, 2 * R
                else:
                    k0, nk = q0, R
                rows_k = pl.ds(k0, nk, stride=d) if d > 1 else pl.ds(k0, nk)
                qs = q_ref[rows_q, :]
                q_st = jnp.concatenate([jnp.where(lm0, qs, 0.0), jnp.where(lm0, 0.0, qs)], 0).astype(BF)
                ks = k_ref[rows_k, :].astype(BF)
                vs = v_ref[rows_k, :].astype(BF)
                dist = (_iota((2 * R, nk), 0) & (R - 1)) + (nk - R) - _iota((2 * R, nk), 1)
                s = _dot_nt(q_st, ks) * (HEAD_DIM ** -0.5)
                m, l, e = _masked_softmax(s, (dist >= 0) & (dist <= band), 1)
                acc = _dot(e.astype(BF), vs)
                acc_scr[ci, rows_q, :] = jnp.where(lm0, acc[0:R], acc[R:2 * R])
                m_scr[ci, rows_q, :] = jnp.where(lm0, m[0:R], m[R:2 * R])
                l_scr[ci, rows_q, :] = jnp.where(lm0, l[0:R], l[R:2 * R])
    ncfg = len(DIL_CONFIGS)
    mx = m_scr[0]
    for ci in range(1, ncfg):
        mx = jnp.maximum(mx, m_scr[ci])
    num = jnp.zeros((T, LANES), F32)
    den = jnp.zeros((T, LANES), F32)
    for ci in range(ncfg):
        w = jnp.exp(m_scr[ci] - mx)
        num = num + w * acc_scr[ci]
        den = den + w * l_scr[ci]
    o_ref[...] = num / den


def _dil_prompt(q, kv, B, T):
    npair = DIL_HEADS // 2
    ncfg = len(DIL_CONFIGS)
    return pl.pallas_call(
        functools.partial(_dil_prompt_body, T),
        grid=(B, npair),
        in_specs=[pl.BlockSpec((T, LANES), lambda b, p: (b, p)),
                  pl.BlockSpec((T, LANES), lambda b, p: (b, p)),
                  pl.BlockSpec((T, LANES), lambda b, p: (b, npair + p))],
        out_specs=pl.BlockSpec((T, LANES), lambda b, p: (b, p)),
        out_shape=jax.ShapeDtypeStruct((B * T, D_MODEL), F32),
        scratch_shapes=[pltpu.VMEM((ncfg, T, LANES), F32)] * 3,
        compiler_params=_cparams(("parallel", "parallel")),
        name="dil_prompt",
    )(q, kv, kv)


def _dil_live_rows(L, nres):
    dmax = DIL_CONFIGS[-1][1]
    wdense = max(w for w, d in DIL_CONFIGS if d < dmax)
    assert L % dmax == 0 and (L - wdense) % dmax == 0 and nres <= dmax
    return (L - wdense) // dmax, wdense


def _dil_sample_body(L, nres, q_ref, kvn_ref, kb_ref, vb_ref, o_ref):
    S = SAMPLE_ROWS
    H = DIL_HEADS
    R = S * H
    Lb = kb_ref.shape[1]
    ngrp, wdense = _dil_live_rows(L, nres)
    dmax = DIL_CONFIGS[-1][1]
    hm = (_iota((H, D_MODEL), 1) // HEAD_DIM) == _iota((H, D_MODEL), 0)
    q = q_ref[...]
    qbd = jnp.concatenate([jnp.where(hm, q[t:t + 1, :], 0.0) for t in range(S)], 0).astype(BF)
    s_b = _dot_nt(kb_ref[0].astype(BF), qbd) * (HEAD_DIM ** -0.5)
    s_n = _dot_nt(kvn_ref[:, 0:D_MODEL].astype(BF), qbd) * (HEAD_DIM ** -0.5)
    row = _iota((Lb, R), 0)
    na = ngrp * nres
    pos = jnp.where(row < na, dmax * (row // nres) + row % nres, L - wdense + row - na)
    dist_b = L + _iota((Lb, R), 1) // H - pos
    dist_n = _iota((S, R), 1) // H - _iota((S, R), 0)
    stats = []
    for window, d in DIL_CONFIGS:
        ok_b = (dist_b <= window) & ((dist_b & (d - 1)) == 0)
        ok_n = (dist_n >= 0) & ((dist_n & (d - 1)) == 0)
        sb = jnp.where(ok_b, s_b, -jnp.inf)
        sn = jnp.where(ok_n, s_n, -jnp.inf)
        m = jnp.maximum(jnp.max(sb, axis=0, keepdims=True), jnp.max(sn, axis=0, keepdims=True))
        m = jnp.where(m > -jnp.inf, m, 0.0)
        eb = jnp.exp(sb - m)
        en = jnp.exp(sn - m)
        l = jnp.sum(eb, axis=0, keepdims=True) + jnp.sum(en, axis=0, keepdims=True)
        stats.append((m, l, eb, en))
    mx = functools.reduce(jnp.maximum, [st[0] for st in stats])
    pb = jnp.zeros((Lb, R), F32)
    pn = jnp.zeros((S, R), F32)
    den = jnp.zeros((1, R), F32)
    for m, l, eb, en in stats:
        w = jnp.exp(m - mx)
        pb = pb + w * eb
        pn = pn + w * en
        den = den + w * l
    inv = 1.0 / den
    o = _dot_tn((pb * inv).astype(BF), vb_ref[0].astype(BF)) \
        + _dot_tn((pn * inv).astype(BF), kvn_ref[:, D_MODEL:2 * D_MODEL].astype(BF))
    keep = (_iota((R, D_MODEL), 1) // HEAD_DIM) == (_iota((R, D_MODEL), 0) & (H - 1))
    o_ref[...] = jnp.sum(jnp.where(keep, o, 0.0).reshape(S, H, D_MODEL), axis=1)


def _dil_sample(q, kv_new, cache, B, nres):
    S = SAMPLE_ROWS
    L = cache.shape[1]
    ngrp, wdense = _dil_live_rows(L, nres)
    dmax = DIL_CONFIGS[-1][1]
    grid_rows = cache.reshape(B, L // dmax, dmax, 2 * D_MODEL)[:, :ngrp, :nres].reshape(B, ngrp * nres, 2 * D_MODEL)
    buf = jnp.concatenate([grid_rows, cache[:, L - wdense:].reshape(B, wdense, 2 * D_MODEL)], axis=1)
    Lb = buf.shape[1]
    return pl.pallas_call(
        functools.partial(_dil_sample_body, L, nres),
        grid=(B,),
        in_specs=[pl.BlockSpec((S, D_MODEL), lambda b: (b, 0)),
                  pl.BlockSpec((S, 2 * D_MODEL), lambda b: (b, 0)),
                  pl.BlockSpec((1, Lb, D_MODEL), lambda b: (b, 0, 0)),
                  pl.BlockSpec((1, Lb, D_MODEL), lambda b: (b, 0, 1))],
        out_specs=pl.BlockSpec((S, D_MODEL), lambda b: (b, 0)),
        out_shape=jax.ShapeDtypeStruct((B * S, D_MODEL), F32),
        compiler_params=_cparams(("parallel",)),
        name="dil_sample",
    )(q, kv_new, buf, buf)


def _pop_max(work, iota_k, nrow):
    m = jnp.max(work, axis=0, keepdims=True)
    idx = jnp.min(jnp.where(work == m, iota_k, float(nrow)), axis=0, keepdims=True)
    return m, idx, jnp.where(iota_k == idx, -jnp.inf, work)


def _peer_topk_body(y_ref, wq_ref, keys_ref, r2_o, e2_o, cnt_o, w1_o):
    tn = y_ref.shape[0]
    K = PEER_TOPK
    NK = PEER_NKEYS
    q = _dot(y_ref[...].astype(BF), wq_ref[...]).astype(BF)
    s1 = _dot_nt(keys_ref[0, 0], q[:, 0:LANES])
    s2 = _dot_nt(keys_ref[0, 1], q[:, LANES:2 * LANES])
    iota_k = _iota((NK, tn), 0).astype(F32)
    iota_r = _iota((K, tn), 0)

    w1_, w2_ = s1, s2
    v1, i1, i2 = [], [], []
    v2 = jnp.zeros((K, tn), F32)
    for it in range(K):
        m, idx, w1_ = _pop_max(w1_, iota_k, NK)
        v1.append(m)
        i1.append(idx)
        m, idx, w2_ = _pop_max(w2_, iota_k, NK)
        v2 = jnp.where(iota_r == it, m, v2)
        i2.append(idx)

    rows = [K] + [8] * (K - 1)
    pieces = []
    for r1 in range(K):
        piece = v1[r1] + v2[0:rows[r1]]
        nvalid = K // (r1 + 1)
        if nvalid < rows[r1]:
            piece = jnp.where(_iota((rows[r1], tn), 0) < nvalid, piece, -jnp.inf)
        pieces.append(piece)
    cand = jnp.concatenate(pieces, 0)
    nc = cand.shape[0]
    iota_c = _iota((nc, tn), 0).astype(F32)
    work = cand
    for it in range(K):
        _, _, work = _pop_max(work, iota_c, nc)
    sel = (work == -jnp.inf) & (cand > -jnp.inf)
    z = jnp.sum(jnp.where(sel, jnp.exp(cand - cand[0:1]), 0.0), axis=0, keepdims=True)
    self_ = jnp.where(sel, 1.0, 0.0)

    r2 = jnp.full((NK, tn), float(K), F32)
    cntk = jnp.zeros((NK, tn), F32)
    off = 0
    for r in range(K):
        cnt_r = jnp.sum(self_[off:off + rows[r]], axis=0, keepdims=True)
        off += rows[r]
        cntk = jnp.where(iota_k == i1[r], cnt_r, cntk)
        r2 = jnp.where(iota_k == i2[r], float(r), r2)
    r2_o[0] = r2.astype(BF)
    e2_o[0] = jnp.exp(s2 - v2[0:1]).astype(BF)
    cnt_o[0] = cntk
    w1_o[0] = jnp.exp(s1 - v1[0]) / z


def _peer_topk(y, wq, keys, tn):
    n = y.shape[0]
    out = pl.BlockSpec((1, PEER_NKEYS, tn), lambda i, h: (h, 0, i))
    shp = lambda dt: jax.ShapeDtypeStruct((PEER_HEADS, PEER_NKEYS, n), dt)
    return pl.pallas_call(
        _peer_topk_body,
        grid=(n // tn, PEER_HEADS),
        in_specs=[pl.BlockSpec((tn, D_MODEL), lambda i, h: (i, 0)),
                  pl.BlockSpec((D_MODEL, 2 * LANES), lambda i, h: (0, h)),
                  pl.BlockSpec((1, 2, PEER_NKEYS, LANES), lambda i, h: (h, 0, 0, 0))],
        out_specs=[out] * 4,
        out_shape=[shp(BF), shp(BF), shp(F32), shp(F32)],
        compiler_params=_cparams(("parallel", "arbitrary")),
        name="peer_topk",
    )(y, wq, keys)


def _gelu(x):
    return 0.5 * x * (1.0 + lax.erf(x * math.sqrt(0.5)))


def _peer_main_body(npe, xt_ref, u_ref, vt_ref, r2_ref, e2_ref, cnt_ref, w1_ref, yt_ref):
    e = pl.program_id(1)
    tn = xt_ref.shape[1]

    @pl.when(e == 0)
    def _():
        yt_ref[...] = jnp.zeros(yt_ref.shape, F32)

    xt = xt_ref[...]
    sub = 2 * PEER_NKEYS
    acc = yt_ref[...]
    nsub = npe // 2
    act = lambda s: _gelu(_dot(u_ref[sub * s:sub * (s + 1), :], xt).astype(BF))
    ahead = 2
    acts = [act(s) for s in range(min(ahead, nsub))]
    for s in range(nsub):
        if s + ahead < nsub:
            acts.append(act(s + ahead))
        a = acts[s]
        parts = []
        for cc in range(2):
            c = e * npe + 2 * s + cc
            g = jnp.zeros((PEER_NKEYS, tn), BF)
            for h in range(PEER_HEADS):
                cnt_row = cnt_ref[h, pl.ds(c, 1), :].astype(BF)
                w_row = w1_ref[h, pl.ds(c, 1), :].astype(BF)
                g = g + jnp.where(r2_ref[h] < cnt_row, e2_ref[h] * w_row, jnp.zeros((), BF))
            parts.append(g * a[PEER_NKEYS * cc:PEER_NKEYS * (cc + 1)].astype(BF))
        acc = acc + _dot(vt_ref[:, sub * s:sub * (s + 1)], jnp.concatenate(parts, 0))
    yt_ref[...] = acc


def _peer_main(xt, u, vt, r2, e2, cnt, w1, tn, te):
    n = xt.shape[1]
    ne = u.shape[0]
    npe = te // PEER_NKEYS
    tab = pl.BlockSpec((PEER_HEADS, PEER_NKEYS, tn), lambda i, e: (0, 0, i))
    return pl.pallas_call(
        functools.partial(_peer_main_body, npe),
        grid=(n // tn, ne // te),
        in_specs=[pl.BlockSpec((D_MODEL, tn), lambda i, e: (0, i)),
                  pl.BlockSpec((te, D_MODEL), lambda i, e: (e, 0)),
                  pl.BlockSpec((D_MODEL, te), lambda i, e: (0, e)),
                  tab, tab, tab, tab],
        out_specs=pl.BlockSpec((D_MODEL, tn), lambda i, e: (0, i)),
        out_shape=jax.ShapeDtypeStruct((D_MODEL, n), F32),
        compiler_params=_cparams(("parallel", "arbitrary")),
        name="peer_main",
    )(xt, u, vt, r2, e2, cnt, w1)


def _ln_t_body(x_ref, ft_ref, g_ref, b_ref, y_ref):
    y_ref[...] = _layer_norm(ALPHA * x_ref[...] + ft_ref[...].T, g_ref[...], b_ref[...])


def _ln_t(x, ft, g, b, tn):
    n = x.shape[0]
    full = lambda t: pl.BlockSpec(t.shape, lambda i: (0,) * t.ndim)
    return pl.pallas_call(
        _ln_t_body,
        grid=(n // tn,),
        in_specs=[pl.BlockSpec((tn, D_MODEL), lambda i: (i, 0)), pl.BlockSpec((D_MODEL, tn), lambda i: (0, i)),
                  full(g), full(b)],
        out_specs=pl.BlockSpec((tn, D_MODEL), lambda i: (i, 0)),
        out_shape=jax.ShapeDtypeStruct((n, D_MODEL), F32),
        compiler_params=_cparams(("parallel",)),
        name="ln_residual",
    )(x, ft, g, b)


def _peer_layer(y, wq, keys, u, v, g, b):
    r2, e2, cnt, w1 = _peer_topk(y, wq.astype(BF), keys.astype(BF), LANES)
    ft = _peer_main(y.T.astype(BF), u.astype(BF), v.T.astype(BF), r2, e2, cnt, w1, 512, 2048)
    return _ln_t(y, ft, g, b, 256)


def _pad_rows(a, S):
    return jnp.pad(a, ((0, 0), (0, S - a.shape[1])) + ((0, 0),) * (a.ndim - 2))


def kernel(x_prompt, x_sample, state_hgrn, cache_cmp_kv, cache_slc_kv, cache_win_kv, cache_dil_kv, page_table,
           hg_gamma, even_w_in, even_w_out, hg_norm_g, nsa_cmp_pe, odd_w_in, odd_w_out, ln_mix_g, ln_mix_b,
           peer_w_q, peer_sub_keys, peer_u, peer_v, ln_ffn_g, ln_ffn_b):
    B, T, D = x_prompt.shape
    Bs, Ts, _ = x_sample.shape
    S = SAMPLE_ROWS
    P = page_table.shape[1] * PAGE_SIZE
    npr = B * T
    yp = x_prompt.reshape(npr, D)
    ys = _pad_rows(x_sample, S).reshape(Bs * S, D)
    cos_p, sin_p = _rope_tables(jnp.tile(jnp.arange(T), B))
    cos_s, sin_s = _rope_tables(jnp.tile(P + jnp.arange(S), Bs))
    live = (jnp.arange(Bs * S) % S < Ts)[:, None]
    hperm = np.array([[j, NSA_GROUP + j] for j in range(NSA_GROUP)]).reshape(-1)
    outs = {}

    for layer in range(DEPTH):
        row2 = lambda a: a[layer].reshape(1, D)
        if layer % 2 == 0:
            e = layer // 2
            w = _even_weight(even_w_in[e])
            wo = even_w_out[e]
            wa = wo[:512].astype(BF)
            wb = wo[512:].reshape(NSA_HEADS, HEAD_DIM, D)[hperm].reshape(512, D).astype(BF)
            g = hg_norm_g[e].reshape(1, 512)
            pe = nsa_cmp_pe[e].reshape(CMP_BLOCK, LANES)
            hq, hk, hlf, hv, hg, qn, qr, cmp, slc, win, gc, gs, gw = _proj_even(yp, w, hg_gamma, cos_p, sin_p, layer, 256)
            st0 = jnp.zeros((B, HG_HEADS // 2, LANES, LANES), F32)
            o_hg, st = _hgrn(hq, hk, hlf, hv, hg, g, st0, B, T, LANES)
            nsa = _nsa_prompt(qn, qr, cmp, slc, win, gc, gs, gw, pe, B, T)
            mp = (o_hg, nsa)
            kv5 = lambda a, b_, t_: a.reshape(b_, t_, 2, NSA_KV_HEADS, HEAD_DIM)
            outs.setdefault("hg_p", []).append(_pairs_to_state(st))
            outs.setdefault("cmp_p", []).append(kv5(cmp, B, T))
            outs.setdefault("slc_p", []).append(kv5(slc, B, T))
            outs.setdefault("win_p", []).append(kv5(win, B, T)[:, -min(NSA_WINDOW, T):])
            hq, hk, hlf, hv, hg, qn, qr, cmp, slc, win, gc, gs, gw = _proj_even(ys, w, hg_gamma, cos_s, sin_s, layer, 256)
            hk = jnp.where(live, hk, 0.0)
            hlf = jnp.where(live, hlf, 0.0)
            C = 2 * S
            pad = lambda a: _pad_rows(a.reshape(Bs, S, 512), C).reshape(Bs * C, 512)
            o_hg, st = _hgrn(pad(hq), pad(hk), pad(hlf), pad(hv), pad(hg), g,
                             _state_to_pairs(state_hgrn[e].astype(F32)), Bs, C, C)
            o_hg = o_hg.reshape(Bs, C, 512)[:, :S].reshape(Bs * S, 512)
            npool = cache_cmp_kv.shape[1]
            nsa = _nsa_sample(page_table, qn, qr, slc, win, cache_win_kv[e].reshape(Bs, -1, 256), gc, gs, gw, pe,
                              cache_cmp_kv[e].reshape(npool, PAGE_SIZE, 256),
                              cache_slc_kv[e].reshape(npool, PAGE_SIZE, 256), Bs, P)
            ms = (o_hg, nsa)
            outs.setdefault("hg_s", []).append(_pairs_to_state(st))
            outs.setdefault("cmp_s", []).append(kv5(cmp, Bs, S)[:, :Ts])
            outs.setdefault("slc_s", []).append(kv5(slc, Bs, S)[:, :Ts])
            outs.setdefault("win_s", []).append(kv5(win, Bs, S)[:, :Ts])
            acol, bcol = 0, 0
        else:
            o = layer // 2
            w = odd_w_in[o].astype(BF)
            wo = odd_w_out[o]
            wa = wo[:512].astype(BF)
            wb = wo[512:].astype(BF)
            q, kv = _proj_odd(yp, w, cos_p, sin_p, 256)
            att = _dil_prompt(q, kv, B, T)
            mp = (att, att)
            kv6 = lambda a, b_, t_: a.reshape(b_, t_, 2, DIL_HEADS, HEAD_DIM)
            outs.setdefault("dil_p", []).append(kv6(kv, B, T)[:, -min(DIL_CONFIGS[-1][0], T):])
            q, kv = _proj_odd(ys, w, cos_s, sin_s, 256)
            att = _dil_sample(q, kv, cache_dil_kv[o], Bs, Ts)
            ms = (att, att)
            outs.setdefault("dil_s", []).append(kv6(kv, Bs, S)[:, :Ts])
            acol, bcol = 0, 1
        lg, lbias = row2(ln_mix_g), row2(ln_mix_b)
        yp = _outproj_ln(mp[0], acol, mp[1], bcol, yp, wa, wb, lg, lbias, 256)
        ys = _outproj_ln(ms[0], acol, ms[1], bcol, ys, wa, wb, lg, lbias, 256)
        y = jnp.concatenate([yp, ys], 0)
        y = _peer_layer(y, peer_w_q[layer], peer_sub_keys[layer], peer_u[layer], peer_v[layer],
                        row2(ln_ffn_g), row2(ln_ffn_b))
        yp, ys = y[:npr], y[npr:]

    stack = lambda k_: jnp.stack(outs[k_])
    return (yp.reshape(B, T, D), ys.reshape(Bs, S, D)[:, :Ts], stack("hg_p"), stack("hg_s"),
            stack("cmp_p"), stack("cmp_s"), stack("slc_p"), stack("slc_s"), stack("win_p"), stack("win_s"),
            stack("dil_p"), stack("dil_s"))
```

```python
import functools
import math

import numpy as np
import jax
import jax.numpy as jnp
from jax import lax
from jax.experimental import pallas as pl
from jax.experimental.pallas import tpu as pltpu

F32 = jnp.float32
BF = jnp.bfloat16

D_MODEL = 1024
HEAD_DIM = 64
LANES = 128
ROPE_THETA = 10000.0
LN_EPS = 1e-5
RMS_EPS = 1e-6
TINY = 1e-30
DEPTH = 2
ALPHA = (2 * DEPTH) ** 0.25
PAGE_SIZE = 128

HG_HEADS = 8
NSA_HEADS = 8
NSA_KV_HEADS = 2
NSA_GROUP = NSA_HEADS // NSA_KV_HEADS
CMP_BLOCK = 32
SEL_BLOCK = 64
SEL_TOPN = 8
SEL_FORCE = 1e4
NSA_WINDOW = 512
DIL_HEADS = 16
DIL_CONFIGS = ((128, 1), (512, 4), (2048, 16))
PEER_HEADS = 8
PEER_NKEYS = 128
PEER_TOPK = 16
SAMPLE_ROWS = 8

VMEM_LIMIT = 56 * 1024 * 1024


def _cparams(sem):
    return pltpu.CompilerParams(dimension_semantics=sem, vmem_limit_bytes=VMEM_LIMIT)


def _dot(a, b):
    return jnp.dot(a, b, preferred_element_type=F32)


def _dot_nt(a, b):
    return lax.dot_general(a, b, (((1,), (1,)), ((), ())), preferred_element_type=F32)


def _dot_tn(a, b):
    return lax.dot_general(a, b, (((0,), (0,)), ((), ())), preferred_element_type=F32)


def _iota(shape, dim):
    return lax.broadcasted_iota(jnp.int32, shape, dim)


def _masked_softmax(s, mask, axis):
    s = jnp.where(mask, s, -jnp.inf)
    m = jnp.max(s, axis=axis, keepdims=True)
    m = jnp.where(m > -jnp.inf, m, 0.0)
    e = jnp.exp(s - m)
    l = jnp.sum(e, axis=axis, keepdims=True)
    return m, l, e


def _rope_chunk(x, c, s):
    lane = _iota(x.shape, 1)
    sw = jnp.where((lane & 63) < 32, pltpu.roll(x, 96, 1), pltpu.roll(x, 32, 1))
    return x * c + sw * s


def _rope_tables(pos):
    half = HEAD_DIM // 2
    inv = ROPE_THETA ** (-jnp.arange(half, dtype=F32) / half)
    ang = pos.astype(F32)[:, None] * inv[None, :]
    cos, sin = jnp.cos(ang), jnp.sin(ang)
    return jnp.tile(cos, (1, 4)), jnp.tile(jnp.concatenate([-sin, sin], 1), (1, 2))


def _layer_norm(z, g, b):
    mu = jnp.mean(z, -1, keepdims=True)
    zc = z - mu
    var = jnp.mean(zc * zc, -1, keepdims=True)
    return zc * lax.rsqrt(var + LN_EPS) * g + b


EVEN_COLS = 2048 + 512 + 768 + 1536


def _even_weight(w_in):
    hperm = np.array([[j, NSA_GROUP + j] for j in range(NSA_GROUP)]).reshape(-1)
    qcols = (2048 + hperm[:, None] * HEAD_DIM + np.arange(HEAD_DIM)[None, :]).reshape(-1)
    gate0 = 2048 + 512 + 768
    gcols = []
    for c in range(3):
        for h in hperm:
            kvh, g = divmod(int(h), NSA_GROUP)
            gcols.append(np.full(HEAD_DIM, gate0 + c * NSA_HEADS + kvh * NSA_GROUP + g))
    cols = np.concatenate([np.arange(2048), qcols, np.arange(2560, 3328), np.concatenate(gcols)])
    return jnp.take(w_in, jnp.asarray(cols, jnp.int32), axis=1).astype(BF)


def _proj_even_body(layer, x_ref, w_ref, gam_ref, cos_ref, sin_ref, hq_o, hk_o, hlf_o, hv_o, hg_o,
                    qn_o, qr_o, cmp_o, slc_o, win_o, gc_o, gs_o, gw_o):
    x = x_ref[...].astype(BF)

    def mm(a, b):
        return _dot(x, w_ref[:, a:b])

    gam = gam_ref[...]
    ge = jnp.exp(gam - jnp.max(gam, axis=0, keepdims=True))
    sm = ge / jnp.sum(ge, axis=0, keepdims=True)
    lb = jnp.sum(sm[0:layer + 1], axis=0, keepdims=True)
    c = cos_ref[...]
    s = sin_ref[...]
    hq = mm(0, 512)
    hq_o[...] = hq * jax.nn.sigmoid(hq)
    f = lb + (1.0 - lb) * jax.nn.sigmoid(mm(512, 1024))
    hk_o[...] = 1.0 - f
    hlf_o[...] = jnp.log(f)
    hv_o[...] = mm(1024, 1536)
    hg = mm(1536, 2048)
    hg_o[...] = hg * jax.nn.sigmoid(hg)
    for j in range(4):
        qj = mm(2048 + LANES * j, 2048 + LANES * (j + 1))
        qn_o[:, LANES * j:LANES * (j + 1)] = qj
        qr_o[:, LANES * j:LANES * (j + 1)] = _rope_chunk(qj, c, s)
    cmp_o[...] = mm(2560, 2816)
    slc_o[:, 0:LANES] = _rope_chunk(mm(2816, 2944), c, s)
    slc_o[:, LANES:2 * LANES] = mm(2944, 3072)
    win_o[:, 0:LANES] = _rope_chunk(mm(3072, 3200), c, s)
    win_o[:, LANES:2 * LANES] = mm(3200, 3328)
    gc_o[...] = jax.nn.sigmoid(mm(3328, 3840))
    gs_o[...] = jax.nn.sigmoid(mm(3840, 4352))
    gw_o[...] = jax.nn.sigmoid(mm(4352, 4864))


def _proj_even(x, w, gamma, cos, sin, layer, tm):
    n = x.shape[0]
    widths = [512] * 7 + [256] * 3 + [512] * 3
    row = lambda w_: pl.BlockSpec((tm, w_), lambda i: (i, 0))
    full = lambda a: pl.BlockSpec(a.shape, lambda i: (0,) * a.ndim)
    return pl.pallas_call(
        functools.partial(_proj_even_body, layer),
        grid=(n // tm,),
        in_specs=[row(D_MODEL), full(w), full(gamma), row(LANES), row(LANES)],
        out_specs=[row(w_) for w_ in widths],
        out_shape=[jax.ShapeDtypeStruct((n, w_), F32) for w_ in widths],
        compiler_params=_cparams(("parallel",)),
        name="proj_even",
    )(x, w, gamma, cos, sin)


def _hgrn_consts(C):
    L = int(math.log2(C))
    t = np.arange(C)[:, None]
    i = np.arange(C)[None, :]
    mats = [i <= t]
    bms = []
    for lv in range(L):
        half = 1 << lv
        blk = 2 * half
        mid = (t // blk) * blk + half
        upper = t >= mid
        mats.append(upper & (i >= mid) & (i <= t))
        mats.append((~upper) & (i > t) & (i <= mid - 1))
        bms.append((t // blk) == (i // blk))
    bms.append(t == i)
    sel = np.concatenate(mats, 0).astype(np.float32)
    bm = np.stack(bms).astype(np.float32)
    return jnp.asarray(sel, BF), jnp.asarray(np.concatenate([bm, bm], 1), F32)


def _hgrn_body(C, L, q_ref, k_ref, lf_ref, v_ref, gate_ref, g_ref, s0_ref, sel_ref, bm_ref,
               o_ref, so_ref, st_scr):
    ci = pl.program_id(1)
    npair = HG_HEADS // 2

    @pl.when(ci == 0)
    def _():
        st_scr[...] = s0_ref[0]

    lane = _iota((C, LANES), 1)
    row = _iota((C, LANES), 0)
    lm0 = lane < HEAD_DIM
    same_head = (_iota((LANES, LANES), 0) < HEAD_DIM) == (_iota((LANES, LANES), 1) < HEAD_DIM)

    def split_heads(a):
        return jnp.concatenate([jnp.where(lm0, a, 0.0), jnp.where(lm0, 0.0, a)], 0).astype(BF)

    lf = lf_ref[...]
    hi = lf.astype(BF)
    r1 = lf - hi.astype(F32)
    md = r1.astype(BF)
    lo = (r1 - md.astype(F32)).astype(BF)
    seg3 = _dot(sel_ref[...], jnp.concatenate([hi, md, lo], axis=1))
    w = npair * LANES
    seg_all = seg3[:, 0:w] + seg3[:, w:2 * w] + seg3[:, 2 * w:3 * w]

    for p in range(npair):
        cols = slice(LANES * p, LANES * (p + 1))
        q = q_ref[:, cols]
        k = k_ref[:, cols]
        v = v_ref[:, cols]
        seg = seg_all[:, cols]
        b = seg[0:C]
        a = _dot_nt(split_heads(q), k.astype(BF)) * bm_ref[L]
        for lv in range(L):
            up = ((row >> lv) & 1) == 1
            eu = jnp.where(up, jnp.exp(seg[(1 + 2 * lv) * C:(2 + 2 * lv) * C]), 0.0)
            el = jnp.where(up, 0.0, jnp.exp(seg[(2 + 2 * lv) * C:(3 + 2 * lv) * C]))
            a = a + _dot_nt(split_heads(q * eu), (k * el).astype(BF)) * bm_ref[lv]
        a2 = jnp.concatenate([a[0:C], a[C:2 * C]], axis=1).astype(BF)
        st = st_scr[p]
        o = _dot(a2, split_heads(v)) + _dot_nt((q * jnp.exp(b)).astype(BF), st.astype(BF))

        bend = b[C - 1:C]
        upd = _dot_tn(v.astype(BF), (k * jnp.exp(bend - b)).astype(BF))
        st_new = st * jnp.exp(bend) + jnp.where(same_head, upd, 0.0)
        st_scr[p] = st_new
        so_ref[0, p] = st_new

        o2 = o * o
        ms0 = jnp.sum(jnp.where(lm0, o2, 0.0), axis=1, keepdims=True) * (1.0 / HEAD_DIM)
        ms1 = jnp.sum(jnp.where(lm0, 0.0, o2), axis=1, keepdims=True) * (1.0 / HEAD_DIM)
        ms = jnp.where(lm0, ms0, ms1)
        o_ref[:, cols] = o * lax.rsqrt(ms + RMS_EPS) * g_ref[:, cols] * gate_ref[:, cols]


def _hgrn(q, k, lf, v, gate, g, st0, B, T, C):
    L = int(math.log2(C))
    sel, bm = _hgrn_consts(C)
    nck = T // C
    npair = HG_HEADS // 2
    tok = pl.BlockSpec((C, npair * LANES), lambda b, c: (b * nck + c, 0))
    stspec = pl.BlockSpec((1, npair, LANES, LANES), lambda b, c: (b, 0, 0, 0))
    full = lambda a: pl.BlockSpec(a.shape, lambda b, c: (0,) * a.ndim)
    return pl.pallas_call(
        functools.partial(_hgrn_body, C, L),
        grid=(B, nck),
        in_specs=[tok, tok, tok, tok, tok, full(g), stspec, full(sel), full(bm)],
        out_specs=[tok, stspec],
        out_shape=[jax.ShapeDtypeStruct((B * T, npair * LANES), F32),
                   jax.ShapeDtypeStruct((B, npair, LANES, LANES), F32)],
        scratch_shapes=[pltpu.VMEM((npair, LANES, LANES), F32)],
        compiler_params=_cparams(("parallel", "arbitrary")),
        name="hgrn2",
    )(q, k, lf, v, gate, g, st0, sel, bm)


def _state_to_pairs(s):
    B = s.shape[0]
    st = jnp.swapaxes(s, -1, -2).reshape(B, 4, 2, HEAD_DIM, HEAD_DIM)
    z = jnp.zeros_like(st[:, :, 0])
    top = jnp.concatenate([st[:, :, 0], z], -1)
    bot = jnp.concatenate([z, st[:, :, 1]], -1)
    return jnp.concatenate([top, bot], -2)


def _pairs_to_state(sp):
    B = sp.shape[0]
    a = sp[:, :, :HEAD_DIM, :HEAD_DIM]
    b = sp[:, :, HEAD_DIM:, HEAD_DIM:]
    return jnp.swapaxes(jnp.stack([a, b], 2).reshape(B, HG_HEADS, HEAD_DIM, HEAD_DIM), -1, -2)


def _stack_heads(ref, rows):
    lm0 = _iota((rows, LANES), 1) < HEAD_DIM
    parts = []
    for j in range(NSA_GROUP):
        cj = ref[:, LANES * j:LANES * (j + 1)]
        parts += [jnp.where(lm0, cj, 0.0), jnp.where(lm0, 0.0, cj)]
    return jnp.concatenate(parts, 0)


def _block_means(cmp_ref, pe_ref, kc_scr, vc_scr, nc):
    h = nc // 2
    ck = cmp_ref[:, 0:LANES].reshape(h, 2 * CMP_BLOCK, LANES)
    pe = pe_ref[...][None]
    kc_scr[0:h] = jnp.mean(ck[:, 0:CMP_BLOCK] + pe, axis=1)
    kc_scr[h:nc] = jnp.mean(ck[:, CMP_BLOCK:2 * CMP_BLOCK] + pe, axis=1)
    cv = cmp_ref[:, LANES:2 * LANES].reshape(h, 2 * CMP_BLOCK, LANES)
    vc_scr[0:h] = jnp.mean(cv[:, 0:CMP_BLOCK], axis=1)
    vc_scr[h:nc] = jnp.mean(cv[:, CMP_BLOCK:2 * CMP_BLOCK], axis=1)


def _cmp_and_select(kc, qn_st, q0, nc, nsel_rows, nsel):
    h = nc // 2
    ncol = 2 * NSA_GROUP * LANES
    s = _dot_nt(kc.astype(BF), qn_st) * (HEAD_DIM ** -0.5)
    r = _iota((nc, ncol), 0)
    cidx = jnp.where(r < h, 2 * r, 2 * (r - h) + 1)
    qpos = q0 + (_iota((nc, ncol), 1) & (LANES - 1))
    avail = (cidx + 1) * CMP_BLOCK - 1 <= qpos
    _, l, e = _masked_softmax(s, avail, 0)
    p = e / jnp.maximum(l, TINY)
    pp = p[0:h] + p[h:nc]
    w2 = 2 * LANES
    imp = pp[:, 0:w2] + pp[:, w2:2 * w2] + pp[:, 2 * w2:3 * w2] + pp[:, 3 * w2:4 * w2]
    if nsel_rows > h:
        imp = jnp.concatenate([imp, jnp.zeros((nsel_rows - h, w2), F32)], 0)
    blk = _iota((nsel_rows, w2), 0)
    qp = q0 + (_iota((nsel_rows, w2), 1) & (LANES - 1))
    forced = (blk == qp // SEL_BLOCK) | (blk == 0)
    imp = jnp.where(forced, SEL_FORCE, jnp.where(blk * SEL_BLOCK <= qp, imp, -1.0))
    imp = jnp.where(blk < nsel, imp, -2.0)
    rank = jnp.zeros((nsel_rows, w2), F32)
    for i in range(nsel):
        ri = imp[i:i + 1, :]
        beats = (ri > imp) | ((ri == imp) & (blk > i))
        rank = rank + jnp.where(beats, 1.0, 0.0)
    sel = jnp.where(rank < float(min(SEL_TOPN, nsel)), 1.0, 0.0)
    return p, sel


def _attend_rows(q_st, k, v, mask):
    s = _dot_nt(q_st, k) * (HEAD_DIM ** -0.5)
    _, l, e = _masked_softmax(s, mask, 1)
    return _dot(e.astype(BF), v) / jnp.maximum(l, TINY)


def _attend(q_st, k, v, mask, rows):
    o = _attend_rows(q_st, k, v, mask)
    lm0 = _iota((rows, LANES), 1) < HEAD_DIM
    return jnp.where(lm0, o[0:rows], o[rows:2 * rows])


def _nsa_prompt_body(T, qn_ref, qr_ref, cmp_ref, slc_ref, win_ref, gc_ref, gs_ref, gw_ref, pe_ref,
                     e_ref, o_ref, kc_scr, vc_scr):
    qb = pl.program_id(1)
    nc = T // CMP_BLOCK
    nsel = T // SEL_BLOCK
    R = LANES

    @pl.when(qb == 0)
    def _():
        _block_means(cmp_ref, pe_ref, kc_scr, vc_scr, nc)

    q0 = qb * R
    qn_st = _stack_heads(qn_ref, R).astype(BF)
    qr_st = _stack_heads(qr_ref, R).astype(BF)
    p, sel = _cmp_and_select(kc_scr[...], qn_st, q0, nc, nsel, nsel)
    oc = _dot(p.T.astype(BF), vc_scr[...].astype(BF))

    sel_e = _dot_tn(sel.astype(BF), e_ref[...])
    qpos = q0 + (_iota((2 * R, T), 0) & (R - 1))
    mask_s = (sel_e > 0.5) & (_iota((2 * R, T), 1) <= qpos)
    ks = slc_ref[:, 0:LANES].astype(BF)
    vs = slc_ref[:, LANES:2 * LANES].astype(BF)

    nw = NSA_WINDOW + R
    start = pl.multiple_of(jnp.maximum(qb - NSA_WINDOW // R, 0) * R, R)
    kw = win_ref[pl.ds(start, nw), 0:LANES].astype(BF)
    vw = win_ref[pl.ds(start, nw), LANES:2 * LANES].astype(BF)
    dist = q0 + (_iota((2 * R, nw), 0) & (R - 1)) - (start + _iota((2 * R, nw), 1))
    mask_w = (dist >= 0) & (dist <= NSA_WINDOW)

    lm0 = _iota((R, LANES), 1) < HEAD_DIM
    for j in range(NSA_GROUP):
        cols = slice(LANES * j, LANES * (j + 1))
        qj = qr_st[2 * R * j:2 * R * (j + 1)]
        o_s = _attend(qj, ks, vs, mask_s, R)
        o_w = _attend(qj, kw, vw, mask_w, R)
        o_c = jnp.where(lm0, oc[2 * R * j:2 * R * j + R], oc[2 * R * j + R:2 * R * (j + 1)])
        o_ref[:, cols] = gc_ref[:, cols] * o_c + gs_ref[:, cols] * o_s + gw_ref[:, cols] * o_w


def _sel_expand(nrows, nkeys):
    e = (np.arange(nkeys)[None, :] // SEL_BLOCK) == np.arange(nrows)[:, None]
    return jnp.asarray(e.astype(np.float32), BF)


def _nsa_prompt(qn, qr, cmp, slc, win, gc, gs, gw, pe, B, T):
    R = LANES
    nq = T // R
    tok = pl.BlockSpec((R, 512), lambda b, i: (b * nq + i, 0))
    seq = pl.BlockSpec((T, 256), lambda b, i: (b, 0))
    e = _sel_expand(T // SEL_BLOCK, T)
    full = lambda a: pl.BlockSpec(a.shape, lambda b, i: (0,) * a.ndim)
    return pl.pallas_call(
        functools.partial(_nsa_prompt_body, T),
        grid=(B, nq),
        in_specs=[tok, tok, seq, seq, seq, tok, tok, tok, full(pe), full(e)],
        out_specs=tok,
        out_shape=jax.ShapeDtypeStruct((B * T, 512), F32),
        scratch_shapes=[pltpu.VMEM((T // CMP_BLOCK, LANES), F32), pltpu.VMEM((T // CMP_BLOCK, LANES), F32)],
        compiler_params=_cparams(("parallel", "arbitrary")),
        name="nsa_prompt",
    )(qn, qr, cmp, slc, win, gc, gs, gw, pe, e)


def _nsa_sample_body(P, npages, *refs):
    (qn_ref, qr_ref, slcn_ref, winn_ref, winb_ref, gc_ref, gs_ref, gw_ref, pe_ref, e_ref) = refs[1:11]
    cmp_pages = refs[11:11 + npages]
    slc_pages = refs[11 + npages:11 + 2 * npages]
    o_ref = refs[11 + 2 * npages]
    cmp_all, slc_all, win_all, kc_scr, vc_scr = refs[12 + 2 * npages:]
    S = SAMPLE_ROWS
    nc = P // CMP_BLOCK
    nsel = -(-(P + 4) // SEL_BLOCK)
    nsel_rows = e_ref.shape[0]
    nk = slc_all.shape[0]
    lw = winb_ref.shape[1]
    nkw = win_all.shape[0]
    for pg in range(npages):
        cmp_all[PAGE_SIZE * pg:PAGE_SIZE * (pg + 1)] = cmp_pages[pg][0]
        slc_all[PAGE_SIZE * pg:PAGE_SIZE * (pg + 1)] = slc_pages[pg][0]
    slc_all[P:P + S] = slcn_ref[...]
    slc_all[P + S:nk] = jnp.zeros((nk - P - S, 256), F32)
    win_all[0:lw] = winb_ref[0]
    win_all[lw:lw + S] = winn_ref[...]
    win_all[lw + S:nkw] = jnp.zeros((nkw - lw - S, 256), F32)
    _block_means(cmp_all, pe_ref, kc_scr, vc_scr, nc)

    lm0s = _iota((S, LANES), 1) < HEAD_DIM
    zpad = jnp.zeros((LANES - S, LANES), F32)
    qn_parts, qr_parts = [], []
    for j in range(NSA_GROUP):
        cn = qn_ref[:, LANES * j:LANES * (j + 1)]
        cr = qr_ref[:, LANES * j:LANES * (j + 1)]
        qn_parts += [jnp.where(lm0s, cn, 0.0), zpad, jnp.where(lm0s, 0.0, cn), zpad]
        qr_parts += [jnp.where(lm0s, cr, 0.0), jnp.where(lm0s, 0.0, cr)]
    qn_st = jnp.concatenate(qn_parts, 0).astype(BF)
    qr_st = jnp.concatenate(qr_parts, 0).astype(BF)
    p, sel = _cmp_and_select(kc_scr[...], qn_st, P, nc, nsel_rows, nsel)
    pt = p.T
    pc = jnp.concatenate([pt[LANES * i:LANES * i + S] for i in range(2 * NSA_GROUP)], 0)
    o_c = _dot(pc.astype(BF), vc_scr[...].astype(BF))

    R = 2 * NSA_GROUP * S
    sel_e = _dot_tn(sel.astype(BF), e_ref[...])
    sel_c = jnp.concatenate([sel_e[0:S], sel_e[LANES:LANES + S]] * NSA_GROUP, 0)
    qpos = P + (_iota((R, nk), 0) & (S - 1))
    mask_s = (sel_c > 0.5) & (_iota((R, nk), 1) <= qpos)
    o_s = _attend_rows(qr_st, slc_all[:, 0:LANES].astype(BF), slc_all[:, LANES:2 * LANES].astype(BF), mask_s)
    dist = (_iota((R, nkw), 0) & (S - 1)) + lw - _iota((R, nkw), 1)
    mask_w = (dist >= 0) & (dist <= NSA_WINDOW)
    o_w = _attend_rows(qr_st, win_all[:, 0:LANES].astype(BF), win_all[:, LANES:2 * LANES].astype(BF), mask_w)
    for j in range(NSA_GROUP):
        cols = slice(LANES * j, LANES * (j + 1))
        r0 = slice(2 * S * j, 2 * S * j + S)
        r1 = slice(2 * S * j + S, 2 * S * (j + 1))
        pick = lambda a: jnp.where(lm0s, a[r0], a[r1])
        o_ref[:, cols] = gc_ref[:, cols] * pick(o_c) + gs_ref[:, cols] * pick(o_s) + gw_ref[:, cols] * pick(o_w)


def _nsa_sample(page_table, qn, qr, slc_new, win_new, win_buf, gc, gs, gw, pe, cmp_pool, slc_pool, B, P):
    S = SAMPLE_ROWS
    npages = P // PAGE_SIZE
    nk = P + LANES
    nsel_rows = 8 * (-(-(-(-(P + 4) // SEL_BLOCK)) // 8))
    e = _sel_expand(nsel_rows, nk)
    tok = lambda w_: pl.BlockSpec((S, w_), lambda b, pt: (b, 0))
    full = lambda a: pl.BlockSpec(a.shape, lambda b, pt: (0,) * a.ndim)
    page = lambda pg: pl.BlockSpec((1, PAGE_SIZE, 256), lambda b, pt: (pt[b * npages + pg], 0, 0))
    in_specs = ([tok(512), tok(512), tok(256), tok(256),
                 pl.BlockSpec((1, win_buf.shape[1], 256), lambda b, pt: (b, 0, 0)),
                 tok(512), tok(512), tok(512), full(pe), full(e)]
                + [page(pg) for pg in range(npages)] * 2)
    gs_ = pltpu.PrefetchScalarGridSpec(
        num_scalar_prefetch=1, grid=(B,), in_specs=in_specs, out_specs=tok(512),
        scratch_shapes=[pltpu.VMEM((P, 256), F32), pltpu.VMEM((nk, 256), F32),
                        pltpu.VMEM((win_buf.shape[1] + LANES, 256), F32),
                        pltpu.VMEM((P // CMP_BLOCK, LANES), F32), pltpu.VMEM((P // CMP_BLOCK, LANES), F32)])
    return pl.pallas_call(
        functools.partial(_nsa_sample_body, P, npages),
        grid_spec=gs_,
        out_shape=jax.ShapeDtypeStruct((B * S, 512), F32),
        compiler_params=_cparams(("arbitrary",)),
        name="nsa_sample",
    )(page_table.reshape(-1), qn, qr, slc_new, win_new, win_buf, gc, gs, gw, pe, e,
      *([cmp_pool] * npages), *([slc_pool] * npages))


def _outproj_ln_body(a_ref, b_ref, x_ref, wa_ref, wb_ref, g_ref, bb_ref, y_ref):
    mix = _dot(a_ref[...].astype(BF), wa_ref[...]) + _dot(b_ref[...].astype(BF), wb_ref[...])
    y_ref[...] = _layer_norm(ALPHA * x_ref[...] + mix, g_ref[...], bb_ref[...])


def _outproj_ln(a, acol, b, bcol, x, wa, wb, g, bb, tm):
    n = x.shape[0]
    full = lambda t: pl.BlockSpec(t.shape, lambda i: (0,) * t.ndim)
    return pl.pallas_call(
        _outproj_ln_body,
        grid=(n // tm,),
        in_specs=[pl.BlockSpec((tm, 512), lambda i: (i, acol)), pl.BlockSpec((tm, 512), lambda i: (i, bcol)),
                  pl.BlockSpec((tm, D_MODEL), lambda i: (i, 0)), full(wa), full(wb), full(g), full(bb)],
        out_specs=pl.BlockSpec((tm, D_MODEL), lambda i: (i, 0)),
        out_shape=jax.ShapeDtypeStruct((n, D_MODEL), F32),
        compiler_params=_cparams(("parallel",)),
        name="outproj_ln",
    )(a, b, x, wa, wb, g, bb)


def _proj_odd_body(x_ref, w_ref, cos_ref, sin_ref, q_o, kv_o):
    x = x_ref[...].astype(BF)
    c = cos_ref[...]
    s = sin_ref[...]
    nchunk = DIL_HEADS * HEAD_DIM // LANES
    for j in range(nchunk):
        cols = slice(LANES * j, LANES * (j + 1))
        q_o[:, cols] = _rope_chunk(_dot(x, w_ref[:, cols]), c, s)
        kv_o[:, cols] = _rope_chunk(_dot(x, w_ref[:, D_MODEL + LANES * j:D_MODEL + LANES * (j + 1)]), c, s)
    kv_o[:, D_MODEL:2 * D_MODEL] = _dot(x, w_ref[:, 2 * D_MODEL:3 * D_MODEL])


def _proj_odd(x, w, cos, sin, tm):
    n = x.shape[0]
    row = lambda w_: pl.BlockSpec((tm, w_), lambda i: (i, 0))
    return pl.pallas_call(
        _proj_odd_body,
        grid=(n // tm,),
        in_specs=[row(D_MODEL), pl.BlockSpec(w.shape, lambda i: (0, 0)), row(LANES), row(LANES)],
        out_specs=[row(D_MODEL), row(2 * D_MODEL)],
        out_shape=[jax.ShapeDtypeStruct((n, D_MODEL), F32), jax.ShapeDtypeStruct((n, 2 * D_MODEL), F32)],
        compiler_params=_cparams(("parallel",)),
        name="proj_odd",
    )(x, w, cos, sin)


def _dil_prompt_body(T, q_ref, k_ref, v_ref, o_ref, acc_scr, m_scr, l_scr):
    R = LANES
    lm0 = _iota((R, LANES), 1) < HEAD_DIM
    for ci, (window, d) in enumerate(DIL_CONFIGS):
        band = window // d
        nblk = T // d // R
        for r in range(d):
            for i in range(nblk):
                q0 = r + d * R * i
                rows_q = pl.ds(q0, R, stride=d) if d > 1 else pl.ds(q0, R)
                if i > 0:
                    k0, nk = q0 - d * R, 2 * R
                else:
                    k0, nk = q0, R
                rows_k = pl.ds(k0, nk, stride=d) if d > 1 else pl.ds(k0, nk)
                qs = q_ref[rows_q, :]
                q_st = jnp.concatenate([jnp.where(lm0, qs, 0.0), jnp.where(lm0, 0.0, qs)], 0).astype(BF)
                ks = k_ref[rows_k, :].astype(BF)
                vs = v_ref[rows_k, :].astype(BF)
                dist = (_iota((2 * R, nk), 0) & (R - 1)) + (nk - R) - _iota((2 * R, nk), 1)
                s = _dot_nt(q_st, ks) * (HEAD_DIM ** -0.5)
                m, l, e = _masked_softmax(s, (dist >= 0) & (dist <= band), 1)
                acc = _dot(e.astype(BF), vs)
                acc_scr[ci, rows_q, :] = jnp.where(lm0, acc[0:R], acc[R:2 * R])
                m_scr[ci, rows_q, :] = jnp.where(lm0, m[0:R], m[R:2 * R])
                l_scr[ci, rows_q, :] = jnp.where(lm0, l[0:R], l[R:2 * R])
    ncfg = len(DIL_CONFIGS)
    mx = m_scr[0]
    for ci in range(1, ncfg):
        mx = jnp.maximum(mx, m_scr[ci])
    num = jnp.zeros((T, LANES), F32)
    den = jnp.zeros((T, LANES), F32)
    for ci in range(ncfg):
        w = jnp.exp(m_scr[ci] - mx)
        num = num + w * acc_scr[ci]
        den = den + w * l_scr[ci]
    o_ref[...] = num / den


def _dil_prompt(q, kv, B, T):
    npair = DIL_HEADS // 2
    ncfg = len(DIL_CONFIGS)
    return pl.pallas_call(
        functools.partial(_dil_prompt_body, T),
        grid=(B, npair),
        in_specs=[pl.BlockSpec((T, LANES), lambda b, p: (b, p)),
                  pl.BlockSpec((T, LANES), lambda b, p: (b, p)),
                  pl.BlockSpec((T, LANES), lambda b, p: (b, npair + p))],
        out_specs=pl.BlockSpec((T, LANES), lambda b, p: (b, p)),
        out_shape=jax.ShapeDtypeStruct((B * T, D_MODEL), F32),
        scratch_shapes=[pltpu.VMEM((ncfg, T, LANES), F32)] * 3,
        compiler_params=_cparams(("parallel", "parallel")),
        name="dil_prompt",
    )(q, kv, kv)


def _dil_sample_body(L, q_ref, kvn_ref, kt_ref, vt_ref, o_ref):
    S = SAMPLE_ROWS
    H = DIL_HEADS
    R = H * S
    scale = HEAD_DIM ** -0.5
    pad16 = lambda a: jnp.concatenate([a, jnp.zeros(a.shape, F32)], 0).astype(BF)
    sb, sn = [], []
    for h in range(H):
        q16 = pad16(q_ref[h])
        sb.append(_dot(q16, kt_ref[0, 0, h].astype(BF))[0:S])
        sn.append(_dot_nt(q16, pad16(kvn_ref[h]))[0:S])
    parts = [
        (jnp.concatenate(sb, 0) * scale, L + (_iota((R, L), 0) & (S - 1)) - _iota((R, L), 1), False),
        (jnp.concatenate(sn, 0) * scale, (_iota((R, 2 * S), 0) & (S - 1)) - _iota((R, 2 * S), 1), True),
    ]
    stats = []
    for window, d in DIL_CONFIGS:
        masked = []
        for s, dist, signed in parts:
            ok = (dist <= window) & ((dist & (d - 1)) == 0)
            if signed:
                ok = ok & (dist >= 0)
            masked.append(jnp.where(ok, s, -jnp.inf))
        m = functools.reduce(jnp.maximum, [jnp.max(x, axis=1, keepdims=True) for x in masked])
        m = jnp.where(m > -jnp.inf, m, 0.0)
        es = [jnp.exp(x - m) for x in masked]
        l = functools.reduce(lambda a, b: a + b, [jnp.sum(e, axis=1, keepdims=True) for e in es])
        stats.append((m, l, es))
    mx = functools.reduce(jnp.maximum, [st[0] for st in stats])
    ws = [jnp.exp(m - mx) for m, _, _ in stats]
    inv = 1.0 / functools.reduce(lambda a, b: a + b, [w * l for w, (_, l, _) in zip(ws, stats)])
    pb, pn = [functools.reduce(lambda a, b: a + b, [w * es[pi] for w, (_, _, es) in zip(ws, stats)]) * inv
              for pi in range(2)]
    for h in range(H):
        rows = slice(S * h, S * (h + 1))
        o = _dot_nt(pad16(pb[rows]), vt_ref[0, 0, h].astype(BF)) + _dot(pad16(pn[rows]), pad16(kvn_ref[H + h]))
        o_ref[h] = o[0:S]


def _dil_sample(q, kv_new, cache, B):
    S = SAMPLE_ROWS
    L, H, Dh = cache.shape[1], cache.shape[3], cache.shape[4]
    n = B * S
    cache_t = jnp.transpose(cache, (0, 2, 3, 4, 1))
    q_h = q.reshape(n, H, Dh).transpose(1, 0, 2)
    kv_h = kv_new.reshape(n, 2 * H, Dh).transpose(1, 0, 2)
    part = lambda j: pl.BlockSpec((1, 1, H, Dh, L), lambda b: (b, j, 0, 0, 0))
    o = pl.pallas_call(
        functools.partial(_dil_sample_body, L),
        grid=(B,),
        in_specs=[pl.BlockSpec((H, S, Dh), lambda b: (0, b, 0)),
                  pl.BlockSpec((2 * H, S, Dh), lambda b: (0, b, 0)),
                  part(0), part(1)],
        out_specs=pl.BlockSpec((H, S, Dh), lambda b: (0, b, 0)),
        out_shape=jax.ShapeDtypeStruct((H, n, Dh), F32),
        compiler_params=_cparams(("parallel",)),
        name="dil_sample",
    )(q_h, kv_h, cache_t, cache_t)
    return o.transpose(1, 0, 2).reshape(n, H * Dh)


def _pop_max(work, iota_k, nrow):
    m = jnp.max(work, axis=0, keepdims=True)
    idx = jnp.min(jnp.where(work == m, iota_k, float(nrow)), axis=0, keepdims=True)
    return m, idx, jnp.where(iota_k == idx, -jnp.inf, work)


def _peer_topk_body(y_ref, wq_ref, keys_ref, r2_o, e2_o, cnt_o, w1_o):
    tn = y_ref.shape[0]
    K = PEER_TOPK
    NK = PEER_NKEYS
    q = _dot(y_ref[...].astype(BF), wq_ref[...]).astype(BF)
    s1 = _dot_nt(keys_ref[0, 0], q[:, 0:LANES])
    s2 = _dot_nt(keys_ref[0, 1], q[:, LANES:2 * LANES])
    iota_k = _iota((NK, tn), 0).astype(F32)
    iota_r = _iota((K, tn), 0)

    w1_, w2_ = s1, s2
    v1, i1, i2 = [], [], []
    v2 = jnp.zeros((K, tn), F32)
    for it in range(K):
        m, idx, w1_ = _pop_max(w1_, iota_k, NK)
        v1.append(m)
        i1.append(idx)
        m, idx, w2_ = _pop_max(w2_, iota_k, NK)
        v2 = jnp.where(iota_r == it, m, v2)
        i2.append(idx)

    rows = [K] + [8] * (K - 1)
    pieces = []
    for r1 in range(K):
        piece = v1[r1] + v2[0:rows[r1]]
        nvalid = K // (r1 + 1)
        if nvalid < rows[r1]:
            piece = jnp.where(_iota((rows[r1], tn), 0) < nvalid, piece, -jnp.inf)
        pieces.append(piece)
    cand = jnp.concatenate(pieces, 0)
    nc = cand.shape[0]
    iota_c = _iota((nc, tn), 0).astype(F32)
    work = cand
    for it in range(K):
        _, _, work = _pop_max(work, iota_c, nc)
    sel = (work == -jnp.inf) & (cand > -jnp.inf)
    z = jnp.sum(jnp.where(sel, jnp.exp(cand - cand[0:1]), 0.0), axis=0, keepdims=True)
    self_ = jnp.where(sel, 1.0, 0.0)

    r2 = jnp.full((NK, tn), float(K), F32)
    cntk = jnp.zeros((NK, tn), F32)
    off = 0
    for r in range(K):
        cnt_r = jnp.sum(self_[off:off + rows[r]], axis=0, keepdims=True)
        off += rows[r]
        cntk = jnp.where(iota_k == i1[r], cnt_r, cntk)
        r2 = jnp.where(iota_k == i2[r], float(r), r2)
    r2_o[0] = r2.astype(BF)
    e2_o[0] = jnp.exp(s2 - v2[0:1]).astype(BF)
    cnt_o[0] = cntk
    w1_o[0] = jnp.exp(s1 - v1[0]) / z


def _peer_topk(y, wq, keys, tn):
    n = y.shape[0]
    out = pl.BlockSpec((1, PEER_NKEYS, tn), lambda i, h: (h, 0, i))
    shp = lambda dt: jax.ShapeDtypeStruct((PEER_HEADS, PEER_NKEYS, n), dt)
    return pl.pallas_call(
        _peer_topk_body,
        grid=(n // tn, PEER_HEADS),
        in_specs=[pl.BlockSpec((tn, D_MODEL), lambda i, h: (i, 0)),
                  pl.BlockSpec((D_MODEL, 2 * LANES), lambda i, h: (0, h)),
                  pl.BlockSpec((1, 2, PEER_NKEYS, LANES), lambda i, h: (h, 0, 0, 0))],
        out_specs=[out] * 4,
        out_shape=[shp(BF), shp(BF), shp(F32), shp(F32)],
        compiler_params=_cparams(("parallel", "arbitrary")),
        name="peer_topk",
    )(y, wq, keys)


def _gelu(x):
    return 0.5 * x * (1.0 + lax.erf(x * math.sqrt(0.5)))


def _peer_main_body(npe, xt_ref, u_ref, vt_ref, r2_ref, e2_ref, cnt_ref, w1_ref, yt_ref):
    e = pl.program_id(1)
    tn = xt_ref.shape[1]

    @pl.when(e == 0)
    def _():
        yt_ref[...] = jnp.zeros(yt_ref.shape, F32)

    xt = xt_ref[...]
    sub = 2 * PEER_NKEYS
    acc = yt_ref[...]
    nsub = npe // 2
    act = lambda s: _gelu(_dot(u_ref[sub * s:sub * (s + 1), :], xt).astype(BF))
    ahead = 2
    acts = [act(s) for s in range(min(ahead, nsub))]
    for s in range(nsub):
        if s + ahead < nsub:
            acts.append(act(s + ahead))
        a = acts[s]
        parts = []
        for cc in range(2):
            c = e * npe + 2 * s + cc
            g = jnp.zeros((PEER_NKEYS, tn), BF)
            for h in range(PEER_HEADS):
                cnt_row = cnt_ref[h, pl.ds(c, 1), :].astype(BF)
                w_row = w1_ref[h, pl.ds(c, 1), :].astype(BF)
                g = g + jnp.where(r2_ref[h] < cnt_row, e2_ref[h] * w_row, jnp.zeros((), BF))
            parts.append(g * a[PEER_NKEYS * cc:PEER_NKEYS * (cc + 1)].astype(BF))
        acc = acc + _dot(vt_ref[:, sub * s:sub * (s + 1)], jnp.concatenate(parts, 0))
    yt_ref[...] = acc


def _peer_main(xt, u, vt, r2, e2, cnt, w1, tn, te):
    n = xt.shape[1]
    ne = u.shape[0]
    npe = te // PEER_NKEYS
    tab = pl.BlockSpec((PEER_HEADS, PEER_NKEYS, tn), lambda i, e: (0, 0, i))
    return pl.pallas_call(
        functools.partial(_peer_main_body, npe),
        grid=(n // tn, ne // te),
        in_specs=[pl.BlockSpec((D_MODEL, tn), lambda i, e: (0, i)),
                  pl.BlockSpec((te, D_MODEL), lambda i, e: (e, 0)),
                  pl.BlockSpec((D_MODEL, te), lambda i, e: (0, e)),
                  tab, tab, tab, tab],
        out_specs=pl.BlockSpec((D_MODEL, tn), lambda i, e: (0, i)),
        out_shape=jax.ShapeDtypeStruct((D_MODEL, n), F32),
        compiler_params=_cparams(("parallel", "arbitrary")),
        name="peer_main",
    )(xt, u, vt, r2, e2, cnt, w1)


def _ln_t_body(x_ref, ft_ref, g_ref, b_ref, y_ref):
    y_ref[...] = _layer_norm(ALPHA * x_ref[...] + ft_ref[...].T, g_ref[...], b_ref[...])


def _ln_t(x, ft, g, b, tn):
    n = x.shape[0]
    full = lambda t: pl.BlockSpec(t.shape, lambda i: (0,) * t.ndim)
    return pl.pallas_call(
        _ln_t_body,
        grid=(n // tn,),
        in_specs=[pl.BlockSpec((tn, D_MODEL), lambda i: (i, 0)), pl.BlockSpec((D_MODEL, tn), lambda i: (0, i)),
                  full(g), full(b)],
        out_specs=pl.BlockSpec((tn, D_MODEL), lambda i: (i, 0)),
        out_shape=jax.ShapeDtypeStruct((n, D_MODEL), F32),
        compiler_params=_cparams(("parallel",)),
        name="ln_residual",
    )(x, ft, g, b)


def _peer_layer(y, wq, keys, u, v, g, b):
    r2, e2, cnt, w1 = _peer_topk(y, wq.astype(BF), keys.astype(BF), LANES)
    ft = _peer_main(y.T.astype(BF), u.astype(BF), v.T.astype(BF), r2, e2, cnt, w1, 512, 2048)
    return _ln_t(y, ft, g, b, 256)


def _pad_rows(a, S):
    return jnp.pad(a, ((0, 0), (0, S - a.shape[1])) + ((0, 0),) * (a.ndim - 2))


def kernel(x_prompt, x_sample, state_hgrn, cache_cmp_kv, cache_slc_kv, cache_win_kv, cache_dil_kv, page_table,
           hg_gamma, even_w_in, even_w_out, hg_norm_g, nsa_cmp_pe, odd_w_in, odd_w_out, ln_mix_g, ln_mix_b,
           peer_w_q, peer_sub_keys, peer_u, peer_v, ln_ffn_g, ln_ffn_b):
    B, T, D = x_prompt.shape
    Bs, Ts, _ = x_sample.shape
    S = SAMPLE_ROWS
    P = page_table.shape[1] * PAGE_SIZE
    npr = B * T
    yp = x_prompt.reshape(npr, D)
    ys = _pad_rows(x_sample, S).reshape(Bs * S, D)
    cos_p, sin_p = _rope_tables(jnp.tile(jnp.arange(T), B))
    cos_s, sin_s = _rope_tables(jnp.tile(P + jnp.arange(S), Bs))
    live = (jnp.arange(Bs * S) % S < Ts)[:, None]
    hperm = np.array([[j, NSA_GROUP + j] for j in range(NSA_GROUP)]).reshape(-1)
    outs = {}

    for layer in range(DEPTH):
        row2 = lambda a: a[layer].reshape(1, D)
        if layer % 2 == 0:
            e = layer // 2
            w = _even_weight(even_w_in[e])
            wo = even_w_out[e]
            wa = wo[:512].astype(BF)
            wb = wo[512:].reshape(NSA_HEADS, HEAD_DIM, D)[hperm].reshape(512, D).astype(BF)
            g = hg_norm_g[e].reshape(1, 512)
            pe = nsa_cmp_pe[e].reshape(CMP_BLOCK, LANES)
            hq, hk, hlf, hv, hg, qn, qr, cmp, slc, win, gc, gs, gw = _proj_even(yp, w, hg_gamma, cos_p, sin_p, layer, 256)
            st0 = jnp.zeros((B, HG_HEADS // 2, LANES, LANES), F32)
            o_hg, st = _hgrn(hq, hk, hlf, hv, hg, g, st0, B, T, LANES)
            nsa = _nsa_prompt(qn, qr, cmp, slc, win, gc, gs, gw, pe, B, T)
            mp = (o_hg, nsa)
            kv5 = lambda a, b_, t_: a.reshape(b_, t_, 2, NSA_KV_HEADS, HEAD_DIM)
            outs.setdefault("hg_p", []).append(_pairs_to_state(st))
            outs.setdefault("cmp_p", []).append(kv5(cmp, B, T))
            outs.setdefault("slc_p", []).append(kv5(slc, B, T))
            outs.setdefault("win_p", []).append(kv5(win, B, T)[:, -min(NSA_WINDOW, T):])
            hq, hk, hlf, hv, hg, qn, qr, cmp, slc, win, gc, gs, gw = _proj_even(ys, w, hg_gamma, cos_s, sin_s, layer, 256)
            hk = jnp.where(live, hk, 0.0)
            hlf = jnp.where(live, hlf, 0.0)
            C = 2 * S
            pad = lambda a: _pad_rows(a.reshape(Bs, S, 512), C).reshape(Bs * C, 512)
            o_hg, st = _hgrn(pad(hq), pad(hk), pad(hlf), pad(hv), pad(hg), g,
                             _state_to_pairs(state_hgrn[e].astype(F32)), Bs, C, C)
            o_hg = o_hg.reshape(Bs, C, 512)[:, :S].reshape(Bs * S, 512)
            npool = cache_cmp_kv.shape[1]
            nsa = _nsa_sample(page_table, qn, qr, slc, win, cache_win_kv[e].reshape(Bs, -1, 256), gc, gs, gw, pe,
                              cache_cmp_kv[e].reshape(npool, PAGE_SIZE, 256),
                              cache_slc_kv[e].reshape(npool, PAGE_SIZE, 256), Bs, P)
            ms = (o_hg, nsa)
            outs.setdefault("hg_s", []).append(_pairs_to_state(st))
            outs.setdefault("cmp_s", []).append(kv5(cmp, Bs, S)[:, :Ts])
            outs.setdefault("slc_s", []).append(kv5(slc, Bs, S)[:, :Ts])
            outs.setdefault("win_s", []).append(kv5(win, Bs, S)[:, :Ts])
            acol, bcol = 0, 0
        else:
            o = layer // 2
            w = odd_w_in[o].astype(BF)
            wo = odd_w_out[o]
            wa = wo[:512].astype(BF)
            wb = wo[512:].astype(BF)
            q, kv = _proj_odd(yp, w, cos_p, sin_p, 256)
            att = _dil_prompt(q, kv, B, T)
            mp = (att, att)
            kv6 = lambda a, b_, t_: a.reshape(b_, t_, 2, DIL_HEADS, HEAD_DIM)
            outs.setdefault("dil_p", []).append(kv6(kv, B, T)[:, -min(DIL_CONFIGS[-1][0], T):])
            q, kv = _proj_odd(ys, w, cos_s, sin_s, 256)
            att = _dil_sample(q, kv, cache_dil_kv[o], Bs)
            ms = (att, att)
            outs.setdefault("dil_s", []).append(kv6(kv, Bs, S)[:, :Ts])
            acol, bcol = 0, 1
        lg, lbias = row2(ln_mix_g), row2(ln_mix_b)
        yp = _outproj_ln(mp[0], acol, mp[1], bcol, yp, wa, wb, lg, lbias, 256)
        ys = _outproj_ln(ms[0], acol, ms[1], bcol, ys, wa, wb, lg, lbias, 256)
        y = jnp.concatenate([yp, ys], 0)
        y = _peer_layer(y, peer_w_q[layer], peer_sub_keys[layer], peer_u[layer], peer_v[layer],
                        row2(ln_ffn_g), row2(ln_ffn_b))
        yp, ys = y[:npr], y[npr:]

    stack = lambda k_: jnp.stack(outs[k_])
    return (yp.reshape(B, T, D), ys.reshape(Bs, S, D)[:, :Ts], stack("hg_p"), stack("hg_s"),
            stack("cmp_p"), stack("cmp_s"), stack("slc_p"), stack("slc_s"), stack("win_p"), stack("win_s"),
            stack("dil_p"), stack("dil_s"))
```

```python
import functools
import math

import numpy as np
import jax
import jax.numpy as jnp
from jax import lax
from jax.experimental import pallas as pl
from jax.experimental.pallas import tpu as pltpu

F32 = jnp.float32
BF = jnp.bfloat16

D_MODEL = 1024
HEAD_DIM = 64
LANES = 128
ROPE_THETA = 10000.0
LN_EPS = 1e-5
RMS_EPS = 1e-6
TINY = 1e-30
DEPTH = 2
ALPHA = (2 * DEPTH) ** 0.25
PAGE_SIZE = 128

HG_HEADS = 8
NSA_HEADS = 8
NSA_KV_HEADS = 2
NSA_GROUP = NSA_HEADS // NSA_KV_HEADS
CMP_BLOCK = 32
SEL_BLOCK = 64
SEL_TOPN = 8
SEL_FORCE = 1e4
NSA_WINDOW = 512
DIL_HEADS = 16
DIL_CONFIGS = ((128, 1), (512, 4), (2048, 16))
PEER_HEADS = 8
PEER_NKEYS = 128
PEER_TOPK = 16
SAMPLE_ROWS = 8

VMEM_LIMIT = 56 * 1024 * 1024


def _cparams(sem):
    return pltpu.CompilerParams(dimension_semantics=sem, vmem_limit_bytes=VMEM_LIMIT)


def _dot(a, b):
    return jnp.dot(a, b, preferred_element_type=F32)


def _dot_nt(a, b):
    return lax.dot_general(a, b, (((1,), (1,)), ((), ())), preferred_element_type=F32)


def _dot_tn(a, b):
    return lax.dot_general(a, b, (((0,), (0,)), ((), ())), preferred_element_type=F32)


def _iota(shape, dim):
    return lax.broadcasted_iota(jnp.int32, shape, dim)


def _masked_softmax(s, mask, axis):
    s = jnp.where(mask, s, -jnp.inf)
    m = jnp.max(s, axis=axis, keepdims=True)
    m = jnp.where(m > -jnp.inf, m, 0.0)
    e = jnp.exp(s - m)
    l = jnp.sum(e, axis=axis, keepdims=True)
    return m, l, e


def _rope_chunk(x, c, s):
    lane = _iota(x.shape, 1)
    sw = jnp.where((lane & 63) < 32, pltpu.roll(x, 96, 1), pltpu.roll(x, 32, 1))
    return x * c + sw * s


def _rope_tables(pos):
    half = HEAD_DIM // 2
    inv = ROPE_THETA ** (-jnp.arange(half, dtype=F32) / half)
    ang = pos.astype(F32)[:, None] * inv[None, :]
    cos, sin = jnp.cos(ang), jnp.sin(ang)
    return jnp.tile(cos, (1, 4)), jnp.tile(jnp.concatenate([-sin, sin], 1), (1, 2))


def _layer_norm(z, g, b):
    mu = jnp.mean(z, -1, keepdims=True)
    zc = z - mu
    var = jnp.mean(zc * zc, -1, keepdims=True)
    return zc * lax.rsqrt(var + LN_EPS) * g + b


EVEN_COLS = 2048 + 512 + 768 + 1536


def _even_weight(w_in):
    hperm = np.array([[j, NSA_GROUP + j] for j in range(NSA_GROUP)]).reshape(-1)
    qcols = (2048 + hperm[:, None] * HEAD_DIM + np.arange(HEAD_DIM)[None, :]).reshape(-1)
    gate0 = 2048 + 512 + 768
    gcols = []
    for c in range(3):
        for h in hperm:
            kvh, g = divmod(int(h), NSA_GROUP)
            gcols.append(np.full(HEAD_DIM, gate0 + c * NSA_HEADS + kvh * NSA_GROUP + g))
    cols = np.concatenate([np.arange(2048), qcols, np.arange(2560, 3328), np.concatenate(gcols)])
    return jnp.take(w_in, jnp.asarray(cols, jnp.int32), axis=1).astype(BF)


def _proj_even_body(layer, x_ref, w_ref, gam_ref, cos_ref, sin_ref, hq_o, hk_o, hlf_o, hv_o, hg_o,
                    qn_o, qr_o, cmp_o, slc_o, win_o, gc_o, gs_o, gw_o):
    x = x_ref[...].astype(BF)

    def mm(a, b):
        return _dot(x, w_ref[:, a:b])

    gam = gam_ref[...]
    ge = jnp.exp(gam - jnp.max(gam, axis=0, keepdims=True))
    sm = ge / jnp.sum(ge, axis=0, keepdims=True)
    lb = jnp.sum(sm[0:layer + 1], axis=0, keepdims=True)
    c = cos_ref[...]
    s = sin_ref[...]
    hq = mm(0, 512)
    hq_o[...] = hq * jax.nn.sigmoid(hq)
    f = lb + (1.0 - lb) * jax.nn.sigmoid(mm(512, 1024))
    hk_o[...] = 1.0 - f
    hlf_o[...] = jnp.log(f)
    hv_o[...] = mm(1024, 1536)
    hg = mm(1536, 2048)
    hg_o[...] = hg * jax.nn.sigmoid(hg)
    for j in range(4):
        qj = mm(2048 + LANES * j, 2048 + LANES * (j + 1))
        qn_o[:, LANES * j:LANES * (j + 1)] = qj
        qr_o[:, LANES * j:LANES * (j + 1)] = _rope_chunk(qj, c, s)
    cmp_o[...] = mm(2560, 2816)
    slc_o[:, 0:LANES] = _rope_chunk(mm(2816, 2944), c, s)
    slc_o[:, LANES:2 * LANES] = mm(2944, 3072)
    win_o[:, 0:LANES] = _rope_chunk(mm(3072, 3200), c, s)
    win_o[:, LANES:2 * LANES] = mm(3200, 3328)
    gc_o[...] = jax.nn.sigmoid(mm(3328, 3840))
    gs_o[...] = jax.nn.sigmoid(mm(3840, 4352))
    gw_o[...] = jax.nn.sigmoid(mm(4352, 4864))


def _proj_even(x, w, gamma, cos, sin, layer, tm):
    n = x.shape[0]
    widths = [512] * 7 + [256] * 3 + [512] * 3
    row = lambda w_: pl.BlockSpec((tm, w_), lambda i: (i, 0))
    full = lambda a: pl.BlockSpec(a.shape, lambda i: (0,) * a.ndim)
    return pl.pallas_call(
        functools.partial(_proj_even_body, layer),
        grid=(n // tm,),
        in_specs=[row(D_MODEL), full(w), full(gamma), row(LANES), row(LANES)],
        out_specs=[row(w_) for w_ in widths],
        out_shape=[jax.ShapeDtypeStruct((n, w_), F32) for w_ in widths],
        compiler_params=_cparams(("parallel",)),
        name="proj_even",
    )(x, w, gamma, cos, sin)


def _hgrn_consts(C):
    L = int(math.log2(C))
    t = np.arange(C)[:, None]
    i = np.arange(C)[None, :]
    mats = [i <= t]
    bms = []
    for lv in range(L):
        half = 1 << lv
        blk = 2 * half
        mid = (t // blk) * blk + half
        upper = t >= mid
        mats.append(upper & (i >= mid) & (i <= t))
        mats.append((~upper) & (i > t) & (i <= mid - 1))
        bms.append((t // blk) == (i // blk))
    bms.append(t == i)
    sel = np.concatenate(mats, 0).astype(np.float32)
    bm = np.stack(bms).astype(np.float32)
    return jnp.asarray(sel, BF), jnp.asarray(np.concatenate([bm, bm], 1), F32)


def _hgrn_body(C, L, q_ref, k_ref, lf_ref, v_ref, gate_ref, g_ref, s0_ref, sel_ref, bm_ref,
               o_ref, so_ref, st_scr):
    ci = pl.program_id(1)
    npair = HG_HEADS // 2

    @pl.when(ci == 0)
    def _():
        st_scr[...] = s0_ref[0]

    lane = _iota((C, LANES), 1)
    row = _iota((C, LANES), 0)
    lm0 = lane < HEAD_DIM
    same_head = (_iota((LANES, LANES), 0) < HEAD_DIM) == (_iota((LANES, LANES), 1) < HEAD_DIM)

    def split_heads(a):
        return jnp.concatenate([jnp.where(lm0, a, 0.0), jnp.where(lm0, 0.0, a)], 0).astype(BF)

    lf = lf_ref[...]
    hi = lf.astype(BF)
    r1 = lf - hi.astype(F32)
    md = r1.astype(BF)
    lo = (r1 - md.astype(F32)).astype(BF)
    seg3 = _dot(sel_ref[...], jnp.concatenate([hi, md, lo], axis=1))
    w = npair * LANES
    seg_all = seg3[:, 0:w] + seg3[:, w:2 * w] + seg3[:, 2 * w:3 * w]

    for p in range(npair):
        cols = slice(LANES * p, LANES * (p + 1))
        q = q_ref[:, cols]
        k = k_ref[:, cols]
        v = v_ref[:, cols]
        seg = seg_all[:, cols]
        b = seg[0:C]
        a = _dot_nt(split_heads(q), k.astype(BF)) * bm_ref[L]
        for lv in range(L):
            up = ((row >> lv) & 1) == 1
            eu = jnp.where(up, jnp.exp(seg[(1 + 2 * lv) * C:(2 + 2 * lv) * C]), 0.0)
            el = jnp.where(up, 0.0, jnp.exp(seg[(2 + 2 * lv) * C:(3 + 2 * lv) * C]))
            a = a + _dot_nt(split_heads(q * eu), (k * el).astype(BF)) * bm_ref[lv]
        a2 = jnp.concatenate([a[0:C], a[C:2 * C]], axis=1).astype(BF)
        st = st_scr[p]
        o = _dot(a2, split_heads(v)) + _dot_nt((q * jnp.exp(b)).astype(BF), st.astype(BF))

        bend = b[C - 1:C]
        upd = _dot_tn(v.astype(BF), (k * jnp.exp(bend - b)).astype(BF))
        st_new = st * jnp.exp(bend) + jnp.where(same_head, upd, 0.0)
        st_scr[p] = st_new
        so_ref[0, p] = st_new

        o2 = o * o
        ms0 = jnp.sum(jnp.where(lm0, o2, 0.0), axis=1, keepdims=True) * (1.0 / HEAD_DIM)
        ms1 = jnp.sum(jnp.where(lm0, 0.0, o2), axis=1, keepdims=True) * (1.0 / HEAD_DIM)
        ms = jnp.where(lm0, ms0, ms1)
        o_ref[:, cols] = o * lax.rsqrt(ms + RMS_EPS) * g_ref[:, cols] * gate_ref[:, cols]


def _hgrn(q, k, lf, v, gate, g, st0, B, T, C):
    L = int(math.log2(C))
    sel, bm = _hgrn_consts(C)
    nck = T // C
    npair = HG_HEADS // 2
    tok = pl.BlockSpec((C, npair * LANES), lambda b, c: (b * nck + c, 0))
    stspec = pl.BlockSpec((1, npair, LANES, LANES), lambda b, c: (b, 0, 0, 0))
    full = lambda a: pl.BlockSpec(a.shape, lambda b, c: (0,) * a.ndim)
    return pl.pallas_call(
        functools.partial(_hgrn_body, C, L),
        grid=(B, nck),
        in_specs=[tok, tok, tok, tok, tok, full(g), stspec, full(sel), full(bm)],
        out_specs=[tok, stspec],
        out_shape=[jax.ShapeDtypeStruct((B * T, npair * LANES), F32),
                   jax.ShapeDtypeStruct((B, npair, LANES, LANES), F32)],
        scratch_shapes=[pltpu.VMEM((npair, LANES, LANES), F32)],
        compiler_params=_cparams(("parallel", "arbitrary")),
        name="hgrn2",
    )(q, k, lf, v, gate, g, st0, sel, bm)


def _state_to_pairs(s):
    B = s.shape[0]
    st = jnp.swapaxes(s, -1, -2).reshape(B, 4, 2, HEAD_DIM, HEAD_DIM)
    z = jnp.zeros_like(st[:, :, 0])
    top = jnp.concatenate([st[:, :, 0], z], -1)
    bot = jnp.concatenate([z, st[:, :, 1]], -1)
    return jnp.concatenate([top, bot], -2)


def _pairs_to_state(sp):
    B = sp.shape[0]
    a = sp[:, :, :HEAD_DIM, :HEAD_DIM]
    b = sp[:, :, HEAD_DIM:, HEAD_DIM:]
    return jnp.swapaxes(jnp.stack([a, b], 2).reshape(B, HG_HEADS, HEAD_DIM, HEAD_DIM), -1, -2)


def _stack_heads(ref, rows):
    lm0 = _iota((rows, LANES), 1) < HEAD_DIM
    parts = []
    for j in range(NSA_GROUP):
        cj = ref[:, LANES * j:LANES * (j + 1)]
        parts += [jnp.where(lm0, cj, 0.0), jnp.where(lm0, 0.0, cj)]
    return jnp.concatenate(parts, 0)


def _block_means(cmp_ref, pe_ref, kc_scr, vc_scr, nc):
    h = nc // 2
    ck = cmp_ref[:, 0:LANES].reshape(h, 2 * CMP_BLOCK, LANES)
    pe = pe_ref[...][None]
    kc_scr[0:h] = jnp.mean(ck[:, 0:CMP_BLOCK] + pe, axis=1)
    kc_scr[h:nc] = jnp.mean(ck[:, CMP_BLOCK:2 * CMP_BLOCK] + pe, axis=1)
    cv = cmp_ref[:, LANES:2 * LANES].reshape(h, 2 * CMP_BLOCK, LANES)
    vc_scr[0:h] = jnp.mean(cv[:, 0:CMP_BLOCK], axis=1)
    vc_scr[h:nc] = jnp.mean(cv[:, CMP_BLOCK:2 * CMP_BLOCK], axis=1)


def _cmp_and_select(kc, qn_st, q0, nc, nsel_rows, nsel):
    h = nc // 2
    ncol = 2 * NSA_GROUP * LANES
    s = _dot_nt(kc.astype(BF), qn_st) * (HEAD_DIM ** -0.5)
    r = _iota((nc, ncol), 0)
    cidx = jnp.where(r < h, 2 * r, 2 * (r - h) + 1)
    qpos = q0 + (_iota((nc, ncol), 1) & (LANES - 1))
    avail = (cidx + 1) * CMP_BLOCK - 1 <= qpos
    _, l, e = _masked_softmax(s, avail, 0)
    p = e / jnp.maximum(l, TINY)
    pp = p[0:h] + p[h:nc]
    w2 = 2 * LANES
    imp = pp[:, 0:w2] + pp[:, w2:2 * w2] + pp[:, 2 * w2:3 * w2] + pp[:, 3 * w2:4 * w2]
    if nsel_rows > h:
        imp = jnp.concatenate([imp, jnp.zeros((nsel_rows - h, w2), F32)], 0)
    blk = _iota((nsel_rows, w2), 0)
    qp = q0 + (_iota((nsel_rows, w2), 1) & (LANES - 1))
    forced = (blk == qp // SEL_BLOCK) | (blk == 0)
    imp = jnp.where(forced, SEL_FORCE, jnp.where(blk * SEL_BLOCK <= qp, imp, -1.0))
    imp = jnp.where(blk < nsel, imp, -2.0)
    rank = jnp.zeros((nsel_rows, w2), F32)
    for i in range(nsel):
        ri = imp[i:i + 1, :]
        beats = (ri > imp) | ((ri == imp) & (blk > i))
        rank = rank + jnp.where(beats, 1.0, 0.0)
    sel = jnp.where(rank < float(min(SEL_TOPN, nsel)), 1.0, 0.0)
    return p, sel


def _attend_rows(q_st, k, v, mask):
    s = _dot_nt(q_st, k) * (HEAD_DIM ** -0.5)
    _, l, e = _masked_softmax(s, mask, 1)
    return _dot(e.astype(BF), v) / jnp.maximum(l, TINY)


def _attend(q_st, k, v, mask, rows):
    o = _attend_rows(q_st, k, v, mask)
    lm0 = _iota((rows, LANES), 1) < HEAD_DIM
    return jnp.where(lm0, o[0:rows], o[rows:2 * rows])


def _nsa_prompt_body(T, qn_ref, qr_ref, cmp_ref, slc_ref, win_ref, gc_ref, gs_ref, gw_ref, pe_ref,
                     e_ref, o_ref, kc_scr, vc_scr):
    qb = pl.program_id(1)
    nc = T // CMP_BLOCK
    nsel = T // SEL_BLOCK
    R = LANES

    @pl.when(qb == 0)
    def _():
        _block_means(cmp_ref, pe_ref, kc_scr, vc_scr, nc)

    q0 = qb * R
    qn_st = _stack_heads(qn_ref, R).astype(BF)
    qr_st = _stack_heads(qr_ref, R).astype(BF)
    p, sel = _cmp_and_select(kc_scr[...], qn_st, q0, nc, nsel, nsel)
    oc = _dot(p.T.astype(BF), vc_scr[...].astype(BF))

    sel_e = _dot_tn(sel.astype(BF), e_ref[...])
    qpos = q0 + (_iota((2 * R, T), 0) & (R - 1))
    mask_s = (sel_e > 0.5) & (_iota((2 * R, T), 1) <= qpos)
    ks = slc_ref[:, 0:LANES].astype(BF)
    vs = slc_ref[:, LANES:2 * LANES].astype(BF)

    nw = NSA_WINDOW + R
    start = pl.multiple_of(jnp.maximum(qb - NSA_WINDOW // R, 0) * R, R)
    kw = win_ref[pl.ds(start, nw), 0:LANES].astype(BF)
    vw = win_ref[pl.ds(start, nw), LANES:2 * LANES].astype(BF)
    dist = q0 + (_iota((2 * R, nw), 0) & (R - 1)) - (start + _iota((2 * R, nw), 1))
    mask_w = (dist >= 0) & (dist <= NSA_WINDOW)

    lm0 = _iota((R, LANES), 1) < HEAD_DIM
    for j in range(NSA_GROUP):
        cols = slice(LANES * j, LANES * (j + 1))
        qj = qr_st[2 * R * j:2 * R * (j + 1)]
        o_s = _attend(qj, ks, vs, mask_s, R)
        o_w = _attend(qj, kw, vw, mask_w, R)
        o_c = jnp.where(lm0, oc[2 * R * j:2 * R * j + R], oc[2 * R * j + R:2 * R * (j + 1)])
        o_ref[:, cols] = gc_ref[:, cols] * o_c + gs_ref[:, cols] * o_s + gw_ref[:, cols] * o_w


def _sel_expand(nrows, nkeys):
    e = (np.arange(nkeys)[None, :] // SEL_BLOCK) == np.arange(nrows)[:, None]
    return jnp.asarray(e.astype(np.float32), BF)


def _nsa_prompt(qn, qr, cmp, slc, win, gc, gs, gw, pe, B, T):
    R = LANES
    nq = T // R
    tok = pl.BlockSpec((R, 512), lambda b, i: (b * nq + i, 0))
    seq = pl.BlockSpec((T, 256), lambda b, i: (b, 0))
    e = _sel_expand(T // SEL_BLOCK, T)
    full = lambda a: pl.BlockSpec(a.shape, lambda b, i: (0,) * a.ndim)
    return pl.pallas_call(
        functools.partial(_nsa_prompt_body, T),
        grid=(B, nq),
        in_specs=[tok, tok, seq, seq, seq, tok, tok, tok, full(pe), full(e)],
        out_specs=tok,
        out_shape=jax.ShapeDtypeStruct((B * T, 512), F32),
        scratch_shapes=[pltpu.VMEM((T // CMP_BLOCK, LANES), F32), pltpu.VMEM((T // CMP_BLOCK, LANES), F32)],
        compiler_params=_cparams(("parallel", "arbitrary")),
        name="nsa_prompt",
    )(qn, qr, cmp, slc, win, gc, gs, gw, pe, e)


def _nsa_sample_body(P, npages, *refs):
    (qn_ref, qr_ref, slcn_ref, winn_ref, winb_ref, gc_ref, gs_ref, gw_ref, pe_ref, e_ref) = refs[1:11]
    cmp_pages = refs[11:11 + npages]
    slc_pages = refs[11 + npages:11 + 2 * npages]
    o_ref = refs[11 + 2 * npages]
    cmp_all, slc_all, win_all, kc_scr, vc_scr = refs[12 + 2 * npages:]
    S = SAMPLE_ROWS
    nc = P // CMP_BLOCK
    nsel = -(-(P + 4) // SEL_BLOCK)
    nsel_rows = e_ref.shape[0]
    nk = slc_all.shape[0]
    lw = winb_ref.shape[1]
    nkw = win_all.shape[0]
    for pg in range(npages):
        cmp_all[PAGE_SIZE * pg:PAGE_SIZE * (pg + 1)] = cmp_pages[pg][0]
        slc_all[PAGE_SIZE * pg:PAGE_SIZE * (pg + 1)] = slc_pages[pg][0]
    slc_all[P:P + S] = slcn_ref[...]
    slc_all[P + S:nk] = jnp.zeros((nk - P - S, 256), F32)
    win_all[0:lw] = winb_ref[0]
    win_all[lw:lw + S] = winn_ref[...]
    win_all[lw + S:nkw] = jnp.zeros((nkw - lw - S, 256), F32)
    _block_means(cmp_all, pe_ref, kc_scr, vc_scr, nc)

    lm0s = _iota((S, LANES), 1) < HEAD_DIM
    zpad = jnp.zeros((LANES - S, LANES), F32)
    qn_parts, qr_parts = [], []
    for j in range(NSA_GROUP):
        cn = qn_ref[:, LANES * j:LANES * (j + 1)]
        cr = qr_ref[:, LANES * j:LANES * (j + 1)]
        qn_parts += [jnp.where(lm0s, cn, 0.0), zpad, jnp.where(lm0s, 0.0, cn), zpad]
        qr_parts += [jnp.where(lm0s, cr, 0.0), jnp.where(lm0s, 0.0, cr)]
    qn_st = jnp.concatenate(qn_parts, 0).astype(BF)
    qr_st = jnp.concatenate(qr_parts, 0).astype(BF)
    p, sel = _cmp_and_select(kc_scr[...], qn_st, P, nc, nsel_rows, nsel)
    pt = p.T
    pc = jnp.concatenate([pt[LANES * i:LANES * i + S] for i in range(2 * NSA_GROUP)], 0)
    o_c = _dot(pc.astype(BF), vc_scr[...].astype(BF))

    R = 2 * NSA_GROUP * S
    sel_e = _dot_tn(sel.astype(BF), e_ref[...])
    sel_c = jnp.concatenate([sel_e[0:S], sel_e[LANES:LANES + S]] * NSA_GROUP, 0)
    qpos = P + (_iota((R, nk), 0) & (S - 1))
    mask_s = (sel_c > 0.5) & (_iota((R, nk), 1) <= qpos)
    o_s = _attend_rows(qr_st, slc_all[:, 0:LANES].astype(BF), slc_all[:, LANES:2 * LANES].astype(BF), mask_s)
    dist = (_iota((R, nkw), 0) & (S - 1)) + lw - _iota((R, nkw), 1)
    mask_w = (dist >= 0) & (dist <= NSA_WINDOW)
    o_w = _attend_rows(qr_st, win_all[:, 0:LANES].astype(BF), win_all[:, LANES:2 * LANES].astype(BF), mask_w)
    for j in range(NSA_GROUP):
        cols = slice(LANES * j, LANES * (j + 1))
        r0 = slice(2 * S * j, 2 * S * j + S)
        r1 = slice(2 * S * j + S, 2 * S * (j + 1))
        pick = lambda a: jnp.where(lm0s, a[r0], a[r1])
        o_ref[:, cols] = gc_ref[:, cols] * pick(o_c) + gs_ref[:, cols] * pick(o_s) + gw_ref[:, cols] * pick(o_w)


def _nsa_sample(page_table, qn, qr, slc_new, win_new, win_buf, gc, gs, gw, pe, cmp_pool, slc_pool, B, P):
    S = SAMPLE_ROWS
    npages = P // PAGE_SIZE
    nk = P + LANES
    nsel_rows = 8 * (-(-(-(-(P + 4) // SEL_BLOCK)) // 8))
    e = _sel_expand(nsel_rows, nk)
    tok = lambda w_: pl.BlockSpec((S, w_), lambda b, pt: (b, 0))
    full = lambda a: pl.BlockSpec(a.shape, lambda b, pt: (0,) * a.ndim)
    page = lambda pg: pl.BlockSpec((1, PAGE_SIZE, 256), lambda b, pt: (pt[b * npages + pg], 0, 0))
    in_specs = ([tok(512), tok(512), tok(256), tok(256),
                 pl.BlockSpec((1, win_buf.shape[1], 256), lambda b, pt: (b, 0, 0)),
                 tok(512), tok(512), tok(512), full(pe), full(e)]
                + [page(pg) for pg in range(npages)] * 2)
    gs_ = pltpu.PrefetchScalarGridSpec(
        num_scalar_prefetch=1, grid=(B,), in_specs=in_specs, out_specs=tok(512),
        scratch_shapes=[pltpu.VMEM((P, 256), F32), pltpu.VMEM((nk, 256), F32),
                        pltpu.VMEM((win_buf.shape[1] + LANES, 256), F32),
                        pltpu.VMEM((P // CMP_BLOCK, LANES), F32), pltpu.VMEM((P // CMP_BLOCK, LANES), F32)])
    return pl.pallas_call(
        functools.partial(_nsa_sample_body, P, npages),
        grid_spec=gs_,
        out_shape=jax.ShapeDtypeStruct((B * S, 512), F32),
        compiler_params=_cparams(("arbitrary",)),
        name="nsa_sample",
    )(page_table.reshape(-1), qn, qr, slc_new, win_new, win_buf, gc, gs, gw, pe, e,
      *([cmp_pool] * npages), *([slc_pool] * npages))


def _outproj_ln_body(a_ref, b_ref, x_ref, wa_ref, wb_ref, g_ref, bb_ref, y_ref):
    mix = _dot(a_ref[...].astype(BF), wa_ref[...]) + _dot(b_ref[...].astype(BF), wb_ref[...])
    y_ref[...] = _layer_norm(ALPHA * x_ref[...] + mix, g_ref[...], bb_ref[...])


def _outproj_ln(a, acol, b, bcol, x, wa, wb, g, bb, tm):
    n = x.shape[0]
    full = lambda t: pl.BlockSpec(t.shape, lambda i: (0,) * t.ndim)
    return pl.pallas_call(
        _outproj_ln_body,
        grid=(n // tm,),
        in_specs=[pl.BlockSpec((tm, 512), lambda i: (i, acol)), pl.BlockSpec((tm, 512), lambda i: (i, bcol)),
                  pl.BlockSpec((tm, D_MODEL), lambda i: (i, 0)), full(wa), full(wb), full(g), full(bb)],
        out_specs=pl.BlockSpec((tm, D_MODEL), lambda i: (i, 0)),
        out_shape=jax.ShapeDtypeStruct((n, D_MODEL), F32),
        compiler_params=_cparams(("parallel",)),
        name="outproj_ln",
    )(a, b, x, wa, wb, g, bb)


def _proj_odd_body(x_ref, w_ref, cos_ref, sin_ref, q_o, kv_o):
    x = x_ref[...].astype(BF)
    c = cos_ref[...]
    s = sin_ref[...]
    nchunk = DIL_HEADS * HEAD_DIM // LANES
    for j in range(nchunk):
        cols = slice(LANES * j, LANES * (j + 1))
        q_o[:, cols] = _rope_chunk(_dot(x, w_ref[:, cols]), c, s)
        kv_o[:, cols] = _rope_chunk(_dot(x, w_ref[:, D_MODEL + LANES * j:D_MODEL + LANES * (j + 1)]), c, s)
    kv_o[:, D_MODEL:2 * D_MODEL] = _dot(x, w_ref[:, 2 * D_MODEL:3 * D_MODEL])


def _proj_odd(x, w, cos, sin, tm):
    n = x.shape[0]
    row = lambda w_: pl.BlockSpec((tm, w_), lambda i: (i, 0))
    return pl.pallas_call(
        _proj_odd_body,
        grid=(n // tm,),
        in_specs=[row(D_MODEL), pl.BlockSpec(w.shape, lambda i: (0, 0)), row(LANES), row(LANES)],
        out_specs=[row(D_MODEL), row(2 * D_MODEL)],
        out_shape=[jax.ShapeDtypeStruct((n, D_MODEL), F32), jax.ShapeDtypeStruct((n, 2 * D_MODEL), F32)],
        compiler_params=_cparams(("parallel",)),
        name="proj_odd",
    )(x, w, cos, sin)


def _dil_prompt_body(T, q_ref, k_ref, v_ref, o_ref, acc_scr, m_scr, l_scr):
    R = LANES
    lm0 = _iota((R, LANES), 1) < HEAD_DIM
    for ci, (window, d) in enumerate(DIL_CONFIGS):
        band = window // d
        nblk = T // d // R
        for r in range(d):
            for i in range(nblk):
                q0 = r + d * R * i
                rows_q = pl.ds(q0, R, stride=d) if d > 1 else pl.ds(q0, R)
                if i > 0:
                    k0, nk = q0 - d * R, 2 * R
                else:
                    k0, nk = q0, R
                rows_k = pl.ds(k0, nk, stride=d) if d > 1 else pl.ds(k0, nk)
                qs = q_ref[rows_q, :]
                q_st = jnp.concatenate([jnp.where(lm0, qs, 0.0), jnp.where(lm0, 0.0, qs)], 0).astype(BF)
                ks = k_ref[rows_k, :].astype(BF)
                vs = v_ref[rows_k, :].astype(BF)
                dist = (_iota((2 * R, nk), 0) & (R - 1)) + (nk - R) - _iota((2 * R, nk), 1)
                s = _dot_nt(q_st, ks) * (HEAD_DIM ** -0.5)
                m, l, e = _masked_softmax(s, (dist >= 0) & (dist <= band), 1)
                acc = _dot(e.astype(BF), vs)
                acc_scr[ci, rows_q, :] = jnp.where(lm0, acc[0:R], acc[R:2 * R])
                m_scr[ci, rows_q, :] = jnp.where(lm0, m[0:R], m[R:2 * R])
                l_scr[ci, rows_q, :] = jnp.where(lm0, l[0:R], l[R:2 * R])
    ncfg = len(DIL_CONFIGS)
    mx = m_scr[0]
    for ci in range(1, ncfg):
        mx = jnp.maximum(mx, m_scr[ci])
    num = jnp.zeros((T, LANES), F32)
    den = jnp.zeros((T, LANES), F32)
    for ci in range(ncfg):
        w = jnp.exp(m_scr[ci] - mx)
        num = num + w * acc_scr[ci]
        den = den + w * l_scr[ci]
    o_ref[...] = num / den


def _dil_prompt(q, kv, B, T):
    npair = DIL_HEADS // 2
    ncfg = len(DIL_CONFIGS)
    return pl.pallas_call(
        functools.partial(_dil_prompt_body, T),
        grid=(B, npair),
        in_specs=[pl.BlockSpec((T, LANES), lambda b, p: (b, p)),
                  pl.BlockSpec((T, LANES), lambda b, p: (b, p)),
                  pl.BlockSpec((T, LANES), lambda b, p: (b, npair + p))],
        out_specs=pl.BlockSpec((T, LANES), lambda b, p: (b, p)),
        out_shape=jax.ShapeDtypeStruct((B * T, D_MODEL), F32),
        scratch_shapes=[pltpu.VMEM((ncfg, T, LANES), F32)] * 3,
        compiler_params=_cparams(("parallel", "parallel")),
        name="dil_prompt",
    )(q, kv, kv)


def _dil_sample_body(L, q_ref, kvn_ref, kt_ref, vt_ref, o_ref):
    S = SAMPLE_ROWS
    H = DIL_HEADS
    R = H * S
    scale = HEAD_DIM ** -0.5
    pad16 = lambda a: jnp.concatenate([a, jnp.zeros(a.shape, F32)], 0).astype(BF)
    sb, sn = [], []
    for h in range(H):
        q16 = pad16(q_ref[h])
        sb.append(_dot(q16, kt_ref[0, 0, h].astype(BF))[0:S])
        sn.append(_dot_nt(q16, pad16(kvn_ref[h]))[0:S])
    parts = [
        (jnp.concatenate(sb, 0) * scale, L + (_iota((R, L), 0) & (S - 1)) - _iota((R, L), 1), False),
        (jnp.concatenate(sn, 0) * scale, (_iota((R, 2 * S), 0) & (S - 1)) - _iota((R, 2 * S), 1), True),
    ]
    stats = []
    for window, d in DIL_CONFIGS:
        masked = []
        for s, dist, signed in parts:
            ok = (dist <= window) & ((dist & (d - 1)) == 0)
            if signed:
                ok = ok & (dist >= 0)
            masked.append(jnp.where(ok, s, -jnp.inf))
        m = functools.reduce(jnp.maximum, [jnp.max(x, axis=1, keepdims=True) for x in masked])
        m = jnp.where(m > -jnp.inf, m, 0.0)
        es = [jnp.exp(x - m) for x in masked]
        l = functools.reduce(lambda a, b: a + b, [jnp.sum(e, axis=1, keepdims=True) for e in es])
        stats.append((m, l, es))
    mx = functools.reduce(jnp.maximum, [st[0] for st in stats])
    ws = [jnp.exp(m - mx) for m, _, _ in stats]
    inv = 1.0 / functools.reduce(lambda a, b: a + b, [w * l for w, (_, l, _) in zip(ws, stats)])
    pb, pn = [functools.reduce(lambda a, b: a + b, [w * es[pi] for w, (_, _, es) in zip(ws, stats)]) * inv
              for pi in range(2)]
    for h in range(H):
        rows = slice(S * h, S * (h + 1))
        o = _dot_nt(pad16(pb[rows]), vt_ref[0, 0, h].astype(BF)) + _dot(pad16(pn[rows]), pad16(kvn_ref[H + h]))
        o_ref[h] = o[0:S]


def _dil_sample(q, kv_new, cache, B):
    S = SAMPLE_ROWS
    L, H, Dh = cache.shape[1], cache.shape[3], cache.shape[4]
    n = B * S
    cache_t = jnp.transpose(cache, (0, 2, 3, 4, 1))
    q_h = q.reshape(n, H, Dh).transpose(1, 0, 2)
    kv_h = kv_new.reshape(n, 2 * H, Dh).transpose(1, 0, 2)
    part = lambda j: pl.BlockSpec((1, 1, H, Dh, L), lambda b: (b, j, 0, 0, 0))
    o = pl.pallas_call(
        functools.partial(_dil_sample_body, L),
        grid=(B,),
        in_specs=[pl.BlockSpec((H, S, Dh), lambda b: (0, b, 0)),
                  pl.BlockSpec((2 * H, S, Dh), lambda b: (0, b, 0)),
                  part(0), part(1)],
        out_specs=pl.BlockSpec((H, S, Dh), lambda b: (0, b, 0)),
        out_shape=jax.ShapeDtypeStruct((H, n, Dh), F32),
        compiler_params=_cparams(("parallel",)),
        name="dil_sample",
    )(q_h, kv_h, cache_t, cache_t)
    return o.transpose(1, 0, 2).reshape(n, H * Dh)


def _pop_max(work, iota_k, nrow):
    m = jnp.max(work, axis=0, keepdims=True)
    idx = jnp.min(jnp.where(work == m, iota_k, float(nrow)), axis=0, keepdims=True)
    return m, idx, jnp.where(iota_k == idx, -jnp.inf, work)


def _peer_route(s1, s2, tn, exact_ties):
    K = PEER_TOPK
    NK = PEER_NKEYS
    iota_k = _iota((NK, tn), 0).astype(F32)
    iota_r = _iota((K, tn), 0)

    w1_, w2_ = s1, s2
    v1, i1, i2 = [], [], []
    v2 = jnp.zeros((K, tn), F32)
    for it in range(K):
        if exact_ties:
            m1, idx1, w1_ = _pop_max(w1_, iota_k, NK)
            m2, idx2, w2_ = _pop_max(w2_, iota_k, NK)
            i1.append(idx1)
            i2.append(idx2)
        else:
            m1 = jnp.max(w1_, axis=0, keepdims=True)
            w1_ = jnp.where(w1_ == m1, -jnp.inf, w1_)
            m2 = jnp.max(w2_, axis=0, keepdims=True)
            w2_ = jnp.where(w2_ == m2, -jnp.inf, w2_)
        v1.append(m1)
        v2 = jnp.where(iota_r == it, m2, v2)
    if exact_ties:
        tied = jnp.zeros((1, tn), F32)
    else:
        gone = lambda w: jnp.sum(jnp.where(w == -jnp.inf, 1.0, 0.0), axis=0, keepdims=True)
        tied = jnp.where((gone(w1_) != float(K)) | (gone(w2_) != float(K)), 1.0, 0.0)

    rows = [K] + [8] * (K - 1)
    pieces = []
    for r1 in range(K):
        piece = v1[r1] + v2[0:rows[r1]]
        nvalid = K // (r1 + 1)
        if nvalid < rows[r1]:
            piece = jnp.where(_iota((rows[r1], tn), 0) < nvalid, piece, -jnp.inf)
        pieces.append(piece)
    cand = jnp.concatenate(pieces, 0)
    nc = cand.shape[0]
    iota_c = _iota((nc, tn), 0).astype(F32)
    work = cand
    for it in range(K):
        _, _, work = _pop_max(work, iota_c, nc)
    sel = (work == -jnp.inf) & (cand > -jnp.inf)
    z = jnp.sum(jnp.where(sel, jnp.exp(cand - cand[0:1]), 0.0), axis=0, keepdims=True)
    self_ = jnp.where(sel, 1.0, 0.0)

    r2 = jnp.full((NK, tn), float(K), F32) if exact_ties else jnp.zeros((NK, tn), F32)
    cntk = jnp.zeros((NK, tn), F32)
    off = 0
    for r in range(K):
        cnt_r = jnp.sum(self_[off:off + rows[r]], axis=0, keepdims=True)
        off += rows[r]
        if exact_ties:
            cntk = jnp.where(iota_k == i1[r], cnt_r, cntk)
            r2 = jnp.where(iota_k == i2[r], float(r), r2)
        else:
            cntk = jnp.where(s1 == v1[r], cnt_r, cntk)
            r2 = r2 + jnp.where(v2[r:r + 1] > s2, 1.0, 0.0)
    return r2.astype(BF), jnp.exp(s2 - v2[0:1]).astype(BF), cntk, jnp.exp(s1 - v1[0]) / z, tied


def _peer_topk_body(y_ref, wq_ref, keys_ref, r2_o, e2_o, cnt_o, w1_o):
    tn = y_ref.shape[0]
    q = _dot(y_ref[...].astype(BF), wq_ref[...]).astype(BF)
    s1 = _dot_nt(keys_ref[0, 0], q[:, 0:LANES])
    s2 = _dot_nt(keys_ref[0, 1], q[:, LANES:2 * LANES])

    def emit(exact_ties):
        r2, e2, cnt, w1, tied = _peer_route(s1, s2, tn, exact_ties)
        r2_o[0] = r2
        e2_o[0] = e2
        cnt_o[0] = cnt
        w1_o[0] = w1
        return tied

    tied = emit(False)

    @pl.when(jnp.max(tied) > 0.0)
    def _():
        emit(True)


def _peer_topk(y, wq, keys, tn):
    n = y.shape[0]
    out = pl.BlockSpec((1, PEER_NKEYS, tn), lambda i, h: (h, 0, i))
    shp = lambda dt: jax.ShapeDtypeStruct((PEER_HEADS, PEER_NKEYS, n), dt)
    return pl.pallas_call(
        _peer_topk_body,
        grid=(n // tn, PEER_HEADS),
        in_specs=[pl.BlockSpec((tn, D_MODEL), lambda i, h: (i, 0)),
                  pl.BlockSpec((D_MODEL, 2 * LANES), lambda i, h: (0, h)),
                  pl.BlockSpec((1, 2, PEER_NKEYS, LANES), lambda i, h: (h, 0, 0, 0))],
        out_specs=[out] * 4,
        out_shape=[shp(BF), shp(BF), shp(F32), shp(F32)],
        compiler_params=_cparams(("parallel", "arbitrary")),
        name="peer_topk",
    )(y, wq, keys)


def _gelu(x):
    return 0.5 * x * (1.0 + lax.erf(x * math.sqrt(0.5)))


def _peer_main_body(npe, xt_ref, u_ref, vt_ref, r2_ref, e2_ref, cnt_ref, w1_ref, yt_ref):
    e = pl.program_id(1)
    tn = xt_ref.shape[1]

    @pl.when(e == 0)
    def _():
        yt_ref[...] = jnp.zeros(yt_ref.shape, F32)

    xt = xt_ref[...]
    sub = 2 * PEER_NKEYS
    acc = yt_ref[...]
    nsub = npe // 2
    act = lambda s: _gelu(_dot(u_ref[sub * s:sub * (s + 1), :], xt).astype(BF))
    ahead = 2
    acts = [act(s) for s in range(min(ahead, nsub))]
    for s in range(nsub):
        if s + ahead < nsub:
            acts.append(act(s + ahead))
        a = acts[s]
        parts = []
        for cc in range(2):
            c = e * npe + 2 * s + cc
            g = jnp.zeros((PEER_NKEYS, tn), BF)
            for h in range(PEER_HEADS):
                cnt_row = cnt_ref[h, pl.ds(c, 1), :].astype(BF)
                w_row = w1_ref[h, pl.ds(c, 1), :].astype(BF)
                g = g + jnp.where(r2_ref[h] < cnt_row, e2_ref[h] * w_row, jnp.zeros((), BF))
            parts.append(g * a[PEER_NKEYS * cc:PEER_NKEYS * (cc + 1)].astype(BF))
        acc = acc + _dot(vt_ref[:, sub * s:sub * (s + 1)], jnp.concatenate(parts, 0))
    yt_ref[...] = acc


def _peer_main(xt, u, vt, r2, e2, cnt, w1, tn, te):
    n = xt.shape[1]
    ne = u.shape[0]
    npe = te // PEER_NKEYS
    tab = pl.BlockSpec((PEER_HEADS, PEER_NKEYS, tn), lambda i, e: (0, 0, i))
    return pl.pallas_call(
        functools.partial(_peer_main_body, npe),
        grid=(n // tn, ne // te),
        in_specs=[pl.BlockSpec((D_MODEL, tn), lambda i, e: (0, i)),
                  pl.BlockSpec((te, D_MODEL), lambda i, e: (e, 0)),
                  pl.BlockSpec((D_MODEL, te), lambda i, e: (0, e)),
                  tab, tab, tab, tab],
        out_specs=pl.BlockSpec((D_MODEL, tn), lambda i, e: (0, i)),
        out_shape=jax.ShapeDtypeStruct((D_MODEL, n), F32),
        compiler_params=_cparams(("parallel", "arbitrary")),
        name="peer_main",
    )(xt, u, vt, r2, e2, cnt, w1)


def _ln_t_body(x_ref, ft_ref, g_ref, b_ref, y_ref):
    y_ref[...] = _layer_norm(ALPHA * x_ref[...] + ft_ref[...].T, g_ref[...], b_ref[...])


def _ln_t(x, ft, g, b, tn):
    n = x.shape[0]
    full = lambda t: pl.BlockSpec(t.shape, lambda i: (0,) * t.ndim)
    return pl.pallas_call(
        _ln_t_body,
        grid=(n // tn,),
        in_specs=[pl.BlockSpec((tn, D_MODEL), lambda i: (i, 0)), pl.BlockSpec((D_MODEL, tn), lambda i: (0, i)),
                  full(g), full(b)],
        out_specs=pl.BlockSpec((tn, D_MODEL), lambda i: (i, 0)),
        out_shape=jax.ShapeDtypeStruct((n, D_MODEL), F32),
        compiler_params=_cparams(("parallel",)),
        name="ln_residual",
    )(x, ft, g, b)


def _peer_layer(y, wq, keys, u, v, g, b):
    r2, e2, cnt, w1 = _peer_topk(y, wq.astype(BF), keys.astype(BF), LANES)
    ft = _peer_main(y.T.astype(BF), u.astype(BF), v.T.astype(BF), r2, e2, cnt, w1, 512, 2048)
    return _ln_t(y, ft, g, b, 256)


def _pad_rows(a, S):
    return jnp.pad(a, ((0, 0), (0, S - a.shape[1])) + ((0, 0),) * (a.ndim - 2))


def kernel(x_prompt, x_sample, state_hgrn, cache_cmp_kv, cache_slc_kv, cache_win_kv, cache_dil_kv, page_table,
           hg_gamma, even_w_in, even_w_out, hg_norm_g, nsa_cmp_pe, odd_w_in, odd_w_out, ln_mix_g, ln_mix_b,
           peer_w_q, peer_sub_keys, peer_u, peer_v, ln_ffn_g, ln_ffn_b):
    B, T, D = x_prompt.shape
    Bs, Ts, _ = x_sample.shape
    S = SAMPLE_ROWS
    P = page_table.shape[1] * PAGE_SIZE
    npr = B * T
    yp = x_prompt.reshape(npr, D)
    ys = _pad_rows(x_sample, S).reshape(Bs * S, D)
    cos_p, sin_p = _rope_tables(jnp.tile(jnp.arange(T), B))
    cos_s, sin_s = _rope_tables(jnp.tile(P + jnp.arange(S), Bs))
    live = (jnp.arange(Bs * S) % S < Ts)[:, None]
    hperm = np.array([[j, NSA_GROUP + j] for j in range(NSA_GROUP)]).reshape(-1)
    outs = {}

    for layer in range(DEPTH):
        row2 = lambda a: a[layer].reshape(1, D)
        if layer % 2 == 0:
            e = layer // 2
            w = _even_weight(even_w_in[e])
            wo = even_w_out[e]
            wa = wo[:512].astype(BF)
            wb = wo[512:].reshape(NSA_HEADS, HEAD_DIM, D)[hperm].reshape(512, D).astype(BF)
            g = hg_norm_g[e].reshape(1, 512)
            pe = nsa_cmp_pe[e].reshape(CMP_BLOCK, LANES)
            hq, hk, hlf, hv, hg, qn, qr, cmp, slc, win, gc, gs, gw = _proj_even(yp, w, hg_gamma, cos_p, sin_p, layer, 256)
            st0 = jnp.zeros((B, HG_HEADS // 2, LANES, LANES), F32)
            o_hg, st = _hgrn(hq, hk, hlf, hv, hg, g, st0, B, T, LANES)
            nsa = _nsa_prompt(qn, qr, cmp, slc, win, gc, gs, gw, pe, B, T)
            mp = (o_hg, nsa)
            kv5 = lambda a, b_, t_: a.reshape(b_, t_, 2, NSA_KV_HEADS, HEAD_DIM)
            outs.setdefault("hg_p", []).append(_pairs_to_state(st))
            outs.setdefault("cmp_p", []).append(kv5(cmp, B, T))
            outs.setdefault("slc_p", []).append(kv5(slc, B, T))
            outs.setdefault("win_p", []).append(kv5(win, B, T)[:, -min(NSA_WINDOW, T):])
            hq, hk, hlf, hv, hg, qn, qr, cmp, slc, win, gc, gs, gw = _proj_even(ys, w, hg_gamma, cos_s, sin_s, layer, 256)
            hk = jnp.where(live, hk, 0.0)
            hlf = jnp.where(live, hlf, 0.0)
            C = 2 * S
            pad = lambda a: _pad_rows(a.reshape(Bs, S, 512), C).reshape(Bs * C, 512)
            o_hg, st = _hgrn(pad(hq), pad(hk), pad(hlf), pad(hv), pad(hg), g,
                             _state_to_pairs(state_hgrn[e].astype(F32)), Bs, C, C)
            o_hg = o_hg.reshape(Bs, C, 512)[:, :S].reshape(Bs * S, 512)
            npool = cache_cmp_kv.shape[1]
            nsa = _nsa_sample(page_table, qn, qr, slc, win, cache_win_kv[e].reshape(Bs, -1, 256), gc, gs, gw, pe,
                              cache_cmp_kv[e].reshape(npool, PAGE_SIZE, 256),
                              cache_slc_kv[e].reshape(npool, PAGE_SIZE, 256), Bs, P)
            ms = (o_hg, nsa)
            outs.setdefault("hg_s", []).append(_pairs_to_state(st))
            outs.setdefault("cmp_s", []).append(kv5(cmp, Bs, S)[:, :Ts])
            outs.setdefault("slc_s", []).append(kv5(slc, Bs, S)[:, :Ts])
            outs.setdefault("win_s", []).append(kv5(win, Bs, S)[:, :Ts])
            acol, bcol = 0, 0
        else:
            o = layer // 2
            w = odd_w_in[o].astype(BF)
            wo = odd_w_out[o]
            wa = wo[:512].astype(BF)
            wb = wo[512:].astype(BF)
            q, kv = _proj_odd(yp, w, cos_p, sin_p, 256)
            att = _dil_prompt(q, kv, B, T)
            mp = (att, att)
            kv6 = lambda a, b_, t_: a.reshape(b_, t_, 2, DIL_HEADS, HEAD_DIM)
            outs.setdefault("dil_p", []).append(kv6(kv, B, T)[:, -min(DIL_CONFIGS[-1][0], T):])
            q, kv = _proj_odd(ys, w, cos_s, sin_s, 256)
            att = _dil_sample(q, kv, cache_dil_kv[o], Bs)
            ms = (att, att)
            outs.setdefault("dil_s", []).append(kv6(kv, Bs, S)[:, :Ts])
            acol, bcol = 0, 1
        lg, lbias = row2(ln_mix_g), row2(ln_mix_b)
        yp = _outproj_ln(mp[0], acol, mp[1], bcol, yp, wa, wb, lg, lbias, 256)
        ys = _outproj_ln(ms[0], acol, ms[1], bcol, ys, wa, wb, lg, lbias, 256)
        y = jnp.concatenate([yp, ys], 0)
        y = _peer_layer(y, peer_w_q[layer], peer_sub_keys[layer], peer_u[layer], peer_v[layer],
                        row2(ln_ffn_g), row2(ln_ffn_b))
        yp, ys = y[:npr], y[npr:]

    stack = lambda k_: jnp.stack(outs[k_])
    return (yp.reshape(B, T, D), ys.reshape(Bs, S, D)[:, :Ts], stack("hg_p"), stack("hg_s"),
            stack("cmp_p"), stack("cmp_s"), stack("slc_p"), stack("slc_s"), stack("win_p"), stack("win_s"),
            stack("dil_p"), stack("dil_s"))
```

```python
import functools
import math

import numpy as np
import jax
import jax.numpy as jnp
from jax import lax
from jax.experimental import pallas as pl
from jax.experimental.pallas import tpu as pltpu

F32 = jnp.float32
BF = jnp.bfloat16

D_MODEL = 1024
HEAD_DIM = 64
LANES = 128
ROPE_THETA = 10000.0
LN_EPS = 1e-5
RMS_EPS = 1e-6
TINY = 1e-30
DEPTH = 2
ALPHA = (2 * DEPTH) ** 0.25
PAGE_SIZE = 128

HG_HEADS = 8
NSA_HEADS = 8
NSA_KV_HEADS = 2
NSA_GROUP = NSA_HEADS // NSA_KV_HEADS
CMP_BLOCK = 32
SEL_BLOCK = 64
SEL_TOPN = 8
SEL_FORCE = 1e4
NSA_WINDOW = 512
DIL_HEADS = 16
DIL_CONFIGS = ((128, 1), (512, 4), (2048, 16))
PEER_HEADS = 8
PEER_NKEYS = 128
PEER_TOPK = 16
SAMPLE_ROWS = 8

VMEM_LIMIT = 56 * 1024 * 1024


def _cparams(sem):
    return pltpu.CompilerParams(dimension_semantics=sem, vmem_limit_bytes=VMEM_LIMIT)


def _dot(a, b):
    return jnp.dot(a, b, preferred_element_type=F32)


def _dot_nt(a, b):
    return lax.dot_general(a, b, (((1,), (1,)), ((), ())), preferred_element_type=F32)


def _dot_tn(a, b):
    return lax.dot_general(a, b, (((0,), (0,)), ((), ())), preferred_element_type=F32)


def _iota(shape, dim):
    return lax.broadcasted_iota(jnp.int32, shape, dim)


def _masked_softmax(s, mask, axis):
    s = jnp.where(mask, s, -jnp.inf)
    m = jnp.max(s, axis=axis, keepdims=True)
    m = jnp.where(m > -jnp.inf, m, 0.0)
    e = jnp.exp(s - m)
    l = jnp.sum(e, axis=axis, keepdims=True)
    return m, l, e


def _rope_chunk(x, c, s):
    lane = _iota(x.shape, 1)
    sw = jnp.where((lane & 63) < 32, pltpu.roll(x, 96, 1), pltpu.roll(x, 32, 1))
    return x * c + sw * s


def _rope_tables(pos):
    half = HEAD_DIM // 2
    inv = ROPE_THETA ** (-jnp.arange(half, dtype=F32) / half)
    ang = pos.astype(F32)[:, None] * inv[None, :]
    cos, sin = jnp.cos(ang), jnp.sin(ang)
    return jnp.tile(cos, (1, 4)), jnp.tile(jnp.concatenate([-sin, sin], 1), (1, 2))


def _layer_norm(z, g, b):
    mu = jnp.mean(z, -1, keepdims=True)
    zc = z - mu
    var = jnp.mean(zc * zc, -1, keepdims=True)
    return zc * lax.rsqrt(var + LN_EPS) * g + b


EVEN_COLS = 2048 + 512 + 768 + 1536


def _even_weight(w_in):
    hperm = np.array([[j, NSA_GROUP + j] for j in range(NSA_GROUP)]).reshape(-1)
    qcols = (2048 + hperm[:, None] * HEAD_DIM + np.arange(HEAD_DIM)[None, :]).reshape(-1)
    gate0 = 2048 + 512 + 768
    gcols = []
    for c in range(3):
        for h in hperm:
            kvh, g = divmod(int(h), NSA_GROUP)
            gcols.append(np.full(HEAD_DIM, gate0 + c * NSA_HEADS + kvh * NSA_GROUP + g))
    cols = np.concatenate([np.arange(2048), qcols, np.arange(2560, 3328), np.concatenate(gcols)])
    return jnp.take(w_in, jnp.asarray(cols, jnp.int32), axis=1).astype(BF)


def _proj_even_body(layer, x_ref, w_ref, gam_ref, cos_ref, sin_ref, hq_o, hk_o, hlf_o, hv_o, hg_o,
                    qn_o, qr_o, cmp_o, slc_o, win_o, gc_o, gs_o, gw_o):
    x = x_ref[...].astype(BF)

    def mm(a, b):
        return _dot(x, w_ref[:, a:b])

    gam = gam_ref[...]
    ge = jnp.exp(gam - jnp.max(gam, axis=0, keepdims=True))
    sm = ge / jnp.sum(ge, axis=0, keepdims=True)
    lb = jnp.sum(sm[0:layer + 1], axis=0, keepdims=True)
    c = cos_ref[...]
    s = sin_ref[...]
    hq = mm(0, 512)
    hq_o[...] = hq * jax.nn.sigmoid(hq)
    f = lb + (1.0 - lb) * jax.nn.sigmoid(mm(512, 1024))
    hk_o[...] = 1.0 - f
    hlf_o[...] = jnp.log(f)
    hv_o[...] = mm(1024, 1536)
    hg = mm(1536, 2048)
    hg_o[...] = hg * jax.nn.sigmoid(hg)
    for j in range(4):
        qj = mm(2048 + LANES * j, 2048 + LANES * (j + 1))
        qn_o[:, LANES * j:LANES * (j + 1)] = qj
        qr_o[:, LANES * j:LANES * (j + 1)] = _rope_chunk(qj, c, s)
    cmp_o[...] = mm(2560, 2816)
    slc_o[:, 0:LANES] = _rope_chunk(mm(2816, 2944), c, s)
    slc_o[:, LANES:2 * LANES] = mm(2944, 3072)
    win_o[:, 0:LANES] = _rope_chunk(mm(3072, 3200), c, s)
    win_o[:, LANES:2 * LANES] = mm(3200, 3328)
    gc_o[...] = jax.nn.sigmoid(mm(3328, 3840))
    gs_o[...] = jax.nn.sigmoid(mm(3840, 4352))
    gw_o[...] = jax.nn.sigmoid(mm(4352, 4864))


def _proj_even(x, w, gamma, cos, sin, layer, tm):
    n = x.shape[0]
    widths = [512] * 7 + [256] * 3 + [512] * 3
    row = lambda w_: pl.BlockSpec((tm, w_), lambda i: (i, 0))
    full = lambda a: pl.BlockSpec(a.shape, lambda i: (0,) * a.ndim)
    return pl.pallas_call(
        functools.partial(_proj_even_body, layer),
        grid=(n // tm,),
        in_specs=[row(D_MODEL), full(w), full(gamma), row(LANES), row(LANES)],
        out_specs=[row(w_) for w_ in widths],
        out_shape=[jax.ShapeDtypeStruct((n, w_), F32) for w_ in widths],
        compiler_params=_cparams(("parallel",)),
        name="proj_even",
    )(x, w, gamma, cos, sin)


def _hgrn_consts(C):
    L = int(math.log2(C))
    t = np.arange(C)[:, None]
    i = np.arange(C)[None, :]
    mats = [i <= t]
    bms = []
    for lv in range(L):
        half = 1 << lv
        blk = 2 * half
        mid = (t // blk) * blk + half
        upper = t >= mid
        mats.append(upper & (i >= mid) & (i <= t))
        mats.append((~upper) & (i > t) & (i <= mid - 1))
        bms.append((t // blk) == (i // blk))
    bms.append(t == i)
    sel = np.concatenate(mats, 0).astype(np.float32)
    bm = np.stack(bms).astype(np.float32)
    return jnp.asarray(sel, BF), jnp.asarray(np.concatenate([bm, bm], 1), F32)


def _hgrn_body(C, L, q_ref, k_ref, lf_ref, v_ref, gate_ref, g_ref, s0_ref, sel_ref, bm_ref,
               o_ref, so_ref, st_scr):
    ci = pl.program_id(1)
    npair = HG_HEADS // 2

    @pl.when(ci == 0)
    def _():
        st_scr[...] = s0_ref[0]

    lane = _iota((C, LANES), 1)
    row = _iota((C, LANES), 0)
    lm0 = lane < HEAD_DIM
    same_head = (_iota((LANES, LANES), 0) < HEAD_DIM) == (_iota((LANES, LANES), 1) < HEAD_DIM)

    def split_heads(a):
        return jnp.concatenate([jnp.where(lm0, a, 0.0), jnp.where(lm0, 0.0, a)], 0).astype(BF)

    lf = lf_ref[...]
    hi = lf.astype(BF)
    r1 = lf - hi.astype(F32)
    md = r1.astype(BF)
    lo = (r1 - md.astype(F32)).astype(BF)
    seg3 = _dot(sel_ref[...], jnp.concatenate([hi, md, lo], axis=1))
    w = npair * LANES
    seg_all = seg3[:, 0:w] + seg3[:, w:2 * w] + seg3[:, 2 * w:3 * w]

    for p in range(npair):
        cols = slice(LANES * p, LANES * (p + 1))
        q = q_ref[:, cols]
        k = k_ref[:, cols]
        v = v_ref[:, cols]
        seg = seg_all[:, cols]
        b = seg[0:C]
        a = _dot_nt(split_heads(q), k.astype(BF)) * bm_ref[L]
        for lv in range(L):
            up = ((row >> lv) & 1) == 1
            eu = jnp.where(up, jnp.exp(seg[(1 + 2 * lv) * C:(2 + 2 * lv) * C]), 0.0)
            el = jnp.where(up, 0.0, jnp.exp(seg[(2 + 2 * lv) * C:(3 + 2 * lv) * C]))
            a = a + _dot_nt(split_heads(q * eu), (k * el).astype(BF)) * bm_ref[lv]
        a2 = jnp.concatenate([a[0:C], a[C:2 * C]], axis=1).astype(BF)
        st = st_scr[p]
        o = _dot(a2, split_heads(v)) + _dot_nt((q * jnp.exp(b)).astype(BF), st.astype(BF))

        bend = b[C - 1:C]
        upd = _dot_tn(v.astype(BF), (k * jnp.exp(bend - b)).astype(BF))
        st_new = st * jnp.exp(bend) + jnp.where(same_head, upd, 0.0)
        st_scr[p] = st_new
        so_ref[0, p] = st_new

        o2 = o * o
        ms0 = jnp.sum(jnp.where(lm0, o2, 0.0), axis=1, keepdims=True) * (1.0 / HEAD_DIM)
        ms1 = jnp.sum(jnp.where(lm0, 0.0, o2), axis=1, keepdims=True) * (1.0 / HEAD_DIM)
        ms = jnp.where(lm0, ms0, ms1)
        o_ref[:, cols] = o * lax.rsqrt(ms + RMS_EPS) * g_ref[:, cols] * gate_ref[:, cols]


def _hgrn(q, k, lf, v, gate, g, st0, B, T, C):
    L = int(math.log2(C))
    sel, bm = _hgrn_consts(C)
    nck = T // C
    npair = HG_HEADS // 2
    tok = pl.BlockSpec((C, npair * LANES), lambda b, c: (b * nck + c, 0))
    stspec = pl.BlockSpec((1, npair, LANES, LANES), lambda b, c: (b, 0, 0, 0))
    full = lambda a: pl.BlockSpec(a.shape, lambda b, c: (0,) * a.ndim)
    return pl.pallas_call(
        functools.partial(_hgrn_body, C, L),
        grid=(B, nck),
        in_specs=[tok, tok, tok, tok, tok, full(g), stspec, full(sel), full(bm)],
        out_specs=[tok, stspec],
        out_shape=[jax.ShapeDtypeStruct((B * T, npair * LANES), F32),
                   jax.ShapeDtypeStruct((B, npair, LANES, LANES), F32)],
        scratch_shapes=[pltpu.VMEM((npair, LANES, LANES), F32)],
        compiler_params=_cparams(("parallel", "arbitrary")),
        name="hgrn2",
    )(q, k, lf, v, gate, g, st0, sel, bm)


def _state_to_pairs(s):
    B = s.shape[0]
    st = jnp.swapaxes(s, -1, -2).reshape(B, 4, 2, HEAD_DIM, HEAD_DIM)
    z = jnp.zeros_like(st[:, :, 0])
    top = jnp.concatenate([st[:, :, 0], z], -1)
    bot = jnp.concatenate([z, st[:, :, 1]], -1)
    return jnp.concatenate([top, bot], -2)


def _pairs_to_state(sp):
    B = sp.shape[0]
    a = sp[:, :, :HEAD_DIM, :HEAD_DIM]
    b = sp[:, :, HEAD_DIM:, HEAD_DIM:]
    return jnp.swapaxes(jnp.stack([a, b], 2).reshape(B, HG_HEADS, HEAD_DIM, HEAD_DIM), -1, -2)


def _stack_heads(ref, rows):
    lm0 = _iota((rows, LANES), 1) < HEAD_DIM
    parts = []
    for j in range(NSA_GROUP):
        cj = ref[:, LANES * j:LANES * (j + 1)]
        parts += [jnp.where(lm0, cj, 0.0), jnp.where(lm0, 0.0, cj)]
    return jnp.concatenate(parts, 0)


def _block_means(cmp_ref, pe_ref, kc_scr, vc_scr, nc):
    h = nc // 2
    ck = cmp_ref[:, 0:LANES].reshape(h, 2 * CMP_BLOCK, LANES)
    pe = pe_ref[...][None]
    kc_scr[0:h] = jnp.mean(ck[:, 0:CMP_BLOCK] + pe, axis=1)
    kc_scr[h:nc] = jnp.mean(ck[:, CMP_BLOCK:2 * CMP_BLOCK] + pe, axis=1)
    cv = cmp_ref[:, LANES:2 * LANES].reshape(h, 2 * CMP_BLOCK, LANES)
    vc_scr[0:h] = jnp.mean(cv[:, 0:CMP_BLOCK], axis=1)
    vc_scr[h:nc] = jnp.mean(cv[:, CMP_BLOCK:2 * CMP_BLOCK], axis=1)


def _cmp_and_select(kc, qn_st, q0, nc, nsel_rows, nsel):
    h = nc // 2
    ncol = 2 * NSA_GROUP * LANES
    s = _dot_nt(kc.astype(BF), qn_st) * (HEAD_DIM ** -0.5)
    r = _iota((nc, ncol), 0)
    cidx = jnp.where(r < h, 2 * r, 2 * (r - h) + 1)
    qpos = q0 + (_iota((nc, ncol), 1) & (LANES - 1))
    avail = (cidx + 1) * CMP_BLOCK - 1 <= qpos
    _, l, e = _masked_softmax(s, avail, 0)
    p = e / jnp.maximum(l, TINY)
    pp = p[0:h] + p[h:nc]
    w2 = 2 * LANES
    imp = pp[:, 0:w2] + pp[:, w2:2 * w2] + pp[:, 2 * w2:3 * w2] + pp[:, 3 * w2:4 * w2]
    if nsel_rows > h:
        imp = jnp.concatenate([imp, jnp.zeros((nsel_rows - h, w2), F32)], 0)
    blk = _iota((nsel_rows, w2), 0)
    qp = q0 + (_iota((nsel_rows, w2), 1) & (LANES - 1))
    forced = (blk == qp // SEL_BLOCK) | (blk == 0)
    imp = jnp.where(forced, SEL_FORCE, jnp.where(blk * SEL_BLOCK <= qp, imp, -1.0))
    imp = jnp.where(blk < nsel, imp, -2.0)
    rank = jnp.zeros((nsel_rows, w2), F32)
    for i in range(nsel):
        ri = imp[i:i + 1, :]
        beats = (ri > imp) | ((ri == imp) & (blk > i))
        rank = rank + jnp.where(beats, 1.0, 0.0)
    sel = jnp.where(rank < float(min(SEL_TOPN, nsel)), 1.0, 0.0)
    return p, sel


def _attend_rows(q_st, k, v, mask):
    s = _dot_nt(q_st, k) * (HEAD_DIM ** -0.5)
    _, l, e = _masked_softmax(s, mask, 1)
    return _dot(e.astype(BF), v) / jnp.maximum(l, TINY)


def _attend(q_st, k, v, mask, rows):
    o = _attend_rows(q_st, k, v, mask)
    lm0 = _iota((rows, LANES), 1) < HEAD_DIM
    return jnp.where(lm0, o[0:rows], o[rows:2 * rows])


def _nsa_prompt_body(T, qn_ref, qr_ref, cmp_ref, slc_ref, win_ref, gc_ref, gs_ref, gw_ref, pe_ref,
                     e_ref, o_ref, kc_scr, vc_scr):
    qb = pl.program_id(1)
    nc = T // CMP_BLOCK
    nsel = T // SEL_BLOCK
    R = LANES

    @pl.when(qb == 0)
    def _():
        _block_means(cmp_ref, pe_ref, kc_scr, vc_scr, nc)

    q0 = qb * R
    qn_st = _stack_heads(qn_ref, R).astype(BF)
    qr_st = _stack_heads(qr_ref, R).astype(BF)
    p, sel = _cmp_and_select(kc_scr[...], qn_st, q0, nc, nsel, nsel)
    oc = _dot(p.T.astype(BF), vc_scr[...].astype(BF))

    sel_e = _dot_tn(sel.astype(BF), e_ref[...])
    qpos = q0 + (_iota((2 * R, T), 0) & (R - 1))
    mask_s = (sel_e > 0.5) & (_iota((2 * R, T), 1) <= qpos)
    ks = slc_ref[:, 0:LANES].astype(BF)
    vs = slc_ref[:, LANES:2 * LANES].astype(BF)

    nw = NSA_WINDOW + R
    start = pl.multiple_of(jnp.maximum(qb - NSA_WINDOW // R, 0) * R, R)
    kw = win_ref[pl.ds(start, nw), 0:LANES].astype(BF)
    vw = win_ref[pl.ds(start, nw), LANES:2 * LANES].astype(BF)
    dist = q0 + (_iota((2 * R, nw), 0) & (R - 1)) - (start + _iota((2 * R, nw), 1))
    mask_w = (dist >= 0) & (dist <= NSA_WINDOW)

    lm0 = _iota((R, LANES), 1) < HEAD_DIM
    for j in range(NSA_GROUP):
        cols = slice(LANES * j, LANES * (j + 1))
        qj = qr_st[2 * R * j:2 * R * (j + 1)]
        o_s = _attend(qj, ks, vs, mask_s, R)
        o_w = _attend(qj, kw, vw, mask_w, R)
        o_c = jnp.where(lm0, oc[2 * R * j:2 * R * j + R], oc[2 * R * j + R:2 * R * (j + 1)])
        o_ref[:, cols] = gc_ref[:, cols] * o_c + gs_ref[:, cols] * o_s + gw_ref[:, cols] * o_w


def _sel_expand(nrows, nkeys):
    e = (np.arange(nkeys)[None, :] // SEL_BLOCK) == np.arange(nrows)[:, None]
    return jnp.asarray(e.astype(np.float32), BF)


def _nsa_prompt(qn, qr, cmp, slc, win, gc, gs, gw, pe, B, T):
    R = LANES
    nq = T // R
    tok = pl.BlockSpec((R, 512), lambda b, i: (b * nq + i, 0))
    seq = pl.BlockSpec((T, 256), lambda b, i: (b, 0))
    e = _sel_expand(T // SEL_BLOCK, T)
    full = lambda a: pl.BlockSpec(a.shape, lambda b, i: (0,) * a.ndim)
    return pl.pallas_call(
        functools.partial(_nsa_prompt_body, T),
        grid=(B, nq),
        in_specs=[tok, tok, seq, seq, seq, tok, tok, tok, full(pe), full(e)],
        out_specs=tok,
        out_shape=jax.ShapeDtypeStruct((B * T, 512), F32),
        scratch_shapes=[pltpu.VMEM((T // CMP_BLOCK, LANES), F32), pltpu.VMEM((T // CMP_BLOCK, LANES), F32)],
        compiler_params=_cparams(("parallel", "arbitrary")),
        name="nsa_prompt",
    )(qn, qr, cmp, slc, win, gc, gs, gw, pe, e)


def _attend_two_part(q_st, kt, vt, mask_p, k_new, v_new, mask_n):
    scale = HEAD_DIM ** -0.5
    sp = jnp.where(mask_p, _dot(q_st, kt) * scale, -jnp.inf)
    sn = jnp.where(mask_n, _dot_nt(q_st, k_new) * scale, -jnp.inf)
    m = jnp.maximum(jnp.max(sp, axis=1, keepdims=True), jnp.max(sn, axis=1, keepdims=True))
    m = jnp.where(m > -jnp.inf, m, 0.0)
    ep = jnp.exp(sp - m)
    en = jnp.exp(sn - m)
    l = jnp.sum(ep, axis=1, keepdims=True) + jnp.sum(en, axis=1, keepdims=True)
    return (_dot_nt(ep.astype(BF), vt) + _dot(en.astype(BF), v_new)) / jnp.maximum(l, TINY)


def _nsa_sample_body(P, npages, *refs):
    (qn_ref, qr_ref, slcn_ref, winn_ref, wint_ref, gc_ref, gs_ref, gw_ref, pet_ref, avg_ref, e_ref) = refs[1:12]
    cmp_pages = refs[12:12 + npages]
    slc_pages = refs[12 + npages:12 + 2 * npages]
    o_ref = refs[12 + 2 * npages]
    cmp_kt, cmp_vt, slc_kt, slc_vt = refs[13 + 2 * npages:]
    S = SAMPLE_ROWS
    nc = P // CMP_BLOCK
    nsel = -(-(P + 4) // SEL_BLOCK)
    nsel_rows = e_ref.shape[0]
    lw = wint_ref.shape[4]
    pair = lambda ref, kv: jnp.concatenate([ref[0, kv, 0], ref[0, kv, 1]], 0)
    for pg in range(npages):
        cols = slice(PAGE_SIZE * pg, PAGE_SIZE * (pg + 1))
        cmp_kt[:, cols] = pair(cmp_pages[pg], 0)
        cmp_vt[:, cols] = pair(cmp_pages[pg], 1)
        slc_kt[:, cols] = pair(slc_pages[pg], 0)
        slc_vt[:, cols] = pair(slc_pages[pg], 1)

    def block_means(x):
        hi = x.astype(BF)
        r1 = x - hi.astype(F32)
        md = r1.astype(BF)
        lo = (r1 - md.astype(F32)).astype(BF)
        a = avg_ref[...]
        return _dot(hi, a) + _dot(md, a) + _dot(lo, a)

    kc = block_means(cmp_kt[...] + pet_ref[...]).T
    vc = block_means(cmp_vt[...]).T

    lm0s = _iota((S, LANES), 1) < HEAD_DIM
    zpad = jnp.zeros((LANES - S, LANES), F32)
    qn_parts, qr_parts = [], []
    for j in range(NSA_GROUP):
        cn = qn_ref[:, LANES * j:LANES * (j + 1)]
        cr = qr_ref[:, LANES * j:LANES * (j + 1)]
        qn_parts += [jnp.where(lm0s, cn, 0.0), zpad, jnp.where(lm0s, 0.0, cn), zpad]
        qr_parts += [jnp.where(lm0s, cr, 0.0), jnp.where(lm0s, 0.0, cr)]
    qn_st = jnp.concatenate(qn_parts, 0).astype(BF)
    qr_st = jnp.concatenate(qr_parts, 0).astype(BF)
    p, sel = _cmp_and_select(kc, qn_st, P, nc, nsel_rows, nsel)
    pt = p.T
    pc = jnp.concatenate([pt[LANES * i:LANES * i + S] for i in range(2 * NSA_GROUP)], 0)
    o_c = _dot(pc.astype(BF), vc.astype(BF))

    R = 2 * NSA_GROUP * S
    pad16 = lambda a: jnp.concatenate([a, jnp.zeros(a.shape, F32)], 0).astype(BF)
    mask_n = _iota((R, 2 * S), 1) <= (_iota((R, 2 * S), 0) & (S - 1))
    sel_e = _dot_tn(sel.astype(BF), e_ref[...])
    sel_c = jnp.concatenate([sel_e[0:S], sel_e[LANES:LANES + S]] * NSA_GROUP, 0)
    o_s = _attend_two_part(qr_st, slc_kt[...].astype(BF), slc_vt[...].astype(BF), sel_c > 0.5,
                           pad16(slcn_ref[:, 0:LANES]), pad16(slcn_ref[:, LANES:2 * LANES]), mask_n)
    dist = (_iota((R, lw), 0) & (S - 1)) + lw - _iota((R, lw), 1)
    o_w = _attend_two_part(qr_st, pair(wint_ref, 0).astype(BF), pair(wint_ref, 1).astype(BF), dist <= NSA_WINDOW,
                           pad16(winn_ref[:, 0:LANES]), pad16(winn_ref[:, LANES:2 * LANES]), mask_n)
    for j in range(NSA_GROUP):
        cols = slice(LANES * j, LANES * (j + 1))
        r0 = slice(2 * S * j, 2 * S * j + S)
        r1 = slice(2 * S * j + S, 2 * S * (j + 1))
        pick = lambda a: jnp.where(lm0s, a[r0], a[r1])
        o_ref[:, cols] = gc_ref[:, cols] * pick(o_c) + gs_ref[:, cols] * pick(o_s) + gw_ref[:, cols] * pick(o_w)


def _nsa_sample(page_table, qn, qr, slc_new, win_new, win_cache, gc, gs, gw, pe, cmp_pool, slc_pool, B, P):
    S = SAMPLE_ROWS
    npages = P // PAGE_SIZE
    nc = P // CMP_BLOCK
    nsel_rows = 8 * (-(-(-(-(P + 4) // SEL_BLOCK)) // 8))
    e = _sel_expand(nsel_rows, P)
    rows_last = lambda a: jnp.transpose(a, (0, 2, 3, 4, 1))
    pet = jnp.tile(pe.reshape(CMP_BLOCK, LANES).T, (1, nc))
    blk = np.arange(P) // CMP_BLOCK
    col = np.where(blk % 2 == 0, blk // 2, nc // 2 + blk // 2)
    avg = jnp.asarray((col[:, None] == np.arange(nc)[None, :]).astype(np.float32) / CMP_BLOCK, BF)
    lw = win_cache.shape[1]
    tok = lambda w_: pl.BlockSpec((S, w_), lambda b, pt: (b, 0))
    full = lambda a: pl.BlockSpec(a.shape, lambda b, pt: (0,) * a.ndim)
    page = lambda pg: pl.BlockSpec((1, 2, NSA_KV_HEADS, HEAD_DIM, PAGE_SIZE),
                                   lambda b, pt: (pt[b * npages + pg], 0, 0, 0, 0))
    in_specs = ([tok(512), tok(512), tok(256), tok(256),
                 pl.BlockSpec((1, 2, NSA_KV_HEADS, HEAD_DIM, lw), lambda b, pt: (b, 0, 0, 0, 0)),
                 tok(512), tok(512), tok(512), full(pet), full(avg), full(e)]
                + [page(pg) for pg in range(npages)] * 2)
    gs_ = pltpu.PrefetchScalarGridSpec(
        num_scalar_prefetch=1, grid=(B,), in_specs=in_specs, out_specs=tok(512),
        scratch_shapes=[pltpu.VMEM((LANES, P), F32)] * 4)
    return pl.pallas_call(
        functools.partial(_nsa_sample_body, P, npages),
        grid_spec=gs_,
        out_shape=jax.ShapeDtypeStruct((B * S, 512), F32),
        compiler_params=_cparams(("arbitrary",)),
        name="nsa_sample",
    )(page_table.reshape(-1), qn, qr, slc_new, win_new, rows_last(win_cache), gc, gs, gw, pet, avg, e,
      *([rows_last(cmp_pool)] * npages), *([rows_last(slc_pool)] * npages))


def _outproj_ln_body(a_ref, b_ref, x_ref, wa_ref, wb_ref, g_ref, bb_ref, y_ref):
    mix = _dot(a_ref[...].astype(BF), wa_ref[...]) + _dot(b_ref[...].astype(BF), wb_ref[...])
    y_ref[...] = _layer_norm(ALPHA * x_ref[...] + mix, g_ref[...], bb_ref[...])


def _outproj_ln(a, acol, b, bcol, x, wa, wb, g, bb, tm):
    n = x.shape[0]
    full = lambda t: pl.BlockSpec(t.shape, lambda i: (0,) * t.ndim)
    return pl.pallas_call(
        _outproj_ln_body,
        grid=(n // tm,),
        in_specs=[pl.BlockSpec((tm, 512), lambda i: (i, acol)), pl.BlockSpec((tm, 512), lambda i: (i, bcol)),
                  pl.BlockSpec((tm, D_MODEL), lambda i: (i, 0)), full(wa), full(wb), full(g), full(bb)],
        out_specs=pl.BlockSpec((tm, D_MODEL), lambda i: (i, 0)),
        out_shape=jax.ShapeDtypeStruct((n, D_MODEL), F32),
        compiler_params=_cparams(("parallel",)),
        name="outproj_ln",
    )(a, b, x, wa, wb, g, bb)


def _proj_odd_body(x_ref, w_ref, cos_ref, sin_ref, q_o, kv_o):
    x = x_ref[...].astype(BF)
    c = cos_ref[...]
    s = sin_ref[...]
    nchunk = DIL_HEADS * HEAD_DIM // LANES
    for j in range(nchunk):
        cols = slice(LANES * j, LANES * (j + 1))
        q_o[:, cols] = _rope_chunk(_dot(x, w_ref[:, cols]), c, s)
        kv_o[:, cols] = _rope_chunk(_dot(x, w_ref[:, D_MODEL + LANES * j:D_MODEL + LANES * (j + 1)]), c, s)
    kv_o[:, D_MODEL:2 * D_MODEL] = _dot(x, w_ref[:, 2 * D_MODEL:3 * D_MODEL])


def _proj_odd(x, w, cos, sin, tm):
    n = x.shape[0]
    row = lambda w_: pl.BlockSpec((tm, w_), lambda i: (i, 0))
    return pl.pallas_call(
        _proj_odd_body,
        grid=(n // tm,),
        in_specs=[row(D_MODEL), pl.BlockSpec(w.shape, lambda i: (0, 0)), row(LANES), row(LANES)],
        out_specs=[row(D_MODEL), row(2 * D_MODEL)],
        out_shape=[jax.ShapeDtypeStruct((n, D_MODEL), F32), jax.ShapeDtypeStruct((n, 2 * D_MODEL), F32)],
        compiler_params=_cparams(("parallel",)),
        name="proj_odd",
    )(x, w, cos, sin)


def _dil_prompt_body(T, q_ref, k_ref, v_ref, o_ref, acc_scr, m_scr, l_scr):
    R = LANES
    lm0 = _iota((R, LANES), 1) < HEAD_DIM
    for ci, (window, d) in enumerate(DIL_CONFIGS):
        band = window // d
        nblk = T // d // R
        for r in range(d):
            for i in range(nblk):
                q0 = r + d * R * i
                rows_q = pl.ds(q0, R, stride=d) if d > 1 else pl.ds(q0, R)
                if i > 0:
                    k0, nk = q0 - d * R, 2 * R
                else:
                    k0, nk = q0, R
                rows_k = pl.ds(k0, nk, stride=d) if d > 1 else pl.ds(k0, nk)
                qs = q_ref[rows_q, :]
                q_st = jnp.concatenate([jnp.where(lm0, qs, 0.0), jnp.where(lm0, 0.0, qs)], 0).astype(BF)
                ks = k_ref[rows_k, :].astype(BF)
                vs = v_ref[rows_k, :].astype(BF)
                dist = (_iota((2 * R, nk), 0) & (R - 1)) + (nk - R) - _iota((2 * R, nk), 1)
                s = _dot_nt(q_st, ks) * (HEAD_DIM ** -0.5)
                m, l, e = _masked_softmax(s, (dist >= 0) & (dist <= band), 1)
                acc = _dot(e.astype(BF), vs)
                acc_scr[ci, rows_q, :] = jnp.where(lm0, acc[0:R], acc[R:2 * R])
                m_scr[ci, rows_q, :] = jnp.where(lm0, m[0:R], m[R:2 * R])
                l_scr[ci, rows_q, :] = jnp.where(lm0, l[0:R], l[R:2 * R])
    ncfg = len(DIL_CONFIGS)
    mx = m_scr[0]
    for ci in range(1, ncfg):
        mx = jnp.maximum(mx, m_scr[ci])
    num = jnp.zeros((T, LANES), F32)
    den = jnp.zeros((T, LANES), F32)
    for ci in range(ncfg):
        w = jnp.exp(m_scr[ci] - mx)
        num = num + w * acc_scr[ci]
        den = den + w * l_scr[ci]
    o_ref[...] = num / den


def _dil_prompt(q, kv, B, T):
    npair = DIL_HEADS // 2
    ncfg = len(DIL_CONFIGS)
    return pl.pallas_call(
        functools.partial(_dil_prompt_body, T),
        grid=(B, npair),
        in_specs=[pl.BlockSpec((T, LANES), lambda b, p: (b, p)),
                  pl.BlockSpec((T, LANES), lambda b, p: (b, p)),
                  pl.BlockSpec((T, LANES), lambda b, p: (b, npair + p))],
        out_specs=pl.BlockSpec((T, LANES), lambda b, p: (b, p)),
        out_shape=jax.ShapeDtypeStruct((B * T, D_MODEL), F32),
        scratch_shapes=[pltpu.VMEM((ncfg, T, LANES), F32)] * 3,
        compiler_params=_cparams(("parallel", "parallel")),
        name="dil_prompt",
    )(q, kv, kv)


def _dil_sample_body(L, q_ref, kvn_ref, kt_ref, vt_ref, o_ref):
    S = SAMPLE_ROWS
    H = DIL_HEADS
    R = H * S
    scale = HEAD_DIM ** -0.5
    pad16 = lambda a: jnp.concatenate([a, jnp.zeros(a.shape, F32)], 0).astype(BF)
    sb, sn = [], []
    for h in range(H):
        q16 = pad16(q_ref[h])
        sb.append(_dot(q16, kt_ref[0, 0, h].astype(BF))[0:S])
        sn.append(_dot_nt(q16, pad16(kvn_ref[h]))[0:S])
    parts = [
        (jnp.concatenate(sb, 0) * scale, L + (_iota((R, L), 0) & (S - 1)) - _iota((R, L), 1), False),
        (jnp.concatenate(sn, 0) * scale, (_iota((R, 2 * S), 0) & (S - 1)) - _iota((R, 2 * S), 1), True),
    ]
    stats = []
    for window, d in DIL_CONFIGS:
        masked = []
        for s, dist, signed in parts:
            ok = (dist <= window) & ((dist & (d - 1)) == 0)
            if signed:
                ok = ok & (dist >= 0)
            masked.append(jnp.where(ok, s, -jnp.inf))
        m = functools.reduce(jnp.maximum, [jnp.max(x, axis=1, keepdims=True) for x in masked])
        m = jnp.where(m > -jnp.inf, m, 0.0)
        es = [jnp.exp(x - m) for x in masked]
        l = functools.reduce(lambda a, b: a + b, [jnp.sum(e, axis=1, keepdims=True) for e in es])
        stats.append((m, l, es))
    mx = functools.reduce(jnp.maximum, [st[0] for st in stats])
    ws = [jnp.exp(m - mx) for m, _, _ in stats]
    inv = 1.0 / functools.reduce(lambda a, b: a + b, [w * l for w, (_, l, _) in zip(ws, stats)])
    pb, pn = [functools.reduce(lambda a, b: a + b, [w * es[pi] for w, (_, _, es) in zip(ws, stats)]) * inv
              for pi in range(2)]
    for h in range(H):
        rows = slice(S * h, S * (h + 1))
        o = _dot_nt(pad16(pb[rows]), vt_ref[0, 0, h].astype(BF)) + _dot(pad16(pn[rows]), pad16(kvn_ref[H + h]))
        o_ref[h] = o[0:S]


def _dil_sample(q, kv_new, cache, B):
    S = SAMPLE_ROWS
    L, H, Dh = cache.shape[1], cache.shape[3], cache.shape[4]
    n = B * S
    cache_t = jnp.transpose(cache, (0, 2, 3, 4, 1))
    q_h = q.reshape(n, H, Dh).transpose(1, 0, 2)
    kv_h = kv_new.reshape(n, 2 * H, Dh).transpose(1, 0, 2)
    part = lambda j: pl.BlockSpec((1, 1, H, Dh, L), lambda b: (b, j, 0, 0, 0))
    o = pl.pallas_call(
        functools.partial(_dil_sample_body, L),
        grid=(B,),
        in_specs=[pl.BlockSpec((H, S, Dh), lambda b: (0, b, 0)),
                  pl.BlockSpec((2 * H, S, Dh), lambda b: (0, b, 0)),
                  part(0), part(1)],
        out_specs=pl.BlockSpec((H, S, Dh), lambda b: (0, b, 0)),
        out_shape=jax.ShapeDtypeStruct((H, n, Dh), F32),
        compiler_params=_cparams(("parallel",)),
        name="dil_sample",
    )(q_h, kv_h, cache_t, cache_t)
    return o.transpose(1, 0, 2).reshape(n, H * Dh)


def _pop_max(work, iota_k, nrow):
    m = jnp.max(work, axis=0, keepdims=True)
    idx = jnp.min(jnp.where(work == m, iota_k, float(nrow)), axis=0, keepdims=True)
    return m, idx, jnp.where(iota_k == idx, -jnp.inf, work)


def _peer_route(s1, s2, tn, exact_ties):
    K = PEER_TOPK
    NK = PEER_NKEYS
    iota_k = _iota((NK, tn), 0).astype(F32)
    iota_r = _iota((K, tn), 0)

    w1_, w2_ = s1, s2
    v1, i1, i2 = [], [], []
    v2 = jnp.zeros((K, tn), F32)
    for it in range(K):
        if exact_ties:
            m1, idx1, w1_ = _pop_max(w1_, iota_k, NK)
            m2, idx2, w2_ = _pop_max(w2_, iota_k, NK)
            i1.append(idx1)
            i2.append(idx2)
        else:
            m1 = jnp.max(w1_, axis=0, keepdims=True)
            w1_ = jnp.where(w1_ == m1, -jnp.inf, w1_)
            m2 = jnp.max(w2_, axis=0, keepdims=True)
            w2_ = jnp.where(w2_ == m2, -jnp.inf, w2_)
        v1.append(m1)
        v2 = jnp.where(iota_r == it, m2, v2)
    if exact_ties:
        tied = jnp.zeros((1, tn), F32)
    else:
        gone = lambda w: jnp.sum(jnp.where(w == -jnp.inf, 1.0, 0.0), axis=0, keepdims=True)
        tied = jnp.where((gone(w1_) != float(K)) | (gone(w2_) != float(K)), 1.0, 0.0)

    rows = [K] + [8] * (K - 1)
    pieces = []
    for r1 in range(K):
        piece = v1[r1] + v2[0:rows[r1]]
        nvalid = K // (r1 + 1)
        if nvalid < rows[r1]:
            piece = jnp.where(_iota((rows[r1], tn), 0) < nvalid, piece, -jnp.inf)
        pieces.append(piece)
    cand = jnp.concatenate(pieces, 0)
    nc = cand.shape[0]
    iota_c = _iota((nc, tn), 0).astype(F32)
    work = cand
    for it in range(K):
        _, _, work = _pop_max(work, iota_c, nc)
    sel = (work == -jnp.inf) & (cand > -jnp.inf)
    z = jnp.sum(jnp.where(sel, jnp.exp(cand - cand[0:1]), 0.0), axis=0, keepdims=True)
    self_ = jnp.where(sel, 1.0, 0.0)

    r2 = jnp.full((NK, tn), float(K), F32) if exact_ties else jnp.zeros((NK, tn), F32)
    cntk = jnp.zeros((NK, tn), F32)
    off = 0
    for r in range(K):
        cnt_r = jnp.sum(self_[off:off + rows[r]], axis=0, keepdims=True)
        off += rows[r]
        if exact_ties:
            cntk = jnp.where(iota_k == i1[r], cnt_r, cntk)
            r2 = jnp.where(iota_k == i2[r], float(r), r2)
        else:
            cntk = jnp.where(s1 == v1[r], cnt_r, cntk)
            r2 = r2 + jnp.where(v2[r:r + 1] > s2, 1.0, 0.0)
    return r2.astype(BF), jnp.exp(s2 - v2[0:1]).astype(BF), cntk, jnp.exp(s1 - v1[0]) / z, tied


def _peer_topk_body(y_ref, wq_ref, keys_ref, r2_o, e2_o, cnt_o, w1_o):
    tn = y_ref.shape[0]
    q = _dot(y_ref[...].astype(BF), wq_ref[...]).astype(BF)
    s1 = _dot_nt(keys_ref[0, 0], q[:, 0:LANES])
    s2 = _dot_nt(keys_ref[0, 1], q[:, LANES:2 * LANES])

    def emit(exact_ties):
        r2, e2, cnt, w1, tied = _peer_route(s1, s2, tn, exact_ties)
        r2_o[0] = r2
        e2_o[0] = e2
        cnt_o[0] = cnt
        w1_o[0] = w1
        return tied

    tied = emit(False)

    @pl.when(jnp.max(tied) > 0.0)
    def _():
        emit(True)


def _peer_topk(y, wq, keys, tn):
    n = y.shape[0]
    out = pl.BlockSpec((1, PEER_NKEYS, tn), lambda i, h: (h, 0, i))
    shp = lambda dt: jax.ShapeDtypeStruct((PEER_HEADS, PEER_NKEYS, n), dt)
    return pl.pallas_call(
        _peer_topk_body,
        grid=(n // tn, PEER_HEADS),
        in_specs=[pl.BlockSpec((tn, D_MODEL), lambda i, h: (i, 0)),
                  pl.BlockSpec((D_MODEL, 2 * LANES), lambda i, h: (0, h)),
                  pl.BlockSpec((1, 2, PEER_NKEYS, LANES), lambda i, h: (h, 0, 0, 0))],
        out_specs=[out] * 4,
        out_shape=[shp(BF), shp(BF), shp(F32), shp(F32)],
        compiler_params=_cparams(("parallel", "arbitrary")),
        name="peer_topk",
    )(y, wq, keys)


def _gelu(x):
    return 0.5 * x * (1.0 + lax.erf(x * math.sqrt(0.5)))


def _peer_main_body(npe, xt_ref, u_ref, vt_ref, r2_ref, e2_ref, cnt_ref, w1_ref, yt_ref):
    e = pl.program_id(1)
    tn = xt_ref.shape[1]

    @pl.when(e == 0)
    def _():
        yt_ref[...] = jnp.zeros(yt_ref.shape, F32)

    xt = xt_ref[...]
    sub = 2 * PEER_NKEYS
    acc = yt_ref[...]
    nsub = npe // 2
    act = lambda s: _gelu(_dot(u_ref[sub * s:sub * (s + 1), :], xt).astype(BF))
    ahead = 2
    acts = [act(s) for s in range(min(ahead, nsub))]
    for s in range(nsub):
        if s + ahead < nsub:
            acts.append(act(s + ahead))
        a = acts[s]
        parts = []
        for cc in range(2):
            c = e * npe + 2 * s + cc
            g = jnp.zeros((PEER_NKEYS, tn), BF)
            for h in range(PEER_HEADS):
                cnt_row = cnt_ref[h, pl.ds(c, 1), :].astype(BF)
                w_row = w1_ref[h, pl.ds(c, 1), :].astype(BF)
                g = g + jnp.where(r2_ref[h] < cnt_row, e2_ref[h] * w_row, jnp.zeros((), BF))
            parts.append(g * a[PEER_NKEYS * cc:PEER_NKEYS * (cc + 1)].astype(BF))
        acc = acc + _dot(vt_ref[:, sub * s:sub * (s + 1)], jnp.concatenate(parts, 0))
    yt_ref[...] = acc


def _peer_main(xt, u, vt, r2, e2, cnt, w1, tn, te):
    n = xt.shape[1]
    ne = u.shape[0]
    npe = te // PEER_NKEYS
    tab = pl.BlockSpec((PEER_HEADS, PEER_NKEYS, tn), lambda i, e: (0, 0, i))
    return pl.pallas_call(
        functools.partial(_peer_main_body, npe),
        grid=(n // tn, ne // te),
        in_specs=[pl.BlockSpec((D_MODEL, tn), lambda i, e: (0, i)),
                  pl.BlockSpec((te, D_MODEL), lambda i, e: (e, 0)),
                  pl.BlockSpec((D_MODEL, te), lambda i, e: (0, e)),
                  tab, tab, tab, tab],
        out_specs=pl.BlockSpec((D_MODEL, tn), lambda i, e: (0, i)),
        out_shape=jax.ShapeDtypeStruct((D_MODEL, n), F32),
        compiler_params=_cparams(("parallel", "arbitrary")),
        name="peer_main",
    )(xt, u, vt, r2, e2, cnt, w1)


def _ln_t_body(x_ref, ft_ref, g_ref, b_ref, y_ref):
    y_ref[...] = _layer_norm(ALPHA * x_ref[...] + ft_ref[...].T, g_ref[...], b_ref[...])


def _ln_t(x, ft, g, b, tn):
    n = x.shape[0]
    full = lambda t: pl.BlockSpec(t.shape, lambda i: (0,) * t.ndim)
    return pl.pallas_call(
        _ln_t_body,
        grid=(n // tn,),
        in_specs=[pl.BlockSpec((tn, D_MODEL), lambda i: (i, 0)), pl.BlockSpec((D_MODEL, tn), lambda i: (0, i)),
                  full(g), full(b)],
        out_specs=pl.BlockSpec((tn, D_MODEL), lambda i: (i, 0)),
        out_shape=jax.ShapeDtypeStruct((n, D_MODEL), F32),
        compiler_params=_cparams(("parallel",)),
        name="ln_residual",
    )(x, ft, g, b)


def _peer_layer(y, wq, keys, u, v, g, b):
    r2, e2, cnt, w1 = _peer_topk(y, wq.astype(BF), keys.astype(BF), LANES)
    ft = _peer_main(y.T.astype(BF), u.astype(BF), v.T.astype(BF), r2, e2, cnt, w1, 512, 2048)
    return _ln_t(y, ft, g, b, 256)


def _pad_rows(a, S):
    return jnp.pad(a, ((0, 0), (0, S - a.shape[1])) + ((0, 0),) * (a.ndim - 2))


def kernel(x_prompt, x_sample, state_hgrn, cache_cmp_kv, cache_slc_kv, cache_win_kv, cache_dil_kv, page_table,
           hg_gamma, even_w_in, even_w_out, hg_norm_g, nsa_cmp_pe, odd_w_in, odd_w_out, ln_mix_g, ln_mix_b,
           peer_w_q, peer_sub_keys, peer_u, peer_v, ln_ffn_g, ln_ffn_b):
    B, T, D = x_prompt.shape
    Bs, Ts, _ = x_sample.shape
    S = SAMPLE_ROWS
    P = page_table.shape[1] * PAGE_SIZE
    npr = B * T
    yp = x_prompt.reshape(npr, D)
    ys = _pad_rows(x_sample, S).reshape(Bs * S, D)
    cos_p, sin_p = _rope_tables(jnp.tile(jnp.arange(T), B))
    cos_s, sin_s = _rope_tables(jnp.tile(P + jnp.arange(S), Bs))
    live = (jnp.arange(Bs * S) % S < Ts)[:, None]
    hperm = np.array([[j, NSA_GROUP + j] for j in range(NSA_GROUP)]).reshape(-1)
    outs = {}

    for layer in range(DEPTH):
        row2 = lambda a: a[layer].reshape(1, D)
        if layer % 2 == 0:
            e = layer // 2
            w = _even_weight(even_w_in[e])
            wo = even_w_out[e]
            wa = wo[:512].astype(BF)
            wb = wo[512:].reshape(NSA_HEADS, HEAD_DIM, D)[hperm].reshape(512, D).astype(BF)
            g = hg_norm_g[e].reshape(1, 512)
            pe = nsa_cmp_pe[e].reshape(CMP_BLOCK, LANES)
            hq, hk, hlf, hv, hg, qn, qr, cmp, slc, win, gc, gs, gw = _proj_even(yp, w, hg_gamma, cos_p, sin_p, layer, 256)
            st0 = jnp.zeros((B, HG_HEADS // 2, LANES, LANES), F32)
            o_hg, st = _hgrn(hq, hk, hlf, hv, hg, g, st0, B, T, LANES)
            nsa = _nsa_prompt(qn, qr, cmp, slc, win, gc, gs, gw, pe, B, T)
            mp = (o_hg, nsa)
            kv5 = lambda a, b_, t_: a.reshape(b_, t_, 2, NSA_KV_HEADS, HEAD_DIM)
            outs.setdefault("hg_p", []).append(_pairs_to_state(st))
            outs.setdefault("cmp_p", []).append(kv5(cmp, B, T))
            outs.setdefault("slc_p", []).append(kv5(slc, B, T))
            outs.setdefault("win_p", []).append(kv5(win, B, T)[:, -min(NSA_WINDOW, T):])
            hq, hk, hlf, hv, hg, qn, qr, cmp, slc, win, gc, gs, gw = _proj_even(ys, w, hg_gamma, cos_s, sin_s, layer, 256)
            hk = jnp.where(live, hk, 0.0)
            hlf = jnp.where(live, hlf, 0.0)
            C = 2 * S
            pad = lambda a: _pad_rows(a.reshape(Bs, S, 512), C).reshape(Bs * C, 512)
            o_hg, st = _hgrn(pad(hq), pad(hk), pad(hlf), pad(hv), pad(hg), g,
                             _state_to_pairs(state_hgrn[e].astype(F32)), Bs, C, C)
            o_hg = o_hg.reshape(Bs, C, 512)[:, :S].reshape(Bs * S, 512)
            nsa = _nsa_sample(page_table, qn, qr, slc, win, cache_win_kv[e], gc, gs, gw, nsa_cmp_pe[e],
                              cache_cmp_kv[e], cache_slc_kv[e], Bs, P)
            ms = (o_hg, nsa)
            outs.setdefault("hg_s", []).append(_pairs_to_state(st))
            outs.setdefault("cmp_s", []).append(kv5(cmp, Bs, S)[:, :Ts])
            outs.setdefault("slc_s", []).append(kv5(slc, Bs, S)[:, :Ts])
            outs.setdefault("win_s", []).append(kv5(win, Bs, S)[:, :Ts])
            acol, bcol = 0, 0
        else:
            o = layer // 2
            w = odd_w_in[o].astype(BF)
            wo = odd_w_out[o]
            wa = wo[:512].astype(BF)
            wb = wo[512:].astype(BF)
            q, kv = _proj_odd(yp, w, cos_p, sin_p, 256)
            att = _dil_prompt(q, kv, B, T)
            mp = (att, att)
            kv6 = lambda a, b_, t_: a.reshape(b_, t_, 2, DIL_HEADS, HEAD_DIM)
            outs.setdefault("dil_p", []).append(kv6(kv, B, T)[:, -min(DIL_CONFIGS[-1][0], T):])
            q, kv = _proj_odd(ys, w, cos_s, sin_s, 256)
            att = _dil_sample(q, kv, cache_dil_kv[o], Bs)
            ms = (att, att)
            outs.setdefault("dil_s", []).append(kv6(kv, Bs, S)[:, :Ts])
            acol, bcol = 0, 1
        lg, lbias = row2(ln_mix_g), row2(ln_mix_b)
        yp = _outproj_ln(mp[0], acol, mp[1], bcol, yp, wa, wb, lg, lbias, 256)
        ys = _outproj_ln(ms[0], acol, ms[1], bcol, ys, wa, wb, lg, lbias, 256)
        y = jnp.concatenate([yp, ys], 0)
        y = _peer_layer(y, peer_w_q[layer], peer_sub_keys[layer], peer_u[layer], peer_v[layer],
                        row2(ln_ffn_g), row2(ln_ffn_b))
        yp, ys = y[:npr], y[npr:]

    stack = lambda k_: jnp.stack(outs[k_])
    return (yp.reshape(B, T, D), ys.reshape(Bs, S, D)[:, :Ts], stack("hg_p"), stack("hg_s"),
            stack("cmp_p"), stack("cmp_s"), stack("slc_p"), stack("slc_s"), stack("win_p"), stack("win_s"),
            stack("dil_p"), stack("dil_s"))
```

```python
import functools
import math

import numpy as np
import jax
import jax.numpy as jnp
from jax import lax
from jax.experimental import pallas as pl
from jax.experimental.pallas import tpu as pltpu

F32 = jnp.float32
BF = jnp.bfloat16

D_MODEL = 1024
HEAD_DIM = 64
LANES = 128
ROPE_THETA = 10000.0
LN_EPS = 1e-5
RMS_EPS = 1e-6
TINY = 1e-30
DEPTH = 2
ALPHA = (2 * DEPTH) ** 0.25
PAGE_SIZE = 128

HG_HEADS = 8
NSA_HEADS = 8
NSA_KV_HEADS = 2
NSA_GROUP = NSA_HEADS // NSA_KV_HEADS
CMP_BLOCK = 32
SEL_BLOCK = 64
SEL_TOPN = 8
SEL_FORCE = 1e4
NSA_WINDOW = 512
DIL_HEADS = 16
DIL_CONFIGS = ((128, 1), (512, 4), (2048, 16))
PEER_HEADS = 8
PEER_NKEYS = 128
PEER_TOPK = 16
SAMPLE_ROWS = 8
QK_SCALE = HEAD_DIM ** -0.5

VMEM_LIMIT = 56 * 1024 * 1024


def _cparams(sem):
    return pltpu.CompilerParams(dimension_semantics=sem, vmem_limit_bytes=VMEM_LIMIT)


def _dot(a, b):
    return jnp.dot(a, b, preferred_element_type=F32)


def _dot_nt(a, b):
    return lax.dot_general(a, b, (((1,), (1,)), ((), ())), preferred_element_type=F32)


def _dot_tn(a, b):
    return lax.dot_general(a, b, (((0,), (0,)), ((), ())), preferred_element_type=F32)


def _iota(shape, dim):
    return lax.broadcasted_iota(jnp.int32, shape, dim)


def _masked_softmax(s, mask, axis):
    s = jnp.where(mask, s, -jnp.inf)
    m = jnp.max(s, axis=axis, keepdims=True)
    m = jnp.where(m > -jnp.inf, m, 0.0)
    e = jnp.exp(s - m)
    l = jnp.sum(e, axis=axis, keepdims=True)
    return m, l, e


def _rope_chunk(x, c, s):
    lane = _iota(x.shape, 1)
    sw = jnp.where((lane & 63) < 32, pltpu.roll(x, 96, 1), pltpu.roll(x, 32, 1))
    return x * c + sw * s


def _rope_tables(pos):
    half = HEAD_DIM // 2
    inv = ROPE_THETA ** (-jnp.arange(half, dtype=F32) / half)
    ang = pos.astype(F32)[:, None] * inv[None, :]
    cos, sin = jnp.cos(ang), jnp.sin(ang)
    return jnp.tile(cos, (1, 4)), jnp.tile(jnp.concatenate([-sin, sin], 1), (1, 2))


def _layer_norm(z, g, b):
    mu = jnp.mean(z, -1, keepdims=True)
    zc = z - mu
    var = jnp.mean(zc * zc, -1, keepdims=True)
    return zc * lax.rsqrt(var + LN_EPS) * g + b


EVEN_COLS = 2048 + 512 + 768 + 1536


def _even_weight(w_in):
    hperm = np.array([[j, NSA_GROUP + j] for j in range(NSA_GROUP)]).reshape(-1)
    qcols = (2048 + hperm[:, None] * HEAD_DIM + np.arange(HEAD_DIM)[None, :]).reshape(-1)
    gate0 = 2048 + 512 + 768
    gcols = []
    for c in range(3):
        for h in hperm:
            kvh, g = divmod(int(h), NSA_GROUP)
            gcols.append(np.full(HEAD_DIM, gate0 + c * NSA_HEADS + kvh * NSA_GROUP + g))
    cols = np.concatenate([np.arange(2048), qcols, np.arange(2560, 3328), np.concatenate(gcols)])
    return jnp.take(w_in, jnp.asarray(cols, jnp.int32), axis=1).astype(BF)


def _proj_even_body(layer, x_ref, w_ref, gam_ref, cos_ref, sin_ref, hq_o, hk_o, hlf_o, hv_o, hg_o,
                    qn_o, qr_o, cmp_o, slc_o, win_o, gc_o, gs_o, gw_o):
    x = x_ref[...].astype(BF)

    def mm(a, b):
        return _dot(x, w_ref[:, a:b])

    gam = gam_ref[...]
    ge = jnp.exp(gam - jnp.max(gam, axis=0, keepdims=True))
    sm = ge / jnp.sum(ge, axis=0, keepdims=True)
    lb = jnp.sum(sm[0:layer + 1], axis=0, keepdims=True)
    c = cos_ref[...]
    s = sin_ref[...]
    hq = mm(0, 512)
    hq_o[...] = hq * jax.nn.sigmoid(hq)
    f = lb + (1.0 - lb) * jax.nn.sigmoid(mm(512, 1024))
    hk_o[...] = 1.0 - f
    hlf_o[...] = jnp.log(f)
    hv_o[...] = mm(1024, 1536)
    hg = mm(1536, 2048)
    hg_o[...] = hg * jax.nn.sigmoid(hg)
    for j in range(4):
        qj = mm(2048 + LANES * j, 2048 + LANES * (j + 1))
        qn_o[:, LANES * j:LANES * (j + 1)] = qj
        qr_o[:, LANES * j:LANES * (j + 1)] = _rope_chunk(qj, c, s)
    cmp_o[...] = mm(2560, 2816)
    slc_o[:, 0:LANES] = _rope_chunk(mm(2816, 2944), c, s)
    slc_o[:, LANES:2 * LANES] = mm(2944, 3072)
    win_o[:, 0:LANES] = _rope_chunk(mm(3072, 3200), c, s)
    win_o[:, LANES:2 * LANES] = mm(3200, 3328)
    gc_o[...] = jax.nn.sigmoid(mm(3328, 3840))
    gs_o[...] = jax.nn.sigmoid(mm(3840, 4352))
    gw_o[...] = jax.nn.sigmoid(mm(4352, 4864))


def _proj_even(x, w, gamma, cos, sin, layer, tm):
    n = x.shape[0]
    widths = [512] * 7 + [256] * 3 + [512] * 3
    row = lambda w_: pl.BlockSpec((tm, w_), lambda i: (i, 0))
    full = lambda a: pl.BlockSpec(a.shape, lambda i: (0,) * a.ndim)
    return pl.pallas_call(
        functools.partial(_proj_even_body, layer),
        grid=(n // tm,),
        in_specs=[row(D_MODEL), full(w), full(gamma), row(LANES), row(LANES)],
        out_specs=[row(w_) for w_ in widths],
        out_shape=[jax.ShapeDtypeStruct((n, w_), F32) for w_ in widths],
        compiler_params=_cparams(("parallel",)),
        name="proj_even",
    )(x, w, gamma, cos, sin)


def _hgrn_consts(C):
    L = int(math.log2(C))
    t = np.arange(C)[:, None]
    i = np.arange(C)[None, :]
    mats = [i <= t]
    bms = []
    for lv in range(L):
        half = 1 << lv
        blk = 2 * half
        mid = (t // blk) * blk + half
        upper = t >= mid
        mats.append(upper & (i >= mid) & (i <= t))
        mats.append((~upper) & (i > t) & (i <= mid - 1))
        bms.append((t // blk) == (i // blk))
    bms.append(t == i)
    sel = np.concatenate(mats, 0).astype(np.float32)
    bm = np.stack(bms).astype(np.float32)
    return jnp.asarray(sel, BF), jnp.asarray(np.concatenate([bm, bm], 1), F32)


def _hgrn_body(C, L, q_ref, k_ref, lf_ref, v_ref, gate_ref, g_ref, s0_ref, sel_ref, bm_ref,
               o_ref, so_ref, st_scr):
    ci = pl.program_id(1)
    npair = HG_HEADS // 2

    @pl.when(ci == 0)
    def _():
        st_scr[...] = s0_ref[0]

    lane = _iota((C, LANES), 1)
    row = _iota((C, LANES), 0)
    lm0 = lane < HEAD_DIM
    same_head = (_iota((LANES, LANES), 0) < HEAD_DIM) == (_iota((LANES, LANES), 1) < HEAD_DIM)

    def split_heads(a):
        return jnp.concatenate([jnp.where(lm0, a, 0.0), jnp.where(lm0, 0.0, a)], 0).astype(BF)

    lf = lf_ref[...]
    hi = lf.astype(BF)
    r1 = lf - hi.astype(F32)
    md = r1.astype(BF)
    lo = (r1 - md.astype(F32)).astype(BF)
    seg3 = _dot(sel_ref[...], jnp.concatenate([hi, md, lo], axis=1))
    w = npair * LANES
    seg_all = seg3[:, 0:w] + seg3[:, w:2 * w] + seg3[:, 2 * w:3 * w]

    for p in range(npair):
        cols = slice(LANES * p, LANES * (p + 1))
        q = q_ref[:, cols]
        k = k_ref[:, cols]
        v = v_ref[:, cols]
        seg = seg_all[:, cols]
        b = seg[0:C]
        a = _dot_nt(split_heads(q), k.astype(BF)) * bm_ref[L]
        for lv in range(L):
            up = ((row >> lv) & 1) == 1
            eu = jnp.where(up, jnp.exp(seg[(1 + 2 * lv) * C:(2 + 2 * lv) * C]), 0.0)
            el = jnp.where(up, 0.0, jnp.exp(seg[(2 + 2 * lv) * C:(3 + 2 * lv) * C]))
            a = a + _dot_nt(split_heads(q * eu), (k * el).astype(BF)) * bm_ref[lv]
        a2 = jnp.concatenate([a[0:C], a[C:2 * C]], axis=1).astype(BF)
        st = st_scr[p]
        o = _dot(a2, split_heads(v)) + _dot_nt((q * jnp.exp(b)).astype(BF), st.astype(BF))

        bend = b[C - 1:C]
        upd = _dot_tn(v.astype(BF), (k * jnp.exp(bend - b)).astype(BF))
        st_new = st * jnp.exp(bend) + jnp.where(same_head, upd, 0.0)
        st_scr[p] = st_new
        so_ref[0, p] = st_new

        o2 = o * o
        ms0 = jnp.sum(jnp.where(lm0, o2, 0.0), axis=1, keepdims=True) * (1.0 / HEAD_DIM)
        ms1 = jnp.sum(jnp.where(lm0, 0.0, o2), axis=1, keepdims=True) * (1.0 / HEAD_DIM)
        ms = jnp.where(lm0, ms0, ms1)
        o_ref[:, cols] = o * lax.rsqrt(ms + RMS_EPS) * g_ref[:, cols] * gate_ref[:, cols]


def _hgrn(q, k, lf, v, gate, g, st0, B, T, C):
    L = int(math.log2(C))
    sel, bm = _hgrn_consts(C)
    nck = T // C
    npair = HG_HEADS // 2
    tok = pl.BlockSpec((C, npair * LANES), lambda b, c: (b * nck + c, 0))
    stspec = pl.BlockSpec((1, npair, LANES, LANES), lambda b, c: (b, 0, 0, 0))
    full = lambda a: pl.BlockSpec(a.shape, lambda b, c: (0,) * a.ndim)
    return pl.pallas_call(
        functools.partial(_hgrn_body, C, L),
        grid=(B, nck),
        in_specs=[tok, tok, tok, tok, tok, full(g), stspec, full(sel), full(bm)],
        out_specs=[tok, stspec],
        out_shape=[jax.ShapeDtypeStruct((B * T, npair * LANES), F32),
                   jax.ShapeDtypeStruct((B, npair, LANES, LANES), F32)],
        scratch_shapes=[pltpu.VMEM((npair, LANES, LANES), F32)],
        compiler_params=_cparams(("parallel", "arbitrary")),
        name="hgrn2",
    )(q, k, lf, v, gate, g, st0, sel, bm)


def _state_to_pairs(s):
    B = s.shape[0]
    st = jnp.swapaxes(s, -1, -2).reshape(B, 4, 2, HEAD_DIM, HEAD_DIM)
    z = jnp.zeros_like(st[:, :, 0])
    top = jnp.concatenate([st[:, :, 0], z], -1)
    bot = jnp.concatenate([z, st[:, :, 1]], -1)
    return jnp.concatenate([top, bot], -2)


def _pairs_to_state(sp):
    B = sp.shape[0]
    a = sp[:, :, :HEAD_DIM, :HEAD_DIM]
    b = sp[:, :, HEAD_DIM:, HEAD_DIM:]
    return jnp.swapaxes(jnp.stack([a, b], 2).reshape(B, HG_HEADS, HEAD_DIM, HEAD_DIM), -1, -2)


def _stack_heads(ref, rows):
    lm0 = _iota((rows, LANES), 1) < HEAD_DIM
    parts = []
    for j in range(NSA_GROUP):
        cj = ref[:, LANES * j:LANES * (j + 1)]
        parts += [jnp.where(lm0, cj, 0.0), jnp.where(lm0, 0.0, cj)]
    return jnp.concatenate(parts, 0)


def _block_means(cmp_ref, pe_ref, kc_scr, vc_scr, nc):
    h = nc // 2
    ck = cmp_ref[:, 0:LANES].reshape(h, 2 * CMP_BLOCK, LANES)
    pe = pe_ref[...][None]
    kc_scr[0:h] = jnp.mean(ck[:, 0:CMP_BLOCK] + pe, axis=1)
    kc_scr[h:nc] = jnp.mean(ck[:, CMP_BLOCK:2 * CMP_BLOCK] + pe, axis=1)
    cv = cmp_ref[:, LANES:2 * LANES].reshape(h, 2 * CMP_BLOCK, LANES)
    vc_scr[0:h] = jnp.mean(cv[:, 0:CMP_BLOCK], axis=1)
    vc_scr[h:nc] = jnp.mean(cv[:, CMP_BLOCK:2 * CMP_BLOCK], axis=1)


def _cmp_and_select(kc, qn_st, q0, nc, nsel_rows, nsel):
    h = nc // 2
    ncol = 2 * NSA_GROUP * LANES
    s = _dot_nt(kc.astype(BF), qn_st)
    r = _iota((nc, ncol), 0)
    cidx = jnp.where(r < h, 2 * r, 2 * (r - h) + 1)
    qpos = q0 + (_iota((nc, ncol), 1) & (LANES - 1))
    avail = (cidx + 1) * CMP_BLOCK - 1 <= qpos
    _, l, e = _masked_softmax(s, avail, 0)
    p = e / jnp.maximum(l, TINY)
    pp = p[0:h] + p[h:nc]
    w2 = 2 * LANES
    imp = pp[:, 0:w2] + pp[:, w2:2 * w2] + pp[:, 2 * w2:3 * w2] + pp[:, 3 * w2:4 * w2]
    if nsel_rows > h:
        imp = jnp.concatenate([imp, jnp.zeros((nsel_rows - h, w2), F32)], 0)
    blk = _iota((nsel_rows, w2), 0)
    qp = q0 + (_iota((nsel_rows, w2), 1) & (LANES - 1))
    forced = (blk == qp // SEL_BLOCK) | (blk == 0)
    imp = jnp.where(forced, SEL_FORCE, jnp.where(blk * SEL_BLOCK <= qp, imp, -1.0))
    imp = jnp.where(blk < nsel, imp, -2.0)
    rank = jnp.zeros((nsel_rows, w2), F32)
    for i in range(nsel):
        ri = imp[i:i + 1, :]
        beats = (ri > imp) | ((ri == imp) & (blk > i))
        rank = rank + jnp.where(beats, 1.0, 0.0)
    sel = jnp.where(rank < float(min(SEL_TOPN, nsel)), 1.0, 0.0)
    return p, sel


def _attend_rows(q_st, k, v, mask):
    s = _dot_nt(q_st, k)
    _, l, e = _masked_softmax(s, mask, 1)
    return _dot(e.astype(BF), v) / jnp.maximum(l, TINY)


def _attend(q_st, k, v, mask, rows):
    o = _attend_rows(q_st, k, v, mask)
    lm0 = _iota((rows, LANES), 1) < HEAD_DIM
    return jnp.where(lm0, o[0:rows], o[rows:2 * rows])


def _nsa_prompt_body(T, qn_ref, qr_ref, cmp_ref, slc_ref, win_ref, gc_ref, gs_ref, gw_ref, pe_ref,
                     e_ref, o_ref, kc_scr, vc_scr):
    qb = pl.program_id(1)
    nc = T // CMP_BLOCK
    nsel = T // SEL_BLOCK
    R = LANES

    @pl.when(qb == 0)
    def _():
        _block_means(cmp_ref, pe_ref, kc_scr, vc_scr, nc)

    q0 = qb * R
    qn_st = (_stack_heads(qn_ref, R) * QK_SCALE).astype(BF)
    qr_st = (_stack_heads(qr_ref, R) * QK_SCALE).astype(BF)
    p, sel = _cmp_and_select(kc_scr[...], qn_st, q0, nc, nsel, nsel)
    oc = _dot(p.T.astype(BF), vc_scr[...].astype(BF))

    sel_e = _dot_tn(sel.astype(BF), e_ref[...])
    qpos = q0 + (_iota((2 * R, T), 0) & (R - 1))
    mask_s = (sel_e > 0.5) & (_iota((2 * R, T), 1) <= qpos)
    ks = slc_ref[:, 0:LANES].astype(BF)
    vs = slc_ref[:, LANES:2 * LANES].astype(BF)

    nw = NSA_WINDOW + R
    start = pl.multiple_of(jnp.maximum(qb - NSA_WINDOW // R, 0) * R, R)
    kw = win_ref[pl.ds(start, nw), 0:LANES].astype(BF)
    vw = win_ref[pl.ds(start, nw), LANES:2 * LANES].astype(BF)
    dist = q0 + (_iota((2 * R, nw), 0) & (R - 1)) - (start + _iota((2 * R, nw), 1))
    mask_w = (dist >= 0) & (dist <= NSA_WINDOW)

    lm0 = _iota((R, LANES), 1) < HEAD_DIM
    for j in range(NSA_GROUP):
        cols = slice(LANES * j, LANES * (j + 1))
        qj = qr_st[2 * R * j:2 * R * (j + 1)]
        o_s = _attend(qj, ks, vs, mask_s, R)
        o_w = _attend(qj, kw, vw, mask_w, R)
        o_c = jnp.where(lm0, oc[2 * R * j:2 * R * j + R], oc[2 * R * j + R:2 * R * (j + 1)])
        o_ref[:, cols] = gc_ref[:, cols] * o_c + gs_ref[:, cols] * o_s + gw_ref[:, cols] * o_w


def _sel_expand(nrows, nkeys):
    e = (np.arange(nkeys)[None, :] // SEL_BLOCK) == np.arange(nrows)[:, None]
    return jnp.asarray(e.astype(np.float32), BF)


def _nsa_prompt(qn, qr, cmp, slc, win, gc, gs, gw, pe, B, T):
    R = LANES
    nq = T // R
    tok = pl.BlockSpec((R, 512), lambda b, i: (b * nq + i, 0))
    seq = pl.BlockSpec((T, 256), lambda b, i: (b, 0))
    e = _sel_expand(T // SEL_BLOCK, T)
    full = lambda a: pl.BlockSpec(a.shape, lambda b, i: (0,) * a.ndim)
    return pl.pallas_call(
        functools.partial(_nsa_prompt_body, T),
        grid=(B, nq),
        in_specs=[tok, tok, seq, seq, seq, tok, tok, tok, full(pe), full(e)],
        out_specs=tok,
        out_shape=jax.ShapeDtypeStruct((B * T, 512), F32),
        scratch_shapes=[pltpu.VMEM((T // CMP_BLOCK, LANES), F32), pltpu.VMEM((T // CMP_BLOCK, LANES), F32)],
        compiler_params=_cparams(("parallel", "arbitrary")),
        name="nsa_prompt",
    )(qn, qr, cmp, slc, win, gc, gs, gw, pe, e)


def _attend_two_part(q_st, kt, vt, mask_p, k_new, v_new, mask_n):
    sp = jnp.where(mask_p, _dot(q_st, kt), -jnp.inf)
    sn = jnp.where(mask_n, _dot_nt(q_st, k_new), -jnp.inf)
    m = jnp.maximum(jnp.max(sp, axis=1, keepdims=True), jnp.max(sn, axis=1, keepdims=True))
    m = jnp.where(m > -jnp.inf, m, 0.0)
    ep = jnp.exp(sp - m)
    en = jnp.exp(sn - m)
    l = jnp.sum(ep, axis=1, keepdims=True) + jnp.sum(en, axis=1, keepdims=True)
    return (_dot_nt(ep.astype(BF), vt) + _dot(en.astype(BF), v_new)) / jnp.maximum(l, TINY)


def _nsa_sample_body(P, npages, *refs):
    (qn_ref, qr_ref, slcn_ref, winn_ref, wint_ref, gc_ref, gs_ref, gw_ref, pet_ref, avg_ref, e_ref) = refs[1:12]
    cmp_pages = refs[12:12 + npages]
    slc_pages = refs[12 + npages:12 + 2 * npages]
    o_ref = refs[12 + 2 * npages]
    cmp_kt, cmp_vt, slc_kt, slc_vt = refs[13 + 2 * npages:]
    S = SAMPLE_ROWS
    nc = P // CMP_BLOCK
    nsel = -(-(P + 4) // SEL_BLOCK)
    nsel_rows = e_ref.shape[0]
    lw = wint_ref.shape[4]
    pair = lambda ref, kv: jnp.concatenate([ref[0, kv, 0], ref[0, kv, 1]], 0)
    for pg in range(npages):
        cols = slice(PAGE_SIZE * pg, PAGE_SIZE * (pg + 1))
        cmp_kt[:, cols] = pair(cmp_pages[pg], 0)
        cmp_vt[:, cols] = pair(cmp_pages[pg], 1)
        slc_kt[:, cols] = pair(slc_pages[pg], 0)
        slc_vt[:, cols] = pair(slc_pages[pg], 1)

    def block_means(x):
        hi = x.astype(BF)
        r1 = x - hi.astype(F32)
        md = r1.astype(BF)
        lo = (r1 - md.astype(F32)).astype(BF)
        a = avg_ref[...]
        return _dot(hi, a) + _dot(md, a) + _dot(lo, a)

    kc = block_means(cmp_kt[...] + pet_ref[...]).T
    vc = block_means(cmp_vt[...]).T

    lm0s = _iota((S, LANES), 1) < HEAD_DIM
    zpad = jnp.zeros((LANES - S, LANES), F32)
    qn_parts, qr_parts = [], []
    for j in range(NSA_GROUP):
        cn = qn_ref[:, LANES * j:LANES * (j + 1)]
        cr = qr_ref[:, LANES * j:LANES * (j + 1)]
        qn_parts += [jnp.where(lm0s, cn, 0.0), zpad, jnp.where(lm0s, 0.0, cn), zpad]
        qr_parts += [jnp.where(lm0s, cr, 0.0), jnp.where(lm0s, 0.0, cr)]
    qn_st = (jnp.concatenate(qn_parts, 0) * QK_SCALE).astype(BF)
    qr_st = (jnp.concatenate(qr_parts, 0) * QK_SCALE).astype(BF)
    p, sel = _cmp_and_select(kc, qn_st, P, nc, nsel_rows, nsel)
    pt = p.T
    pc = jnp.concatenate([pt[LANES * i:LANES * i + S] for i in range(2 * NSA_GROUP)], 0)
    o_c = _dot(pc.astype(BF), vc.astype(BF))

    R = 2 * NSA_GROUP * S
    pad16 = lambda a: jnp.concatenate([a, jnp.zeros(a.shape, F32)], 0).astype(BF)
    mask_n = _iota((R, 2 * S), 1) <= (_iota((R, 2 * S), 0) & (S - 1))
    sel_e = _dot_tn(sel.astype(BF), e_ref[...])
    sel_c = jnp.concatenate([sel_e[0:S], sel_e[LANES:LANES + S]] * NSA_GROUP, 0)
    o_s = _attend_two_part(qr_st, slc_kt[...].astype(BF), slc_vt[...].astype(BF), sel_c > 0.5,
                           pad16(slcn_ref[:, 0:LANES]), pad16(slcn_ref[:, LANES:2 * LANES]), mask_n)
    dist = (_iota((R, lw), 0) & (S - 1)) + lw - _iota((R, lw), 1)
    o_w = _attend_two_part(qr_st, pair(wint_ref, 0).astype(BF), pair(wint_ref, 1).astype(BF), dist <= NSA_WINDOW,
                           pad16(winn_ref[:, 0:LANES]), pad16(winn_ref[:, LANES:2 * LANES]), mask_n)
    for j in range(NSA_GROUP):
        cols = slice(LANES * j, LANES * (j + 1))
        r0 = slice(2 * S * j, 2 * S * j + S)
        r1 = slice(2 * S * j + S, 2 * S * (j + 1))
        pick = lambda a: jnp.where(lm0s, a[r0], a[r1])
        o_ref[:, cols] = gc_ref[:, cols] * pick(o_c) + gs_ref[:, cols] * pick(o_s) + gw_ref[:, cols] * pick(o_w)


def _nsa_sample(page_table, qn, qr, slc_new, win_new, win_cache, gc, gs, gw, pe, cmp_pool, slc_pool, B, P):
    S = SAMPLE_ROWS
    npages = P // PAGE_SIZE
    nc = P // CMP_BLOCK
    nsel_rows = 8 * (-(-(-(-(P + 4) // SEL_BLOCK)) // 8))
    e = _sel_expand(nsel_rows, P)
    rows_last = lambda a: jnp.transpose(a, (0, 2, 3, 4, 1))
    pet = jnp.tile(pe.reshape(CMP_BLOCK, LANES).T, (1, nc))
    blk = np.arange(P) // CMP_BLOCK
    col = np.where(blk % 2 == 0, blk // 2, nc // 2 + blk // 2)
    avg = jnp.asarray((col[:, None] == np.arange(nc)[None, :]).astype(np.float32) / CMP_BLOCK, BF)
    lw = win_cache.shape[1]
    tok = lambda w_: pl.BlockSpec((S, w_), lambda b, pt: (b, 0))
    full = lambda a: pl.BlockSpec(a.shape, lambda b, pt: (0,) * a.ndim)
    page = lambda pg: pl.BlockSpec((1, 2, NSA_KV_HEADS, HEAD_DIM, PAGE_SIZE),
                                   lambda b, pt: (pt[b * npages + pg], 0, 0, 0, 0))
    in_specs = ([tok(512), tok(512), tok(256), tok(256),
                 pl.BlockSpec((1, 2, NSA_KV_HEADS, HEAD_DIM, lw), lambda b, pt: (b, 0, 0, 0, 0)),
                 tok(512), tok(512), tok(512), full(pet), full(avg), full(e)]
                + [page(pg) for pg in range(npages)] * 2)
    gs_ = pltpu.PrefetchScalarGridSpec(
        num_scalar_prefetch=1, grid=(B,), in_specs=in_specs, out_specs=tok(512),
        scratch_shapes=[pltpu.VMEM((LANES, P), F32)] * 4)
    return pl.pallas_call(
        functools.partial(_nsa_sample_body, P, npages),
        grid_spec=gs_,
        out_shape=jax.ShapeDtypeStruct((B * S, 512), F32),
        compiler_params=_cparams(("arbitrary",)),
        name="nsa_sample",
    )(page_table.reshape(-1), qn, qr, slc_new, win_new, rows_last(win_cache), gc, gs, gw, pet, avg, e,
      *([rows_last(cmp_pool)] * npages), *([rows_last(slc_pool)] * npages))


def _outproj_ln_body(a_ref, b_ref, x_ref, wa_ref, wb_ref, g_ref, bb_ref, y_ref):
    mix = _dot(a_ref[...].astype(BF), wa_ref[...]) + _dot(b_ref[...].astype(BF), wb_ref[...])
    y_ref[...] = _layer_norm(ALPHA * x_ref[...] + mix, g_ref[...], bb_ref[...])


def _outproj_ln(a, acol, b, bcol, x, wa, wb, g, bb, tm):
    n = x.shape[0]
    full = lambda t: pl.BlockSpec(t.shape, lambda i: (0,) * t.ndim)
    return pl.pallas_call(
        _outproj_ln_body,
        grid=(n // tm,),
        in_specs=[pl.BlockSpec((tm, 512), lambda i: (i, acol)), pl.BlockSpec((tm, 512), lambda i: (i, bcol)),
                  pl.BlockSpec((tm, D_MODEL), lambda i: (i, 0)), full(wa), full(wb), full(g), full(bb)],
        out_specs=pl.BlockSpec((tm, D_MODEL), lambda i: (i, 0)),
        out_shape=jax.ShapeDtypeStruct((n, D_MODEL), F32),
        compiler_params=_cparams(("parallel",)),
        name="outproj_ln",
    )(a, b, x, wa, wb, g, bb)


def _proj_odd_body(x_ref, w_ref, cos_ref, sin_ref, q_o, kv_o):
    x = x_ref[...].astype(BF)
    c = cos_ref[...]
    s = sin_ref[...]
    nchunk = DIL_HEADS * HEAD_DIM // LANES
    for j in range(nchunk):
        cols = slice(LANES * j, LANES * (j + 1))
        q_o[:, cols] = _rope_chunk(_dot(x, w_ref[:, cols]), c, s)
        kv_o[:, cols] = _rope_chunk(_dot(x, w_ref[:, D_MODEL + LANES * j:D_MODEL + LANES * (j + 1)]), c, s)
    kv_o[:, D_MODEL:2 * D_MODEL] = _dot(x, w_ref[:, 2 * D_MODEL:3 * D_MODEL])


def _proj_odd(x, w, cos, sin, tm):
    n = x.shape[0]
    row = lambda w_: pl.BlockSpec((tm, w_), lambda i: (i, 0))
    return pl.pallas_call(
        _proj_odd_body,
        grid=(n // tm,),
        in_specs=[row(D_MODEL), pl.BlockSpec(w.shape, lambda i: (0, 0)), row(LANES), row(LANES)],
        out_specs=[row(D_MODEL), row(2 * D_MODEL)],
        out_shape=[jax.ShapeDtypeStruct((n, D_MODEL), F32), jax.ShapeDtypeStruct((n, 2 * D_MODEL), F32)],
        compiler_params=_cparams(("parallel",)),
        name="proj_odd",
    )(x, w, cos, sin)


def _dil_prompt_body(T, q_ref, k_ref, v_ref, o_ref, acc_scr, m_scr, l_scr):
    R = LANES
    lm0 = _iota((R, LANES), 1) < HEAD_DIM
    for ci, (window, d) in enumerate(DIL_CONFIGS):
        band = window // d
        nblk = T // d // R
        for r in range(d):
            for i in range(nblk):
                q0 = r + d * R * i
                rows_q = pl.ds(q0, R, stride=d) if d > 1 else pl.ds(q0, R)
                if i > 0:
                    k0, nk = q0 - d * R, 2 * R
                else:
                    k0, nk = q0, R
                rows_k = pl.ds(k0, nk, stride=d) if d > 1 else pl.ds(k0, nk)
                qs = q_ref[rows_q, :] * QK_SCALE
                q_st = jnp.concatenate([jnp.where(lm0, qs, 0.0), jnp.where(lm0, 0.0, qs)], 0).astype(BF)
                ks = k_ref[rows_k, :].astype(BF)
                vs = v_ref[rows_k, :].astype(BF)
                dist = (_iota((2 * R, nk), 0) & (R - 1)) + (nk - R) - _iota((2 * R, nk), 1)
                s = _dot_nt(q_st, ks)
                m, l, e = _masked_softmax(s, (dist >= 0) & (dist <= band), 1)
                acc = _dot(e.astype(BF), vs)
                acc_scr[ci, rows_q, :] = jnp.where(lm0, acc[0:R], acc[R:2 * R])
                m_scr[ci, rows_q, :] = jnp.where(lm0, m[0:R], m[R:2 * R])
                l_scr[ci, rows_q, :] = jnp.where(lm0, l[0:R], l[R:2 * R])
    ncfg = len(DIL_CONFIGS)
    mx = m_scr[0]
    for ci in range(1, ncfg):
        mx = jnp.maximum(mx, m_scr[ci])
    num = jnp.zeros((T, LANES), F32)
    den = jnp.zeros((T, LANES), F32)
    for ci in range(ncfg):
        w = jnp.exp(m_scr[ci] - mx)
        num = num + w * acc_scr[ci]
        den = den + w * l_scr[ci]
    o_ref[...] = num / den


def _dil_prompt(q, kv, B, T):
    npair = DIL_HEADS // 2
    ncfg = len(DIL_CONFIGS)
    return pl.pallas_call(
        functools.partial(_dil_prompt_body, T),
        grid=(B, npair),
        in_specs=[pl.BlockSpec((T, LANES), lambda b, p: (b, p)),
                  pl.BlockSpec((T, LANES), lambda b, p: (b, p)),
                  pl.BlockSpec((T, LANES), lambda b, p: (b, npair + p))],
        out_specs=pl.BlockSpec((T, LANES), lambda b, p: (b, p)),
        out_shape=jax.ShapeDtypeStruct((B * T, D_MODEL), F32),
        scratch_shapes=[pltpu.VMEM((ncfg, T, LANES), F32)] * 3,
        compiler_params=_cparams(("parallel", "parallel")),
        name="dil_prompt",
    )(q, kv, kv)


def _dil_sample_body(L, q_ref, kvn_ref, kt_ref, vt_ref, o_ref):
    S = SAMPLE_ROWS
    H = DIL_HEADS
    R = H * S
    pad16 = lambda a: jnp.concatenate([a, jnp.zeros(a.shape, F32)], 0).astype(BF)
    sb, sn = [], []
    for h in range(H):
        q16 = pad16(q_ref[h] * QK_SCALE)
        sb.append(_dot(q16, kt_ref[0, 0, h].astype(BF))[0:S])
        sn.append(_dot_nt(q16, pad16(kvn_ref[h]))[0:S])
    parts = [
        (jnp.concatenate(sb, 0), L + (_iota((R, L), 0) & (S - 1)) - _iota((R, L), 1), False),
        (jnp.concatenate(sn, 0), (_iota((R, 2 * S), 0) & (S - 1)) - _iota((R, 2 * S), 1), True),
    ]
    stats = []
    for window, d in DIL_CONFIGS:
        masked = []
        for s, dist, signed in parts:
            ok = (dist <= window) & ((dist & (d - 1)) == 0)
            if signed:
                ok = ok & (dist >= 0)
            masked.append(jnp.where(ok, s, -jnp.inf))
        m = functools.reduce(jnp.maximum, [jnp.max(x, axis=1, keepdims=True) for x in masked])
        m = jnp.where(m > -jnp.inf, m, 0.0)
        es = [jnp.exp(x - m) for x in masked]
        l = functools.reduce(lambda a, b: a + b, [jnp.sum(e, axis=1, keepdims=True) for e in es])
        stats.append((m, l, es))
    mx = functools.reduce(jnp.maximum, [st[0] for st in stats])
    ws = [jnp.exp(m - mx) for m, _, _ in stats]
    inv = 1.0 / functools.reduce(lambda a, b: a + b, [w * l for w, (_, l, _) in zip(ws, stats)])
    pb, pn = [functools.reduce(lambda a, b: a + b, [w * es[pi] for w, (_, _, es) in zip(ws, stats)]) * inv
              for pi in range(2)]
    for h in range(H):
        rows = slice(S * h, S * (h + 1))
        o = _dot_nt(pad16(pb[rows]), vt_ref[0, 0, h].astype(BF)) + _dot(pad16(pn[rows]), pad16(kvn_ref[H + h]))
        o_ref[h] = o[0:S]


def _dil_sample(q, kv_new, cache, B):
    S = SAMPLE_ROWS
    L, H, Dh = cache.shape[1], cache.shape[3], cache.shape[4]
    n = B * S
    cache_t = jnp.transpose(cache, (0, 2, 3, 4, 1))
    q_h = q.reshape(n, H, Dh).transpose(1, 0, 2)
    kv_h = kv_new.reshape(n, 2 * H, Dh).transpose(1, 0, 2)
    part = lambda j: pl.BlockSpec((1, 1, H, Dh, L), lambda b: (b, j, 0, 0, 0))
    o = pl.pallas_call(
        functools.partial(_dil_sample_body, L),
        grid=(B,),
        in_specs=[pl.BlockSpec((H, S, Dh), lambda b: (0, b, 0)),
                  pl.BlockSpec((2 * H, S, Dh), lambda b: (0, b, 0)),
                  part(0), part(1)],
        out_specs=pl.BlockSpec((H, S, Dh), lambda b: (0, b, 0)),
        out_shape=jax.ShapeDtypeStruct((H, n, Dh), F32),
        compiler_params=_cparams(("parallel",)),
        name="dil_sample",
    )(q_h, kv_h, cache_t, cache_t)
    return o.transpose(1, 0, 2).reshape(n, H * Dh)


def _pop_max(work, iota_k, nrow):
    m = jnp.max(work, axis=0, keepdims=True)
    idx = jnp.min(jnp.where(work == m, iota_k, float(nrow)), axis=0, keepdims=True)
    return m, idx, jnp.where(iota_k == idx, -jnp.inf, work)


def _peer_route(s1, s2, tn, exact_ties):
    K = PEER_TOPK
    NK = PEER_NKEYS
    iota_k = _iota((NK, tn), 0).astype(F32)
    iota_r = _iota((K, tn), 0)

    w1_, w2_ = s1, s2
    v1, i1, i2 = [], [], []
    v2 = jnp.zeros((K, tn), F32)
    for it in range(K):
        if exact_ties:
            m1, idx1, w1_ = _pop_max(w1_, iota_k, NK)
            m2, idx2, w2_ = _pop_max(w2_, iota_k, NK)
            i1.append(idx1)
            i2.append(idx2)
        else:
            m1 = jnp.max(w1_, axis=0, keepdims=True)
            w1_ = jnp.where(w1_ == m1, -jnp.inf, w1_)
            m2 = jnp.max(w2_, axis=0, keepdims=True)
            w2_ = jnp.where(w2_ == m2, -jnp.inf, w2_)
        v1.append(m1)
        v2 = jnp.where(iota_r == it, m2, v2)
    if exact_ties:
        tied = jnp.zeros((1, tn), F32)
    else:
        gone = lambda w: jnp.sum(jnp.where(w == -jnp.inf, 1.0, 0.0), axis=0, keepdims=True)
        tied = jnp.where((gone(w1_) != float(K)) | (gone(w2_) != float(K)), 1.0, 0.0)

    rows = [K] + [8] * (K - 1)
    pieces = []
    for r1 in range(K):
        piece = v1[r1] + v2[0:rows[r1]]
        nvalid = K // (r1 + 1)
        if nvalid < rows[r1]:
            piece = jnp.where(_iota((rows[r1], tn), 0) < nvalid, piece, -jnp.inf)
        pieces.append(piece)
    cand = jnp.concatenate(pieces, 0)
    nc = cand.shape[0]
    iota_c = _iota((nc, tn), 0).astype(F32)
    work = cand
    for it in range(K):
        _, _, work = _pop_max(work, iota_c, nc)
    sel = (work == -jnp.inf) & (cand > -jnp.inf)
    z = jnp.sum(jnp.where(sel, jnp.exp(cand - cand[0:1]), 0.0), axis=0, keepdims=True)
    self_ = jnp.where(sel, 1.0, 0.0)

    r2 = jnp.full((NK, tn), float(K), F32) if exact_ties else jnp.zeros((NK, tn), F32)
    cntk = jnp.zeros((NK, tn), F32)
    off = 0
    for r in range(K):
        cnt_r = jnp.sum(self_[off:off + rows[r]], axis=0, keepdims=True)
        off += rows[r]
        if exact_ties:
            cntk = jnp.where(iota_k == i1[r], cnt_r, cntk)
            r2 = jnp.where(iota_k == i2[r], float(r), r2)
        else:
            cntk = jnp.where(s1 == v1[r], cnt_r, cntk)
            r2 = r2 + jnp.where(v2[r:r + 1] > s2, 1.0, 0.0)
    return r2.astype(BF), jnp.exp(s2 - v2[0:1]).astype(BF), cntk, jnp.exp(s1 - v1[0]) / z, tied


def _peer_topk_body(y_ref, wq_ref, keys_ref, r2_o, e2_o, cnt_o, w1_o):
    tn = y_ref.shape[0]
    q = _dot(y_ref[...].astype(BF), wq_ref[...]).astype(BF)
    s1 = _dot_nt(keys_ref[0, 0], q[:, 0:LANES])
    s2 = _dot_nt(keys_ref[0, 1], q[:, LANES:2 * LANES])

    def emit(exact_ties):
        r2, e2, cnt, w1, tied = _peer_route(s1, s2, tn, exact_ties)
        r2_o[0] = r2
        e2_o[0] = e2
        cnt_o[0] = cnt
        w1_o[0] = w1
        return tied

    tied = emit(False)

    @pl.when(jnp.max(tied) > 0.0)
    def _():
        emit(True)


def _peer_topk(y, wq, keys, tn):
    n = y.shape[0]
    out = pl.BlockSpec((1, PEER_NKEYS, tn), lambda i, h: (h, 0, i))
    shp = lambda dt: jax.ShapeDtypeStruct((PEER_HEADS, PEER_NKEYS, n), dt)
    return pl.pallas_call(
        _peer_topk_body,
        grid=(n // tn, PEER_HEADS),
        in_specs=[pl.BlockSpec((tn, D_MODEL), lambda i, h: (i, 0)),
                  pl.BlockSpec((D_MODEL, 2 * LANES), lambda i, h: (0, h)),
                  pl.BlockSpec((1, 2, PEER_NKEYS, LANES), lambda i, h: (h, 0, 0, 0))],
        out_specs=[out] * 4,
        out_shape=[shp(BF), shp(BF), shp(F32), shp(F32)],
        compiler_params=_cparams(("parallel", "arbitrary")),
        name="peer_topk",
    )(y, wq, keys)


def _gelu(x):
    return 0.5 * x * (1.0 + lax.erf(x * math.sqrt(0.5)))


def _peer_main_body(npe, xt_ref, u_ref, vt_ref, r2_ref, e2_ref, cnt_ref, w1_ref, yt_ref):
    e = pl.program_id(1)
    tn = xt_ref.shape[1]

    @pl.when(e == 0)
    def _():
        yt_ref[...] = jnp.zeros(yt_ref.shape, F32)

    xt = xt_ref[...]
    sub = 2 * PEER_NKEYS
    acc = yt_ref[...]
    nsub = npe // 2
    act = lambda s: _gelu(_dot(u_ref[sub * s:sub * (s + 1), :], xt).astype(BF))
    ahead = 2
    acts = [act(s) for s in range(min(ahead, nsub))]
    for s in range(nsub):
        if s + ahead < nsub:
            acts.append(act(s + ahead))
        a = acts[s]
        parts = []
        for cc in range(2):
            c = e * npe + 2 * s + cc
            g = jnp.zeros((PEER_NKEYS, tn), BF)
            for h in range(PEER_HEADS):
                cnt_row = cnt_ref[h, pl.ds(c, 1), :].astype(BF)
                w_row = w1_ref[h, pl.ds(c, 1), :].astype(BF)
                g = g + jnp.where(r2_ref[h] < cnt_row, e2_ref[h] * w_row, jnp.zeros((), BF))
            parts.append(g * a[PEER_NKEYS * cc:PEER_NKEYS * (cc + 1)].astype(BF))
        acc = acc + _dot(vt_ref[:, sub * s:sub * (s + 1)], jnp.concatenate(parts, 0))
    yt_ref[...] = acc


def _peer_main(xt, u, vt, r2, e2, cnt, w1, tn, te):
    n = xt.shape[1]
    ne = u.shape[0]
    npe = te // PEER_NKEYS
    tab = pl.BlockSpec((PEER_HEADS, PEER_NKEYS, tn), lambda i, e: (0, 0, i))
    return pl.pallas_call(
        functools.partial(_peer_main_body, npe),
        grid=(n // tn, ne // te),
        in_specs=[pl.BlockSpec((D_MODEL, tn), lambda i, e: (0, i)),
                  pl.BlockSpec((te, D_MODEL), lambda i, e: (e, 0)),
                  pl.BlockSpec((D_MODEL, te), lambda i, e: (0, e)),
                  tab, tab, tab, tab],
        out_specs=pl.BlockSpec((D_MODEL, tn), lambda i, e: (0, i)),
        out_shape=jax.ShapeDtypeStruct((D_MODEL, n), F32),
        compiler_params=_cparams(("parallel", "arbitrary")),
        name="peer_main",
    )(xt, u, vt, r2, e2, cnt, w1)


def _ln_t_body(x_ref, ft_ref, g_ref, b_ref, y_ref):
    y_ref[...] = _layer_norm(ALPHA * x_ref[...] + ft_ref[...].T, g_ref[...], b_ref[...])


def _ln_t(x, ft, g, b, tn):
    n = x.shape[0]
    full = lambda t: pl.BlockSpec(t.shape, lambda i: (0,) * t.ndim)
    return pl.pallas_call(
        _ln_t_body,
        grid=(n // tn,),
        in_specs=[pl.BlockSpec((tn, D_MODEL), lambda i: (i, 0)), pl.BlockSpec((D_MODEL, tn), lambda i: (0, i)),
                  full(g), full(b)],
        out_specs=pl.BlockSpec((tn, D_MODEL), lambda i: (i, 0)),
        out_shape=jax.ShapeDtypeStruct((n, D_MODEL), F32),
        compiler_params=_cparams(("parallel",)),
        name="ln_residual",
    )(x, ft, g, b)


def _peer_layer(y, wq, keys, u, v, g, b):
    r2, e2, cnt, w1 = _peer_topk(y, wq.astype(BF), keys.astype(BF), LANES)
    ft = _peer_main(y.T.astype(BF), u.astype(BF), v.T.astype(BF), r2, e2, cnt, w1, 512, 2048)
    return _ln_t(y, ft, g, b, 256)


def _pad_rows(a, S):
    return jnp.pad(a, ((0, 0), (0, S - a.shape[1])) + ((0, 0),) * (a.ndim - 2))


def kernel(x_prompt, x_sample, state_hgrn, cache_cmp_kv, cache_slc_kv, cache_win_kv, cache_dil_kv, page_table,
           hg_gamma, even_w_in, even_w_out, hg_norm_g, nsa_cmp_pe, odd_w_in, odd_w_out, ln_mix_g, ln_mix_b,
           peer_w_q, peer_sub_keys, peer_u, peer_v, ln_ffn_g, ln_ffn_b):
    B, T, D = x_prompt.shape
    Bs, Ts, _ = x_sample.shape
    S = SAMPLE_ROWS
    P = page_table.shape[1] * PAGE_SIZE
    npr = B * T
    yp = x_prompt.reshape(npr, D)
    ys = _pad_rows(x_sample, S).reshape(Bs * S, D)
    cos_p, sin_p = _rope_tables(jnp.tile(jnp.arange(T), B))
    cos_s, sin_s = _rope_tables(jnp.tile(P + jnp.arange(S), Bs))
    live = (jnp.arange(Bs * S) % S < Ts)[:, None]
    hperm = np.array([[j, NSA_GROUP + j] for j in range(NSA_GROUP)]).reshape(-1)
    outs = {}

    for layer in range(DEPTH):
        row2 = lambda a: a[layer].reshape(1, D)
        if layer % 2 == 0:
            e = layer // 2
            w = _even_weight(even_w_in[e])
            wo = even_w_out[e]
            wa = wo[:512].astype(BF)
            wb = wo[512:].reshape(NSA_HEADS, HEAD_DIM, D)[hperm].reshape(512, D).astype(BF)
            g = hg_norm_g[e].reshape(1, 512)
            pe = nsa_cmp_pe[e].reshape(CMP_BLOCK, LANES)
            hq, hk, hlf, hv, hg, qn, qr, cmp, slc, win, gc, gs, gw = _proj_even(yp, w, hg_gamma, cos_p, sin_p, layer, 256)
            st0 = jnp.zeros((B, HG_HEADS // 2, LANES, LANES), F32)
            o_hg, st = _hgrn(hq, hk, hlf, hv, hg, g, st0, B, T, LANES)
            nsa = _nsa_prompt(qn, qr, cmp, slc, win, gc, gs, gw, pe, B, T)
            mp = (o_hg, nsa)
            kv5 = lambda a, b_, t_: a.reshape(b_, t_, 2, NSA_KV_HEADS, HEAD_DIM)
            outs.setdefault("hg_p", []).append(_pairs_to_state(st))
            outs.setdefault("cmp_p", []).append(kv5(cmp, B, T))
            outs.setdefault("slc_p", []).append(kv5(slc, B, T))
            outs.setdefault("win_p", []).append(kv5(win, B, T)[:, -min(NSA_WINDOW, T):])
            hq, hk, hlf, hv, hg, qn, qr, cmp, slc, win, gc, gs, gw = _proj_even(ys, w, hg_gamma, cos_s, sin_s, layer, 256)
            hk = jnp.where(live, hk, 0.0)
            hlf = jnp.where(live, hlf, 0.0)
            C = 2 * S
            pad = lambda a: _pad_rows(a.reshape(Bs, S, 512), C).reshape(Bs * C, 512)
            o_hg, st = _hgrn(pad(hq), pad(hk), pad(hlf), pad(hv), pad(hg), g,
                             _state_to_pairs(state_hgrn[e].astype(F32)), Bs, C, C)
            o_hg = o_hg.reshape(Bs, C, 512)[:, :S].reshape(Bs * S, 512)
            nsa = _nsa_sample(page_table, qn, qr, slc, win, cache_win_kv[e], gc, gs, gw, nsa_cmp_pe[e],
                              cache_cmp_kv[e], cache_slc_kv[e], Bs, P)
            ms = (o_hg, nsa)
            outs.setdefault("hg_s", []).append(_pairs_to_state(st))
            outs.setdefault("cmp_s", []).append(kv5(cmp, Bs, S)[:, :Ts])
            outs.setdefault("slc_s", []).append(kv5(slc, Bs, S)[:, :Ts])
            outs.setdefault("win_s", []).append(kv5(win, Bs, S)[:, :Ts])
            acol, bcol = 0, 0
        else:
            o = layer // 2
            w = odd_w_in[o].astype(BF)
            wo = odd_w_out[o]
            wa = wo[:512].astype(BF)
            wb = wo[512:].astype(BF)
            q, kv = _proj_odd(yp, w, cos_p, sin_p, 256)
            att = _dil_prompt(q, kv, B, T)
            mp = (att, att)
            kv6 = lambda a, b_, t_: a.reshape(b_, t_, 2, DIL_HEADS, HEAD_DIM)
            outs.setdefault("dil_p", []).append(kv6(kv, B, T)[:, -min(DIL_CONFIGS[-1][0], T):])
            q, kv = _proj_odd(ys, w, cos_s, sin_s, 256)
            att = _dil_sample(q, kv, cache_dil_kv[o], Bs)
            ms = (att, att)
            outs.setdefault("dil_s", []).append(kv6(kv, Bs, S)[:, :Ts])
            acol, bcol = 0, 1
        lg, lbias = row2(ln_mix_g), row2(ln_mix_b)
        yp = _outproj_ln(mp[0], acol, mp[1], bcol, yp, wa, wb, lg, lbias, 256)
        ys = _outproj_ln(ms[0], acol, ms[1], bcol, ys, wa, wb, lg, lbias, 256)
        y = jnp.concatenate([yp, ys.reshape(Bs, S, D)[:, :Ts].reshape(Bs * Ts, D)], 0)
        y = _peer_layer(y, peer_w_q[layer], peer_sub_keys[layer], peer_u[layer], peer_v[layer],
                        row2(ln_ffn_g), row2(ln_ffn_b))
        yp, ys = y[:npr], _pad_rows(y[npr:].reshape(Bs, Ts, D), S).reshape(Bs * S, D)

    stack = lambda k_: jnp.stack(outs[k_])
    return (yp.reshape(B, T, D), ys.reshape(Bs, S, D)[:, :Ts], stack("hg_p"), stack("hg_s"),
            stack("cmp_p"), stack("cmp_s"), stack("slc_p"), stack("slc_s"), stack("win_p"), stack("win_s"),
            stack("dil_p"), stack("dil_s"))
```

```python
import functools
import math

import numpy as np
import jax
import jax.numpy as jnp
from jax import lax
from jax.experimental import pallas as pl
from jax.experimental.pallas import tpu as pltpu

F32 = jnp.float32
BF = jnp.bfloat16

D_MODEL = 1024
HEAD_DIM = 64
LANES = 128
ROPE_THETA = 10000.0
LN_EPS = 1e-5
RMS_EPS = 1e-6
TINY = 1e-30
DEPTH = 2
ALPHA = (2 * DEPTH) ** 0.25
PAGE_SIZE = 128

HG_HEADS = 8
NSA_HEADS = 8
NSA_KV_HEADS = 2
NSA_GROUP = NSA_HEADS // NSA_KV_HEADS
CMP_BLOCK = 32
SEL_BLOCK = 64
SEL_TOPN = 8
SEL_FORCE = 1e4
NSA_WINDOW = 512
DIL_HEADS = 16
DIL_CONFIGS = ((128, 1), (512, 4), (2048, 16))
PEER_HEADS = 8
PEER_NKEYS = 128
PEER_TOPK = 16
SAMPLE_ROWS = 8
QK_SCALE = HEAD_DIM ** -0.5

VMEM_LIMIT = 56 * 1024 * 1024


def _cparams(sem):
    return pltpu.CompilerParams(dimension_semantics=sem, vmem_limit_bytes=VMEM_LIMIT)


def _dot(a, b):
    return jnp.dot(a, b, preferred_element_type=F32)


def _dot_nt(a, b):
    return lax.dot_general(a, b, (((1,), (1,)), ((), ())), preferred_element_type=F32)


def _dot_tn(a, b):
    return lax.dot_general(a, b, (((0,), (0,)), ((), ())), preferred_element_type=F32)


def _iota(shape, dim):
    return lax.broadcasted_iota(jnp.int32, shape, dim)


def _masked_softmax(s, mask, axis):
    s = jnp.where(mask, s, -jnp.inf)
    m = jnp.max(s, axis=axis, keepdims=True)
    m = jnp.where(m > -jnp.inf, m, 0.0)
    e = jnp.exp(s - m)
    l = jnp.sum(e, axis=axis, keepdims=True)
    return m, l, e


def _rope_chunk(x, c, s):
    lane = _iota(x.shape, 1)
    sw = jnp.where((lane & 63) < 32, pltpu.roll(x, 96, 1), pltpu.roll(x, 32, 1))
    return x * c + sw * s


def _rope_tables(pos):
    half = HEAD_DIM // 2
    inv = ROPE_THETA ** (-jnp.arange(half, dtype=F32) / half)
    ang = pos.astype(F32)[:, None] * inv[None, :]
    cos, sin = jnp.cos(ang), jnp.sin(ang)
    return jnp.tile(cos, (1, 4)), jnp.tile(jnp.concatenate([-sin, sin], 1), (1, 2))


def _layer_norm(z, g, b):
    mu = jnp.mean(z, -1, keepdims=True)
    zc = z - mu
    var = jnp.mean(zc * zc, -1, keepdims=True)
    return zc * lax.rsqrt(var + LN_EPS) * g + b


EVEN_COLS = 2048 + 512 + 768 + 1536


def _even_weight(w_in):
    hperm = np.array([[j, NSA_GROUP + j] for j in range(NSA_GROUP)]).reshape(-1)
    qcols = (2048 + hperm[:, None] * HEAD_DIM + np.arange(HEAD_DIM)[None, :]).reshape(-1)
    gate0 = 2048 + 512 + 768
    gcols = []
    for c in range(3):
        for h in hperm:
            kvh, g = divmod(int(h), NSA_GROUP)
            gcols.append(np.full(HEAD_DIM, gate0 + c * NSA_HEADS + kvh * NSA_GROUP + g))
    cols = np.concatenate([np.arange(2048), qcols, np.arange(2560, 3328), np.concatenate(gcols)])
    return jnp.take(w_in, jnp.asarray(cols, jnp.int32), axis=1).astype(BF)


def _proj_even_body(layer, x_ref, w_ref, gam_ref, cos_ref, sin_ref, hq_o, hk_o, hlf_o, hv_o, hg_o,
                    qn_o, qr_o, cmp_o, slc_o, win_o, gc_o, gs_o, gw_o):
    x = x_ref[...].astype(BF)

    def mm(a, b):
        return _dot(x, w_ref[:, a:b])

    gam = gam_ref[...]
    ge = jnp.exp(gam - jnp.max(gam, axis=0, keepdims=True))
    sm = ge / jnp.sum(ge, axis=0, keepdims=True)
    lb = jnp.sum(sm[0:layer + 1], axis=0, keepdims=True)
    c = cos_ref[...]
    s = sin_ref[...]
    hq = mm(0, 512)
    hq_o[...] = hq * jax.nn.sigmoid(hq)
    f = lb + (1.0 - lb) * jax.nn.sigmoid(mm(512, 1024))
    hk_o[...] = 1.0 - f
    hlf_o[...] = jnp.log(f)
    hv_o[...] = mm(1024, 1536)
    hg = mm(1536, 2048)
    hg_o[...] = hg * jax.nn.sigmoid(hg)
    for j in range(4):
        qj = mm(2048 + LANES * j, 2048 + LANES * (j + 1))
        qn_o[:, LANES * j:LANES * (j + 1)] = qj
        qr_o[:, LANES * j:LANES * (j + 1)] = _rope_chunk(qj, c, s)
    cmp_o[...] = mm(2560, 2816)
    slc_o[:, 0:LANES] = _rope_chunk(mm(2816, 2944), c, s)
    slc_o[:, LANES:2 * LANES] = mm(2944, 3072)
    win_o[:, 0:LANES] = _rope_chunk(mm(3072, 3200), c, s)
    win_o[:, LANES:2 * LANES] = mm(3200, 3328)
    gc_o[...] = jax.nn.sigmoid(mm(3328, 3840))
    gs_o[...] = jax.nn.sigmoid(mm(3840, 4352))
    gw_o[...] = jax.nn.sigmoid(mm(4352, 4864))


def _proj_even(x, w, gamma, cos, sin, layer, tm):
    n = x.shape[0]
    widths = [512] * 7 + [256] * 3 + [512] * 3
    row = lambda w_: pl.BlockSpec((tm, w_), lambda i: (i, 0))
    full = lambda a: pl.BlockSpec(a.shape, lambda i: (0,) * a.ndim)
    return pl.pallas_call(
        functools.partial(_proj_even_body, layer),
        grid=(n // tm,),
        in_specs=[row(D_MODEL), full(w), full(gamma), row(LANES), row(LANES)],
        out_specs=[row(w_) for w_ in widths],
        out_shape=[jax.ShapeDtypeStruct((n, w_), F32) for w_ in widths],
        compiler_params=_cparams(("parallel",)),
        name="proj_even",
    )(x, w, gamma, cos, sin)


def _hgrn_consts(C):
    L = int(math.log2(C))
    t = np.arange(C)[:, None]
    i = np.arange(C)[None, :]
    mats = [i <= t]
    bms = []
    for lv in range(L):
        half = 1 << lv
        blk = 2 * half
        mid = (t // blk) * blk + half
        upper = t >= mid
        mats.append(upper & (i >= mid) & (i <= t))
        mats.append((~upper) & (i > t) & (i <= mid - 1))
        bms.append((t // blk) == (i // blk))
    bms.append(t == i)
    sel = np.concatenate(mats, 0).astype(np.float32)
    bm = np.stack(bms).astype(np.float32)
    return jnp.asarray(sel, BF), jnp.asarray(np.concatenate([bm, bm], 1), F32)


def _hgrn_body(C, L, q_ref, k_ref, lf_ref, v_ref, gate_ref, g_ref, s0_ref, sel_ref, bm_ref,
               o_ref, so_ref, st_scr):
    ci = pl.program_id(1)
    npair = HG_HEADS // 2

    @pl.when(ci == 0)
    def _():
        st_scr[...] = s0_ref[0]

    lane = _iota((C, LANES), 1)
    row = _iota((C, LANES), 0)
    lm0 = lane < HEAD_DIM
    same_head = (_iota((LANES, LANES), 0) < HEAD_DIM) == (_iota((LANES, LANES), 1) < HEAD_DIM)

    def split_heads(a):
        return jnp.concatenate([jnp.where(lm0, a, 0.0), jnp.where(lm0, 0.0, a)], 0).astype(BF)

    lf = lf_ref[...]
    hi = lf.astype(BF)
    r1 = lf - hi.astype(F32)
    md = r1.astype(BF)
    lo = (r1 - md.astype(F32)).astype(BF)
    seg3 = _dot(sel_ref[...], jnp.concatenate([hi, md, lo], axis=1))
    w = npair * LANES
    seg_all = seg3[:, 0:w] + seg3[:, w:2 * w] + seg3[:, 2 * w:3 * w]

    for p in range(npair):
        cols = slice(LANES * p, LANES * (p + 1))
        q = q_ref[:, cols]
        k = k_ref[:, cols]
        v = v_ref[:, cols]
        seg = seg_all[:, cols]
        b = seg[0:C]
        a = _dot_nt(split_heads(q), k.astype(BF)) * bm_ref[L]
        for lv in range(L):
            up = ((row >> lv) & 1) == 1
            eu = jnp.where(up, jnp.exp(seg[(1 + 2 * lv) * C:(2 + 2 * lv) * C]), 0.0)
            el = jnp.where(up, 0.0, jnp.exp(seg[(2 + 2 * lv) * C:(3 + 2 * lv) * C]))
            a = a + _dot_nt(split_heads(q * eu), (k * el).astype(BF)) * bm_ref[lv]
        a2 = jnp.concatenate([a[0:C], a[C:2 * C]], axis=1).astype(BF)
        st = st_scr[p]
        o = _dot(a2, split_heads(v)) + _dot_nt((q * jnp.exp(b)).astype(BF), st.astype(BF))

        bend = b[C - 1:C]
        upd = _dot_tn(v.astype(BF), (k * jnp.exp(bend - b)).astype(BF))
        st_new = st * jnp.exp(bend) + jnp.where(same_head, upd, 0.0)
        st_scr[p] = st_new
        so_ref[0, p] = st_new

        o2 = o * o
        ms0 = jnp.sum(jnp.where(lm0, o2, 0.0), axis=1, keepdims=True) * (1.0 / HEAD_DIM)
        ms1 = jnp.sum(jnp.where(lm0, 0.0, o2), axis=1, keepdims=True) * (1.0 / HEAD_DIM)
        ms = jnp.where(lm0, ms0, ms1)
        o_ref[:, cols] = o * lax.rsqrt(ms + RMS_EPS) * g_ref[:, cols] * gate_ref[:, cols]


def _hgrn(q, k, lf, v, gate, g, st0, B, T, C):
    L = int(math.log2(C))
    sel, bm = _hgrn_consts(C)
    nck = T // C
    npair = HG_HEADS // 2
    tok = pl.BlockSpec((C, npair * LANES), lambda b, c: (b * nck + c, 0))
    stspec = pl.BlockSpec((1, npair, LANES, LANES), lambda b, c: (b, 0, 0, 0))
    full = lambda a: pl.BlockSpec(a.shape, lambda b, c: (0,) * a.ndim)
    return pl.pallas_call(
        functools.partial(_hgrn_body, C, L),
        grid=(B, nck),
        in_specs=[tok, tok, tok, tok, tok, full(g), stspec, full(sel), full(bm)],
        out_specs=[tok, stspec],
        out_shape=[jax.ShapeDtypeStruct((B * T, npair * LANES), F32),
                   jax.ShapeDtypeStruct((B, npair, LANES, LANES), F32)],
        scratch_shapes=[pltpu.VMEM((npair, LANES, LANES), F32)],
        compiler_params=_cparams(("parallel", "arbitrary")),
        name="hgrn2",
    )(q, k, lf, v, gate, g, st0, sel, bm)


def _state_to_pairs(s):
    B = s.shape[0]
    st = jnp.swapaxes(s, -1, -2).reshape(B, 4, 2, HEAD_DIM, HEAD_DIM)
    z = jnp.zeros_like(st[:, :, 0])
    top = jnp.concatenate([st[:, :, 0], z], -1)
    bot = jnp.concatenate([z, st[:, :, 1]], -1)
    return jnp.concatenate([top, bot], -2)


def _pairs_to_state(sp):
    B = sp.shape[0]
    a = sp[:, :, :HEAD_DIM, :HEAD_DIM]
    b = sp[:, :, HEAD_DIM:, HEAD_DIM:]
    return jnp.swapaxes(jnp.stack([a, b], 2).reshape(B, HG_HEADS, HEAD_DIM, HEAD_DIM), -1, -2)


def _stack_heads(ref, rows):
    lm0 = _iota((rows, LANES), 1) < HEAD_DIM
    parts = []
    for j in range(NSA_GROUP):
        cj = ref[:, LANES * j:LANES * (j + 1)]
        parts += [jnp.where(lm0, cj, 0.0), jnp.where(lm0, 0.0, cj)]
    return jnp.concatenate(parts, 0)


def _block_means(cmp_ref, pe_ref, kc_scr, vc_scr, nc):
    h = nc // 2
    ck = cmp_ref[:, 0:LANES].reshape(h, 2 * CMP_BLOCK, LANES)
    pe = pe_ref[...][None]
    kc_scr[0:h] = jnp.mean(ck[:, 0:CMP_BLOCK] + pe, axis=1)
    kc_scr[h:nc] = jnp.mean(ck[:, CMP_BLOCK:2 * CMP_BLOCK] + pe, axis=1)
    cv = cmp_ref[:, LANES:2 * LANES].reshape(h, 2 * CMP_BLOCK, LANES)
    vc_scr[0:h] = jnp.mean(cv[:, 0:CMP_BLOCK], axis=1)
    vc_scr[h:nc] = jnp.mean(cv[:, CMP_BLOCK:2 * CMP_BLOCK], axis=1)


def _cmp_and_select(kc, qn_st, q0, nc, nsel_rows, nsel):
    h = nc // 2
    ncol = 2 * NSA_GROUP * LANES
    s = _dot_nt(kc.astype(BF), qn_st)
    r = _iota((nc, ncol), 0)
    cidx = jnp.where(r < h, 2 * r, 2 * (r - h) + 1)
    qpos = q0 + (_iota((nc, ncol), 1) & (LANES - 1))
    avail = (cidx + 1) * CMP_BLOCK - 1 <= qpos
    _, l, e = _masked_softmax(s, avail, 0)
    p = e / jnp.maximum(l, TINY)
    pp = p[0:h] + p[h:nc]
    w2 = 2 * LANES
    imp = pp[:, 0:w2] + pp[:, w2:2 * w2] + pp[:, 2 * w2:3 * w2] + pp[:, 3 * w2:4 * w2]
    if nsel_rows > h:
        imp = jnp.concatenate([imp, jnp.zeros((nsel_rows - h, w2), F32)], 0)
    blk = _iota((nsel_rows, w2), 0)
    qp = q0 + (_iota((nsel_rows, w2), 1) & (LANES - 1))
    forced = (blk == qp // SEL_BLOCK) | (blk == 0)
    imp = jnp.where(forced, SEL_FORCE, jnp.where(blk * SEL_BLOCK <= qp, imp, -1.0))
    imp = jnp.where(blk < nsel, imp, -2.0)
    rank = jnp.zeros((nsel_rows, w2), F32)
    for i in range(nsel):
        ri = imp[i:i + 1, :]
        beats = (ri > imp) | ((ri == imp) & (blk > i))
        rank = rank + jnp.where(beats, 1.0, 0.0)
    sel = jnp.where(rank < float(min(SEL_TOPN, nsel)), 1.0, 0.0)
    return p, sel


def _attend_rows(q_st, k, v, mask):
    s = _dot_nt(q_st, k)
    _, l, e = _masked_softmax(s, mask, 1)
    return _dot(e.astype(BF), v) / jnp.maximum(l, TINY)


def _attend(q_st, k, v, mask, rows):
    o = _attend_rows(q_st, k, v, mask)
    lm0 = _iota((rows, LANES), 1) < HEAD_DIM
    return jnp.where(lm0, o[0:rows], o[rows:2 * rows])


def _nsa_prompt_body(T, qn_ref, qr_ref, cmp_ref, slc_ref, win_ref, gc_ref, gs_ref, gw_ref, pe_ref,
                     e_ref, o_ref, kc_scr, vc_scr):
    qb = pl.program_id(1)
    nc = T // CMP_BLOCK
    nsel = T // SEL_BLOCK
    R = LANES

    @pl.when(qb == 0)
    def _():
        _block_means(cmp_ref, pe_ref, kc_scr, vc_scr, nc)

    q0 = qb * R
    qn_st = (_stack_heads(qn_ref, R) * QK_SCALE).astype(BF)
    qr_st = (_stack_heads(qr_ref, R) * QK_SCALE).astype(BF)
    p, sel = _cmp_and_select(kc_scr[...], qn_st, q0, nc, nsel, nsel)
    oc = _dot(p.T.astype(BF), vc_scr[...].astype(BF))

    sel_e = _dot_tn(sel.astype(BF), e_ref[...])
    qpos = q0 + (_iota((2 * R, T), 0) & (R - 1))
    mask_s = (sel_e > 0.5) & (_iota((2 * R, T), 1) <= qpos)
    ks = slc_ref[:, 0:LANES].astype(BF)
    vs = slc_ref[:, LANES:2 * LANES].astype(BF)

    nw = NSA_WINDOW + R
    start = pl.multiple_of(jnp.maximum(qb - NSA_WINDOW // R, 0) * R, R)
    kw = win_ref[pl.ds(start, nw), 0:LANES].astype(BF)
    vw = win_ref[pl.ds(start, nw), LANES:2 * LANES].astype(BF)
    dist = q0 + (_iota((2 * R, nw), 0) & (R - 1)) - (start + _iota((2 * R, nw), 1))
    mask_w = (dist >= 0) & (dist <= NSA_WINDOW)

    lm0 = _iota((R, LANES), 1) < HEAD_DIM
    for j in range(NSA_GROUP):
        cols = slice(LANES * j, LANES * (j + 1))
        qj = qr_st[2 * R * j:2 * R * (j + 1)]
        o_s = _attend(qj, ks, vs, mask_s, R)
        o_w = _attend(qj, kw, vw, mask_w, R)
        o_c = jnp.where(lm0, oc[2 * R * j:2 * R * j + R], oc[2 * R * j + R:2 * R * (j + 1)])
        o_ref[:, cols] = gc_ref[:, cols] * o_c + gs_ref[:, cols] * o_s + gw_ref[:, cols] * o_w


def _sel_expand(nrows, nkeys):
    e = (np.arange(nkeys)[None, :] // SEL_BLOCK) == np.arange(nrows)[:, None]
    return jnp.asarray(e.astype(np.float32), BF)


def _nsa_prompt(qn, qr, cmp, slc, win, gc, gs, gw, pe, B, T):
    R = LANES
    nq = T // R
    tok = pl.BlockSpec((R, 512), lambda b, i: (b * nq + i, 0))
    seq = pl.BlockSpec((T, 256), lambda b, i: (b, 0))
    e = _sel_expand(T // SEL_BLOCK, T)
    full = lambda a: pl.BlockSpec(a.shape, lambda b, i: (0,) * a.ndim)
    return pl.pallas_call(
        functools.partial(_nsa_prompt_body, T),
        grid=(B, nq),
        in_specs=[tok, tok, seq, seq, seq, tok, tok, tok, full(pe), full(e)],
        out_specs=tok,
        out_shape=jax.ShapeDtypeStruct((B * T, 512), F32),
        scratch_shapes=[pltpu.VMEM((T // CMP_BLOCK, LANES), F32), pltpu.VMEM((T // CMP_BLOCK, LANES), F32)],
        compiler_params=_cparams(("parallel", "arbitrary")),
        name="nsa_prompt",
    )(qn, qr, cmp, slc, win, gc, gs, gw, pe, e)


def _attend_two_part(q_st, kt, vt, mask_p, k_new, v_new, mask_n):
    sp = jnp.where(mask_p, _dot(q_st, kt), -jnp.inf)
    sn = jnp.where(mask_n, _dot_nt(q_st, k_new), -jnp.inf)
    m = jnp.maximum(jnp.max(sp, axis=1, keepdims=True), jnp.max(sn, axis=1, keepdims=True))
    m = jnp.where(m > -jnp.inf, m, 0.0)
    ep = jnp.exp(sp - m)
    en = jnp.exp(sn - m)
    l = jnp.sum(ep, axis=1, keepdims=True) + jnp.sum(en, axis=1, keepdims=True)
    return (_dot_nt(ep.astype(BF), vt) + _dot(en.astype(BF), v_new)) / jnp.maximum(l, TINY)


def _nsa_sample_body(P, npages, *refs):
    (qn_ref, qr_ref, slcn_ref, winn_ref, wint_ref, gc_ref, gs_ref, gw_ref, pet_ref, avg_ref, e_ref) = refs[1:12]
    cmp_pages = refs[12:12 + npages]
    slc_pages = refs[12 + npages:12 + 2 * npages]
    o_ref = refs[12 + 2 * npages]
    cmp_kt, cmp_vt, slc_kt, slc_vt = refs[13 + 2 * npages:]
    S = SAMPLE_ROWS
    nc = P // CMP_BLOCK
    nsel = -(-(P + 4) // SEL_BLOCK)
    nsel_rows = e_ref.shape[0]
    lw = wint_ref.shape[4]
    pair = lambda ref, kv: jnp.concatenate([ref[0, kv, 0], ref[0, kv, 1]], 0)
    for pg in range(npages):
        cols = slice(PAGE_SIZE * pg, PAGE_SIZE * (pg + 1))
        cmp_kt[:, cols] = pair(cmp_pages[pg], 0)
        cmp_vt[:, cols] = pair(cmp_pages[pg], 1)
        slc_kt[:, cols] = pair(slc_pages[pg], 0)
        slc_vt[:, cols] = pair(slc_pages[pg], 1)

    def block_means(x):
        hi = x.astype(BF)
        r1 = x - hi.astype(F32)
        md = r1.astype(BF)
        lo = (r1 - md.astype(F32)).astype(BF)
        a = avg_ref[...]
        return _dot(hi, a) + _dot(md, a) + _dot(lo, a)

    kc = block_means(cmp_kt[...] + pet_ref[...]).T
    vc = block_means(cmp_vt[...]).T

    lm0s = _iota((S, LANES), 1) < HEAD_DIM
    zpad = jnp.zeros((LANES - S, LANES), F32)
    qn_parts, qr_parts = [], []
    for j in range(NSA_GROUP):
        cn = qn_ref[:, LANES * j:LANES * (j + 1)]
        cr = qr_ref[:, LANES * j:LANES * (j + 1)]
        qn_parts += [jnp.where(lm0s, cn, 0.0), zpad, jnp.where(lm0s, 0.0, cn), zpad]
        qr_parts += [jnp.where(lm0s, cr, 0.0), jnp.where(lm0s, 0.0, cr)]
    qn_st = (jnp.concatenate(qn_parts, 0) * QK_SCALE).astype(BF)
    qr_st = (jnp.concatenate(qr_parts, 0) * QK_SCALE).astype(BF)
    p, sel = _cmp_and_select(kc, qn_st, P, nc, nsel_rows, nsel)
    pt = p.T
    pc = jnp.concatenate([pt[LANES * i:LANES * i + S] for i in range(2 * NSA_GROUP)], 0)
    o_c = _dot(pc.astype(BF), vc.astype(BF))

    R = 2 * NSA_GROUP * S
    pad16 = lambda a: jnp.concatenate([a, jnp.zeros(a.shape, F32)], 0).astype(BF)
    mask_n = _iota((R, 2 * S), 1) <= (_iota((R, 2 * S), 0) & (S - 1))
    sel_e = _dot_tn(sel.astype(BF), e_ref[...])
    sel_c = jnp.concatenate([sel_e[0:S], sel_e[LANES:LANES + S]] * NSA_GROUP, 0)
    o_s = _attend_two_part(qr_st, slc_kt[...].astype(BF), slc_vt[...].astype(BF), sel_c > 0.5,
                           pad16(slcn_ref[:, 0:LANES]), pad16(slcn_ref[:, LANES:2 * LANES]), mask_n)
    dist = (_iota((R, lw), 0) & (S - 1)) + lw - _iota((R, lw), 1)
    o_w = _attend_two_part(qr_st, pair(wint_ref, 0).astype(BF), pair(wint_ref, 1).astype(BF), dist <= NSA_WINDOW,
                           pad16(winn_ref[:, 0:LANES]), pad16(winn_ref[:, LANES:2 * LANES]), mask_n)
    for j in range(NSA_GROUP):
        cols = slice(LANES * j, LANES * (j + 1))
        r0 = slice(2 * S * j, 2 * S * j + S)
        r1 = slice(2 * S * j + S, 2 * S * (j + 1))
        pick = lambda a: jnp.where(lm0s, a[r0], a[r1])
        o_ref[:, cols] = gc_ref[:, cols] * pick(o_c) + gs_ref[:, cols] * pick(o_s) + gw_ref[:, cols] * pick(o_w)


def _nsa_sample(page_table, qn, qr, slc_new, win_new, win_cache, gc, gs, gw, pe, cmp_pool, slc_pool, B, P):
    S = SAMPLE_ROWS
    npages = P // PAGE_SIZE
    nc = P // CMP_BLOCK
    nsel_rows = 8 * (-(-(-(-(P + 4) // SEL_BLOCK)) // 8))
    e = _sel_expand(nsel_rows, P)
    rows_last = lambda a: jnp.transpose(a, (0, 2, 3, 4, 1))
    pet = jnp.tile(pe.reshape(CMP_BLOCK, LANES).T, (1, nc))
    blk = np.arange(P) // CMP_BLOCK
    col = np.where(blk % 2 == 0, blk // 2, nc // 2 + blk // 2)
    avg = jnp.asarray((col[:, None] == np.arange(nc)[None, :]).astype(np.float32) / CMP_BLOCK, BF)
    lw = win_cache.shape[1]
    tok = lambda w_: pl.BlockSpec((S, w_), lambda b, pt: (b, 0))
    full = lambda a: pl.BlockSpec(a.shape, lambda b, pt: (0,) * a.ndim)
    page = lambda pg: pl.BlockSpec((1, 2, NSA_KV_HEADS, HEAD_DIM, PAGE_SIZE),
                                   lambda b, pt: (pt[b * npages + pg], 0, 0, 0, 0))
    in_specs = ([tok(512), tok(512), tok(256), tok(256),
                 pl.BlockSpec((1, 2, NSA_KV_HEADS, HEAD_DIM, lw), lambda b, pt: (b, 0, 0, 0, 0)),
                 tok(512), tok(512), tok(512), full(pet), full(avg), full(e)]
                + [page(pg) for pg in range(npages)] * 2)
    gs_ = pltpu.PrefetchScalarGridSpec(
        num_scalar_prefetch=1, grid=(B,), in_specs=in_specs, out_specs=tok(512),
        scratch_shapes=[pltpu.VMEM((LANES, P), F32)] * 4)
    return pl.pallas_call(
        functools.partial(_nsa_sample_body, P, npages),
        grid_spec=gs_,
        out_shape=jax.ShapeDtypeStruct((B * S, 512), F32),
        compiler_params=_cparams(("arbitrary",)),
        name="nsa_sample",
    )(page_table.reshape(-1), qn, qr, slc_new, win_new, rows_last(win_cache), gc, gs, gw, pet, avg, e,
      *([rows_last(cmp_pool)] * npages), *([rows_last(slc_pool)] * npages))


def _outproj_ln_body(a_ref, b_ref, x_ref, wa_ref, wb_ref, g_ref, bb_ref, y_ref):
    mix = _dot(a_ref[...].astype(BF), wa_ref[...]) + _dot(b_ref[...].astype(BF), wb_ref[...])
    y_ref[...] = _layer_norm(ALPHA * x_ref[...] + mix, g_ref[...], bb_ref[...])


def _outproj_ln(a, acol, b, bcol, x, wa, wb, g, bb, tm):
    n = x.shape[0]
    full = lambda t: pl.BlockSpec(t.shape, lambda i: (0,) * t.ndim)
    return pl.pallas_call(
        _outproj_ln_body,
        grid=(n // tm,),
        in_specs=[pl.BlockSpec((tm, 512), lambda i: (i, acol)), pl.BlockSpec((tm, 512), lambda i: (i, bcol)),
                  pl.BlockSpec((tm, D_MODEL), lambda i: (i, 0)), full(wa), full(wb), full(g), full(bb)],
        out_specs=pl.BlockSpec((tm, D_MODEL), lambda i: (i, 0)),
        out_shape=jax.ShapeDtypeStruct((n, D_MODEL), F32),
        compiler_params=_cparams(("parallel",)),
        name="outproj_ln",
    )(a, b, x, wa, wb, g, bb)


def _proj_odd_body(x_ref, w_ref, cos_ref, sin_ref, q_o, kv_o):
    x = x_ref[...].astype(BF)
    c = cos_ref[...]
    s = sin_ref[...]
    nchunk = DIL_HEADS * HEAD_DIM // LANES
    for j in range(nchunk):
        cols = slice(LANES * j, LANES * (j + 1))
        q_o[:, cols] = _rope_chunk(_dot(x, w_ref[:, cols]), c, s)
        kv_o[:, cols] = _rope_chunk(_dot(x, w_ref[:, D_MODEL + LANES * j:D_MODEL + LANES * (j + 1)]), c, s)
    kv_o[:, D_MODEL:2 * D_MODEL] = _dot(x, w_ref[:, 2 * D_MODEL:3 * D_MODEL])


def _proj_odd(x, w, cos, sin, tm):
    n = x.shape[0]
    row = lambda w_: pl.BlockSpec((tm, w_), lambda i: (i, 0))
    return pl.pallas_call(
        _proj_odd_body,
        grid=(n // tm,),
        in_specs=[row(D_MODEL), pl.BlockSpec(w.shape, lambda i: (0, 0)), row(LANES), row(LANES)],
        out_specs=[row(D_MODEL), row(2 * D_MODEL)],
        out_shape=[jax.ShapeDtypeStruct((n, D_MODEL), F32), jax.ShapeDtypeStruct((n, 2 * D_MODEL), F32)],
        compiler_params=_cparams(("parallel",)),
        name="proj_odd",
    )(x, w, cos, sin)


def _dil_prompt_body(T, q_ref, k_ref, v_ref, o_ref, acc_scr, m_scr, l_scr):
    R = LANES
    lm0 = _iota((R, LANES), 1) < HEAD_DIM
    for ci, (window, d) in enumerate(DIL_CONFIGS):
        band = window // d
        nblk = T // d // R
        for r in range(d):
            for i in range(nblk):
                q0 = r + d * R * i
                rows_q = pl.ds(q0, R, stride=d) if d > 1 else pl.ds(q0, R)
                if i > 0:
                    k0, nk = q0 - d * R, 2 * R
                else:
                    k0, nk = q0, R
                rows_k = pl.ds(k0, nk, stride=d) if d > 1 else pl.ds(k0, nk)
                qs = q_ref[rows_q, :] * QK_SCALE
                q_st = jnp.concatenate([jnp.where(lm0, qs, 0.0), jnp.where(lm0, 0.0, qs)], 0).astype(BF)
                ks = k_ref[rows_k, :].astype(BF)
                vs = v_ref[rows_k, :].astype(BF)
                dist = (_iota((2 * R, nk), 0) & (R - 1)) + (nk - R) - _iota((2 * R, nk), 1)
                s = _dot_nt(q_st, ks)
                m, l, e = _masked_softmax(s, (dist >= 0) & (dist <= band), 1)
                acc = _dot(e.astype(BF), vs)
                acc_scr[ci, rows_q, :] = jnp.where(lm0, acc[0:R], acc[R:2 * R])
                m_scr[ci, rows_q, :] = jnp.where(lm0, m[0:R], m[R:2 * R])
                l_scr[ci, rows_q, :] = jnp.where(lm0, l[0:R], l[R:2 * R])
    ncfg = len(DIL_CONFIGS)
    mx = m_scr[0]
    for ci in range(1, ncfg):
        mx = jnp.maximum(mx, m_scr[ci])
    num = jnp.zeros((T, LANES), F32)
    den = jnp.zeros((T, LANES), F32)
    for ci in range(ncfg):
        w = jnp.exp(m_scr[ci] - mx)
        num = num + w * acc_scr[ci]
        den = den + w * l_scr[ci]
    o_ref[...] = num / den


def _dil_prompt(q, kv, B, T):
    npair = DIL_HEADS // 2
    ncfg = len(DIL_CONFIGS)
    return pl.pallas_call(
        functools.partial(_dil_prompt_body, T),
        grid=(B, npair),
        in_specs=[pl.BlockSpec((T, LANES), lambda b, p: (b, p)),
                  pl.BlockSpec((T, LANES), lambda b, p: (b, p)),
                  pl.BlockSpec((T, LANES), lambda b, p: (b, npair + p))],
        out_specs=pl.BlockSpec((T, LANES), lambda b, p: (b, p)),
        out_shape=jax.ShapeDtypeStruct((B * T, D_MODEL), F32),
        scratch_shapes=[pltpu.VMEM((ncfg, T, LANES), F32)] * 3,
        compiler_params=_cparams(("parallel", "parallel")),
        name="dil_prompt",
    )(q, kv, kv)


def _dil_sample_body(L, q_ref, kvn_ref, kt_ref, vt_ref, o_ref):
    S = SAMPLE_ROWS
    H = DIL_HEADS
    R = H * S
    pad16 = lambda a: jnp.concatenate([a, jnp.zeros(a.shape, F32)], 0).astype(BF)
    sb, sn = [], []
    for h in range(H):
        q16 = pad16(q_ref[h] * QK_SCALE)
        sb.append(_dot(q16, kt_ref[0, 0, h].astype(BF))[0:S])
        sn.append(_dot_nt(q16, pad16(kvn_ref[h]))[0:S])
    parts = [
        (jnp.concatenate(sb, 0), L + (_iota((R, L), 0) & (S - 1)) - _iota((R, L), 1), False),
        (jnp.concatenate(sn, 0), (_iota((R, 2 * S), 0) & (S - 1)) - _iota((R, 2 * S), 1), True),
    ]
    stats = []
    for window, d in DIL_CONFIGS:
        masked = []
        for s, dist, signed in parts:
            ok = (dist <= window) & ((dist & (d - 1)) == 0)
            if signed:
                ok = ok & (dist >= 0)
            masked.append(jnp.where(ok, s, -jnp.inf))
        m = functools.reduce(jnp.maximum, [jnp.max(x, axis=1, keepdims=True) for x in masked])
        m = jnp.where(m > -jnp.inf, m, 0.0)
        es = [jnp.exp(x - m) for x in masked]
        l = functools.reduce(lambda a, b: a + b, [jnp.sum(e, axis=1, keepdims=True) for e in es])
        stats.append((m, l, es))
    mx = functools.reduce(jnp.maximum, [st[0] for st in stats])
    ws = [jnp.exp(m - mx) for m, _, _ in stats]
    inv = 1.0 / functools.reduce(lambda a, b: a + b, [w * l for w, (_, l, _) in zip(ws, stats)])
    pb, pn = [functools.reduce(lambda a, b: a + b, [w * es[pi] for w, (_, _, es) in zip(ws, stats)]) * inv
              for pi in range(2)]
    for h in range(H):
        rows = slice(S * h, S * (h + 1))
        o = _dot_nt(pad16(pb[rows]), vt_ref[0, 0, h].astype(BF)) + _dot(pad16(pn[rows]), pad16(kvn_ref[H + h]))
        o_ref[h] = o[0:S]


def _dil_sample(q, kv_new, cache, B):
    S = SAMPLE_ROWS
    L, H, Dh = cache.shape[1], cache.shape[3], cache.shape[4]
    n = B * S
    cache_t = jnp.transpose(cache, (0, 2, 3, 4, 1))
    q_h = q.reshape(n, H, Dh).transpose(1, 0, 2)
    kv_h = kv_new.reshape(n, 2 * H, Dh).transpose(1, 0, 2)
    part = lambda j: pl.BlockSpec((1, 1, H, Dh, L), lambda b: (b, j, 0, 0, 0))
    o = pl.pallas_call(
        functools.partial(_dil_sample_body, L),
        grid=(B,),
        in_specs=[pl.BlockSpec((H, S, Dh), lambda b: (0, b, 0)),
                  pl.BlockSpec((2 * H, S, Dh), lambda b: (0, b, 0)),
                  part(0), part(1)],
        out_specs=pl.BlockSpec((H, S, Dh), lambda b: (0, b, 0)),
        out_shape=jax.ShapeDtypeStruct((H, n, Dh), F32),
        compiler_params=_cparams(("parallel",)),
        name="dil_sample",
    )(q_h, kv_h, cache_t, cache_t)
    return o.transpose(1, 0, 2).reshape(n, H * Dh)


def _pop_max(work, iota_k, nrow):
    m = jnp.max(work, axis=0, keepdims=True)
    idx = jnp.min(jnp.where(work == m, iota_k, float(nrow)), axis=0, keepdims=True)
    return m, idx, jnp.where(iota_k == idx, -jnp.inf, work)


def _peer_route(s1, s2, tn, exact_ties):
    K = PEER_TOPK
    NK = PEER_NKEYS
    iota_k = _iota((NK, tn), 0).astype(F32)
    iota_r = _iota((K, tn), 0)

    w1_, w2_ = s1, s2
    v1, i1, i2 = [], [], []
    v2 = jnp.zeros((K, tn), F32)
    for it in range(K):
        if exact_ties:
            m1, idx1, w1_ = _pop_max(w1_, iota_k, NK)
            m2, idx2, w2_ = _pop_max(w2_, iota_k, NK)
            i1.append(idx1)
            i2.append(idx2)
        else:
            m1 = jnp.max(w1_, axis=0, keepdims=True)
            w1_ = jnp.where(w1_ == m1, -jnp.inf, w1_)
            m2 = jnp.max(w2_, axis=0, keepdims=True)
            w2_ = jnp.where(w2_ == m2, -jnp.inf, w2_)
        v1.append(m1)
        v2 = jnp.where(iota_r == it, m2, v2)
    if exact_ties:
        tied = jnp.zeros((1, tn), F32)
    else:
        gone = lambda w: jnp.sum(jnp.where(w == -jnp.inf, 1.0, 0.0), axis=0, keepdims=True)
        tied = jnp.where((gone(w1_) != float(K)) | (gone(w2_) != float(K)), 1.0, 0.0)

    rows = [K] + [8] * (K - 1)
    pieces = []
    for r1 in range(K):
        piece = v1[r1] + v2[0:rows[r1]]
        nvalid = K // (r1 + 1)
        if nvalid < rows[r1]:
            piece = jnp.where(_iota((rows[r1], tn), 0) < nvalid, piece, -jnp.inf)
        pieces.append(piece)
    cand = jnp.concatenate(pieces, 0)
    nc = cand.shape[0]
    iota_c = _iota((nc, tn), 0).astype(F32)
    work = cand
    for it in range(K):
        _, _, work = _pop_max(work, iota_c, nc)
    sel = (work == -jnp.inf) & (cand > -jnp.inf)
    z = jnp.sum(jnp.where(sel, jnp.exp(cand - cand[0:1]), 0.0), axis=0, keepdims=True)
    self_ = jnp.where(sel, 1.0, 0.0)

    r2 = jnp.full((NK, tn), float(K), F32) if exact_ties else jnp.zeros((NK, tn), F32)
    cntk = jnp.zeros((NK, tn), F32)
    off = 0
    for r in range(K):
        cnt_r = jnp.sum(self_[off:off + rows[r]], axis=0, keepdims=True)
        off += rows[r]
        if exact_ties:
            cntk = jnp.where(iota_k == i1[r], cnt_r, cntk)
            r2 = jnp.where(iota_k == i2[r], float(r), r2)
        else:
            cntk = jnp.where(s1 == v1[r], cnt_r, cntk)
            r2 = r2 + jnp.where(v2[r:r + 1] > s2, 1.0, 0.0)
    return r2.astype(BF), jnp.exp(s2 - v2[0:1]).astype(BF), cntk, jnp.exp(s1 - v1[0]) / z, tied


def _peer_topk_body(y_ref, wq_ref, keys_ref, r2_o, e2_o, cnt_o, w1_o):
    tn = y_ref.shape[0]
    q = _dot(y_ref[...].astype(BF), wq_ref[...]).astype(BF)
    s1 = _dot_nt(keys_ref[0, 0], q[:, 0:LANES])
    s2 = _dot_nt(keys_ref[0, 1], q[:, LANES:2 * LANES])

    def emit(exact_ties):
        r2, e2, cnt, w1, tied = _peer_route(s1, s2, tn, exact_ties)
        r2_o[0] = r2
        e2_o[0] = e2
        cnt_o[0] = cnt
        w1_o[0] = w1
        return tied

    tied = emit(False)

    @pl.when(jnp.max(tied) > 0.0)
    def _():
        emit(True)


def _peer_topk(y, wq, keys, tn):
    n = y.shape[0]
    out = pl.BlockSpec((1, PEER_NKEYS, tn), lambda i, h: (h, 0, i))
    shp = lambda dt: jax.ShapeDtypeStruct((PEER_HEADS, PEER_NKEYS, n), dt)
    return pl.pallas_call(
        _peer_topk_body,
        grid=(n // tn, PEER_HEADS),
        in_specs=[pl.BlockSpec((tn, D_MODEL), lambda i, h: (i, 0)),
                  pl.BlockSpec((D_MODEL, 2 * LANES), lambda i, h: (0, h)),
                  pl.BlockSpec((1, 2, PEER_NKEYS, LANES), lambda i, h: (h, 0, 0, 0))],
        out_specs=[out] * 4,
        out_shape=[shp(BF), shp(BF), shp(F32), shp(F32)],
        compiler_params=_cparams(("parallel", "arbitrary")),
        name="peer_topk",
    )(y, wq, keys)


def _gelu(x):
    return 0.5 * x * (1.0 + lax.erf(x * math.sqrt(0.5)))


def _peer_main_body(npe, xt_ref, u_ref, vt_ref, r2_ref, e2_ref, cnt_ref, w1_ref, yt_ref):
    e = pl.program_id(1)
    tn = xt_ref.shape[1]

    @pl.when(e == 0)
    def _():
        yt_ref[...] = jnp.zeros(yt_ref.shape, F32)

    xt = xt_ref[...]
    sub = 2 * PEER_NKEYS
    acc = yt_ref[...]
    nsub = npe // 2
    act = lambda s: _gelu(_dot(u_ref[sub * s:sub * (s + 1), :], xt).astype(BF))
    ahead = 3
    acts = [act(s) for s in range(min(ahead, nsub))]
    for s in range(nsub):
        if s + ahead < nsub:
            acts.append(act(s + ahead))
        a = acts[s]
        parts = []
        for cc in range(2):
            c = e * npe + 2 * s + cc
            g = jnp.zeros((PEER_NKEYS, tn), BF)
            for h in range(PEER_HEADS):
                cnt_row = cnt_ref[h, pl.ds(c, 1), :].astype(BF)
                w_row = w1_ref[h, pl.ds(c, 1), :].astype(BF)
                g = g + jnp.where(r2_ref[h] < cnt_row, e2_ref[h] * w_row, jnp.zeros((), BF))
            parts.append(g * a[PEER_NKEYS * cc:PEER_NKEYS * (cc + 1)].astype(BF))
        acc = acc + _dot(vt_ref[:, sub * s:sub * (s + 1)], jnp.concatenate(parts, 0))
    yt_ref[...] = acc


def _peer_main(xt, u, vt, r2, e2, cnt, w1, tn, te):
    n = xt.shape[1]
    ne = u.shape[0]
    npe = te // PEER_NKEYS
    tab = pl.BlockSpec((PEER_HEADS, PEER_NKEYS, tn), lambda i, e: (0, 0, i))
    return pl.pallas_call(
        functools.partial(_peer_main_body, npe),
        grid=(n // tn, ne // te),
        in_specs=[pl.BlockSpec((D_MODEL, tn), lambda i, e: (0, i)),
                  pl.BlockSpec((te, D_MODEL), lambda i, e: (e, 0)),
                  pl.BlockSpec((D_MODEL, te), lambda i, e: (0, e)),
                  tab, tab, tab, tab],
        out_specs=pl.BlockSpec((D_MODEL, tn), lambda i, e: (0, i)),
        out_shape=jax.ShapeDtypeStruct((D_MODEL, n), F32),
        compiler_params=_cparams(("parallel", "arbitrary")),
        name="peer_main",
    )(xt, u, vt, r2, e2, cnt, w1)


def _ln_t_body(x_ref, ft_ref, g_ref, b_ref, y_ref):
    y_ref[...] = _layer_norm(ALPHA * x_ref[...] + ft_ref[...].T, g_ref[...], b_ref[...])


def _ln_t(x, ft, g, b, tn):
    n = x.shape[0]
    full = lambda t: pl.BlockSpec(t.shape, lambda i: (0,) * t.ndim)
    return pl.pallas_call(
        _ln_t_body,
        grid=(n // tn,),
        in_specs=[pl.BlockSpec((tn, D_MODEL), lambda i: (i, 0)), pl.BlockSpec((D_MODEL, tn), lambda i: (0, i)),
                  full(g), full(b)],
        out_specs=pl.BlockSpec((tn, D_MODEL), lambda i: (i, 0)),
        out_shape=jax.ShapeDtypeStruct((n, D_MODEL), F32),
        compiler_params=_cparams(("parallel",)),
        name="ln_residual",
    )(x, ft, g, b)


def _peer_layer(y, wq, keys, u, v, g, b):
    r2, e2, cnt, w1 = _peer_topk(y, wq.astype(BF), keys.astype(BF), LANES)
    ft = _peer_main(y.T.astype(BF), u.astype(BF), v.T.astype(BF), r2, e2, cnt, w1, 512, 2048)
    return _ln_t(y, ft, g, b, 256)


def _pad_rows(a, S):
    return jnp.pad(a, ((0, 0), (0, S - a.shape[1])) + ((0, 0),) * (a.ndim - 2))


def kernel(x_prompt, x_sample, state_hgrn, cache_cmp_kv, cache_slc_kv, cache_win_kv, cache_dil_kv, page_table,
           hg_gamma, even_w_in, even_w_out, hg_norm_g, nsa_cmp_pe, odd_w_in, odd_w_out, ln_mix_g, ln_mix_b,
           peer_w_q, peer_sub_keys, peer_u, peer_v, ln_ffn_g, ln_ffn_b):
    B, T, D = x_prompt.shape
    Bs, Ts, _ = x_sample.shape
    S = SAMPLE_ROWS
    P = page_table.shape[1] * PAGE_SIZE
    npr = B * T
    yp = x_prompt.reshape(npr, D)
    ys = _pad_rows(x_sample, S).reshape(Bs * S, D)
    cos_p, sin_p = _rope_tables(jnp.tile(jnp.arange(T), B))
    cos_s, sin_s = _rope_tables(jnp.tile(P + jnp.arange(S), Bs))
    live = (jnp.arange(Bs * S) % S < Ts)[:, None]
    hperm = np.array([[j, NSA_GROUP + j] for j in range(NSA_GROUP)]).reshape(-1)
    outs = {}

    for layer in range(DEPTH):
        row2 = lambda a: a[layer].reshape(1, D)
        if layer % 2 == 0:
            e = layer // 2
            w = _even_weight(even_w_in[e])
            wo = even_w_out[e]
            wa = wo[:512].astype(BF)
            wb = wo[512:].reshape(NSA_HEADS, HEAD_DIM, D)[hperm].reshape(512, D).astype(BF)
            g = hg_norm_g[e].reshape(1, 512)
            pe = nsa_cmp_pe[e].reshape(CMP_BLOCK, LANES)
            hq, hk, hlf, hv, hg, qn, qr, cmp, slc, win, gc, gs, gw = _proj_even(yp, w, hg_gamma, cos_p, sin_p, layer, 256)
            st0 = jnp.zeros((B, HG_HEADS // 2, LANES, LANES), F32)
            o_hg, st = _hgrn(hq, hk, hlf, hv, hg, g, st0, B, T, LANES)
            nsa = _nsa_prompt(qn, qr, cmp, slc, win, gc, gs, gw, pe, B, T)
            mp = (o_hg, nsa)
            kv5 = lambda a, b_, t_: a.reshape(b_, t_, 2, NSA_KV_HEADS, HEAD_DIM)
            outs.setdefault("hg_p", []).append(_pairs_to_state(st))
            outs.setdefault("cmp_p", []).append(kv5(cmp, B, T))
            outs.setdefault("slc_p", []).append(kv5(slc, B, T))
            outs.setdefault("win_p", []).append(kv5(win, B, T)[:, -min(NSA_WINDOW, T):])
            hq, hk, hlf, hv, hg, qn, qr, cmp, slc, win, gc, gs, gw = _proj_even(ys, w, hg_gamma, cos_s, sin_s, layer, 256)
            hk = jnp.where(live, hk, 0.0)
            hlf = jnp.where(live, hlf, 0.0)
            C = 2 * S
            pad = lambda a: _pad_rows(a.reshape(Bs, S, 512), C).reshape(Bs * C, 512)
            o_hg, st = _hgrn(pad(hq), pad(hk), pad(hlf), pad(hv), pad(hg), g,
                             _state_to_pairs(state_hgrn[e].astype(F32)), Bs, C, C)
            o_hg = o_hg.reshape(Bs, C, 512)[:, :S].reshape(Bs * S, 512)
            nsa = _nsa_sample(page_table, qn, qr, slc, win, cache_win_kv[e], gc, gs, gw, nsa_cmp_pe[e],
                              cache_cmp_kv[e], cache_slc_kv[e], Bs, P)
            ms = (o_hg, nsa)
            outs.setdefault("hg_s", []).append(_pairs_to_state(st))
            outs.setdefault("cmp_s", []).append(kv5(cmp, Bs, S)[:, :Ts])
            outs.setdefault("slc_s", []).append(kv5(slc, Bs, S)[:, :Ts])
            outs.setdefault("win_s", []).append(kv5(win, Bs, S)[:, :Ts])
            acol, bcol = 0, 0
        else:
            o = layer // 2
            w = odd_w_in[o].astype(BF)
            wo = odd_w_out[o]
            wa = wo[:512].astype(BF)
            wb = wo[512:].astype(BF)
            q, kv = _proj_odd(yp, w, cos_p, sin_p, 256)
            att = _dil_prompt(q, kv, B, T)
            mp = (att, att)
            kv6 = lambda a, b_, t_: a.reshape(b_, t_, 2, DIL_HEADS, HEAD_DIM)
            outs.setdefault("dil_p", []).append(kv6(kv, B, T)[:, -min(DIL_CONFIGS[-1][0], T):])
            q, kv = _proj_odd(ys, w, cos_s, sin_s, 256)
            att = _dil_sample(q, kv, cache_dil_kv[o], Bs)
            ms = (att, att)
            outs.setdefault("dil_s", []).append(kv6(kv, Bs, S)[:, :Ts])
            acol, bcol = 0, 1
        lg, lbias = row2(ln_mix_g), row2(ln_mix_b)
        yp = _outproj_ln(mp[0], acol, mp[1], bcol, yp, wa, wb, lg, lbias, 256)
        ys = _outproj_ln(ms[0], acol, ms[1], bcol, ys, wa, wb, lg, lbias, 256)
        y = jnp.concatenate([yp, ys.reshape(Bs, S, D)[:, :Ts].reshape(Bs * Ts, D)], 0)
        y = _peer_layer(y, peer_w_q[layer], peer_sub_keys[layer], peer_u[layer], peer_v[layer],
                        row2(ln_ffn_g), row2(ln_ffn_b))
        yp, ys = y[:npr], _pad_rows(y[npr:].reshape(Bs, Ts, D), S).reshape(Bs * S, D)

    stack = lambda k_: jnp.stack(outs[k_])
    return (yp.reshape(B, T, D), ys.reshape(Bs, S, D)[:, :Ts], stack("hg_p"), stack("hg_s"),
            stack("cmp_p"), stack("cmp_s"), stack("slc_p"), stack("slc_s"), stack("win_p"), stack("win_s"),
            stack("dil_p"), stack("dil_s"))
```
